```python
import math
import jax
import jax.numpy as jnp
from jax import lax
import numpy as np

D_MODEL = 2048
BATCH = 8
SEQ = 2048
DEPTH = 4

GRID_W = 64
CTX_LEN = 256
HEAD_DIM = 128
ROPE_THETA = 10000.0
LN_EPS = 1e-5
NEG_INF = -1e30
N_MIXERS = 4
DEEPNORM_ALPHA = (2.0 * DEPTH) ** 0.25
DEEPNORM_BETA = (8.0 * DEPTH) ** -0.25

NA_HEADS = D_MODEL // HEAD_DIM
NA_KH = 8
NA_KW = 16
CONV_WIDTH = 31
SWA_HEADS = D_MODEL // HEAD_DIM
SWA_KV_HEADS = SWA_HEADS // 4
SWA_WINDOW = 128
SWA_BLOCK = 128
DIFF_HEADS = D_MODEL // (2 * HEAD_DIM)
DIFF_BLOCK = 128
MOE_GROUPS = 4
MOE_EXPERTS_PER_GROUP = 8
MOE_EXPERTS = MOE_GROUPS * MOE_EXPERTS_PER_GROUP
MOE_TOP_K = 2
MOE_D_FF = D_MODEL // 4
MOE_BLOCK = 256

kernel_name = 'hybrid_dit_interleaved_hmoe'


def layer_norm(x, g, b):
    xf = x.astype(jnp.float32)
    mu = jnp.mean(xf, -1, keepdims=True)
    var = jnp.mean(jnp.square(xf - mu), -1, keepdims=True)
    return ((xf - mu) * lax.rsqrt(var + LN_EPS)).astype(x.dtype) * g + b


def rms_norm(x, g):
    xf = x.astype(jnp.float32)
    return (xf * lax.rsqrt(jnp.mean(jnp.square(xf), -1, keepdims=True) + LN_EPS)).astype(x.dtype) * g


def softmax_f32(s):
    return jax.nn.softmax(s.astype(jnp.float32), axis=-1)


def axial_rope_tables(n_tokens):
    t = jnp.arange(n_tokens, dtype=jnp.int32)
    pos = jnp.stack([t // GRID_W, t % GRID_W], -1).astype(jnp.float32)
    n_freq = HEAD_DIM // 4
    inv_freq = ROPE_THETA ** (-jnp.arange(n_freq, dtype=jnp.float32) / n_freq)
    ang = pos[:, :, None] * inv_freq
    return jnp.cos(ang), jnp.sin(ang)


def apply_axial_rope(x, cos, sin):
    shp = x.shape
    xr = x.reshape(shp[:-1] + (2, 2, HEAD_DIM // 4))
    x1, x2 = xr[..., 0, :], xr[..., 1, :]
    cs, sn = cos.astype(x.dtype), sin.astype(x.dtype)
    out = jnp.stack([x1 * cs - x2 * sn, x2 * cs + x1 * sn], axis=-2)
    return out.reshape(shp)


def adaln_modulation(cvec, w, b):
    m = jax.nn.silu(cvec) @ w + b
    return jnp.split(m, 6, axis=-1)


def neighbourhood_attention(u, uc, w_qkv, rpb, w_o):
    B, L, _ = u.shape
    C = uc.shape[1]
    H, Dh = NA_HEADS, HEAD_DIM
    rows = L // GRID_W
    kh = min(NA_KH, rows)
    scale = Dh ** -0.5

    def heads(h):
        n = h.shape[1]
        qkv = (h @ w_qkv).reshape(B, n, 3, H, Dh)
        return jnp.transpose(qkv, (2, 0, 3, 1, 4))

    q, k, v = heads(u)
    qc, kc, vc = heads(uc)
    pc = softmax_f32(jnp.einsum('bhqd,bhkd->bhqk', qc, kc) * scale).astype(vc.dtype)
    oc = jnp.einsum('bhqk,bhkd->bhqd', pc, vc)
    qg = q.reshape(B, H, rows, GRID_W, Dh)
    kg = k.reshape(B, H, rows, GRID_W, Dh)
    vg = v.reshape(B, H, rows, GRID_W, Dh)
    cols = jnp.arange(GRID_W)
    col_start = jnp.clip(cols - NA_KW // 2, 0, GRID_W - NA_KW)
    col_ok = (cols[None, :] >= col_start[:, None]) & (cols[None, :] < col_start[:, None] + NA_KW)
    dc_idx = jnp.clip(cols[None, :] - cols[:, None] + NA_KW - 1, 0, 2 * NA_KW - 2)
    rpb_cols = jnp.take(rpb, dc_idx, axis=2)

    def row_block(r):
        r0 = jnp.clip(r - kh // 2, 0, rows - kh)
        k_blk = lax.dynamic_slice_in_dim(kg, r0, kh, axis=2)
        v_blk = lax.dynamic_slice_in_dim(vg, r0, kh, axis=2)
        q_row = lax.dynamic_index_in_dim(qg, r, axis=2, keepdims=False)
        dr_idx = r0 + jnp.arange(kh) - r + NA_KH - 1
        bias = jnp.transpose(jnp.take(rpb_cols, dr_idx, axis=1), (0, 2, 1, 3))
        s_loc = jnp.einsum('bhqd,bhikd->bhqik', q_row, k_blk).astype(jnp.float32) * scale + bias.astype(jnp.float32)
        s_loc = jnp.where(col_ok[:, None, :], s_loc, NEG_INF).reshape(B, H, GRID_W, kh * GRID_W)
        s_ctx = jnp.einsum('bhqd,bhkd->bhqk', q_row, kc).astype(jnp.float32) * scale
        p = softmax_f32(jnp.concatenate([s_loc, s_ctx], -1)).astype(v.dtype)
        p_loc = p[..., :kh * GRID_W].reshape(B, H, GRID_W, kh, GRID_W)
        return (jnp.einsum('bhqik,bhikd->bhqd', p_loc, v_blk)
                + jnp.einsum('bhqk,bhkd->bhqd', p[..., kh * GRID_W:], vc))

    o = lax.map(row_block, jnp.arange(rows))
    o = jnp.transpose(o, (1, 0, 3, 2, 4)).reshape(B, L, H * Dh)
    oc = jnp.transpose(oc, (0, 2, 1, 3)).reshape(B, C, H * Dh)
    return o @ w_o, oc @ w_o


def conformer_conv(u, uc, w_in, b_in, dw, dw_b, ln_g, ln_b, w_out, b_out):
    D = u.shape[-1]
    pad = CONV_WIDTH // 2

    def run(h):
        a = h @ w_in + b_in
        h = a[..., :D] * jax.nn.sigmoid(a[..., D:])
        h = lax.conv_general_dilated(h, dw[:, None, :], window_strides=(1,), padding=[(pad, pad)],
                                     dimension_numbers=('NWC', 'WIO', 'NWC'),
                                     feature_group_count=D) + dw_b
        h = jax.nn.silu(layer_norm(h, ln_g, ln_b))
        return h @ w_out + b_out

    return run(u), run(uc)


def window_sink_attention(u, uc, w_qkv, sink, w_o, cos, sin):
    B, L, D = u.shape
    C = uc.shape[1]
    G, R, Dh = SWA_KV_HEADS, SWA_HEADS // SWA_KV_HEADS, HEAD_DIM
    nb = L // SWA_BLOCK
    scale = Dh ** -0.5
    nq = G * R * Dh

    def heads(h):
        n = h.shape[1]
        a = h @ w_qkv
        q = a[..., :nq].reshape(B, n, G, R, Dh).transpose(0, 2, 3, 1, 4)
        k = a[..., nq:nq + G * Dh].reshape(B, n, G, Dh).transpose(0, 2, 1, 3)
        v = a[..., nq + G * Dh:].reshape(B, n, G, Dh).transpose(0, 2, 1, 3)
        return q, k, v

    q, k, v = heads(u)
    q, k = apply_axial_rope(q, cos, sin), apply_axial_rope(k, cos, sin)
    qc, kc, vc = heads(uc)
    sink_f = sink.astype(jnp.float32).reshape(1, G, R, 1, 1)
    s_c = jnp.einsum('bgrqd,bgkd->bgrqk', qc, kc).astype(jnp.float32) * scale
    s_c = jnp.concatenate([s_c, jnp.broadcast_to(sink_f, (B, G, R, C, 1))], -1)
    pc = softmax_f32(s_c)[..., :C].astype(vc.dtype)
    oc = jnp.einsum('bgrqk,bgkd->bgrqd', pc, vc)
    pad = SWA_WINDOW
    span = SWA_BLOCK + 2 * pad
    kp = jnp.pad(k, ((0, 0), (0, 0), (pad, pad), (0, 0)))
    vp = jnp.pad(v, ((0, 0), (0, 0), (pad, pad), (0, 0)))
    qb = jnp.moveaxis(q.reshape(B, G, R, nb, SWA_BLOCK, Dh), 3, 0)
    sink_q = jnp.broadcast_to(sink_f, (B, G, R, SWA_BLOCK, 1))
    qi = jnp.arange(SWA_BLOCK)[:, None]
    kj = jnp.arange(span)[None, :]

    def block(args):
        n, q_blk = args
        start = n * SWA_BLOCK
        k_blk = lax.dynamic_slice_in_dim(kp, start, span, axis=2)
        v_blk = lax.dynamic_slice_in_dim(vp, start, span, axis=2)
        kpos = start - pad + kj
        qpos = start + qi
        ok = (jnp.abs(qpos - kpos) <= SWA_WINDOW) & (kpos >= 0) & (kpos < L)
        s_loc = jnp.where(ok, jnp.einsum('bgrqd,bgkd->bgrqk', q_blk, k_blk).astype(jnp.float32) * scale, NEG_INF)
        s_ctx = jnp.einsum('bgrqd,bgkd->bgrqk', q_blk, kc).astype(jnp.float32) * scale
        p = softmax_f32(jnp.concatenate([s_loc, s_ctx, sink_q], -1)).astype(v.dtype)
        return (jnp.einsum('bgrqk,bgkd->bgrqd', p[..., :span], v_blk)
                + jnp.einsum('bgrqk,bgkd->bgrqd', p[..., span:span + C], vc))

    o = lax.map(block, (jnp.arange(nb), qb))
    o = jnp.transpose(o, (1, 0, 4, 2, 3, 5)).reshape(B, L, nq)
    oc = jnp.transpose(oc, (0, 3, 1, 2, 4)).reshape(B, C, nq)
    return o @ w_o, oc @ w_o


def differential_attention(u, uc, w_qkv, lam, subln_g, w_o, cos, sin, lambda_init):
    B, L, D = u.shape
    C = uc.shape[1]
    H, Dh = DIFF_HEADS, HEAD_DIM
    nb = L // DIFF_BLOCK
    scale = Dh ** -0.5
    lam_f = lam.astype(jnp.float32)
    lmbda = jnp.exp(jnp.sum(lam_f[0] * lam_f[1])) - jnp.exp(jnp.sum(lam_f[2] * lam_f[3])) + lambda_init

    def heads(h):
        n = h.shape[1]
        a = h @ w_qkv
        q = a[..., :2 * H * Dh].reshape(B, n, H, 2, Dh).transpose(0, 2, 3, 1, 4)
        k = a[..., 2 * H * Dh:4 * H * Dh].reshape(B, n, H, 2, Dh).transpose(0, 2, 3, 1, 4)
        v = a[..., 4 * H * Dh:].reshape(B, n, H, 2 * Dh).transpose(0, 2, 1, 3)
        return q, k, v

    def diff_attend(qq, kk, vv):
        p = softmax_f32(jnp.einsum('bhmqd,bhmkd->bhmqk', qq, kk) * scale)
        pd = (p[:, :, 0] - lmbda * p[:, :, 1]).astype(vv.dtype)
        return jnp.einsum('bhqk,bhkd->bhqd', pd, vv)

    def finish(o):
        o = rms_norm(o, subln_g) * (1.0 - lambda_init)
        n = o.shape[2]
        return o.transpose(0, 2, 1, 3).reshape(B, n, H * 2 * Dh) @ w_o

    q, k, v = heads(u)
    q, k = apply_axial_rope(q, cos, sin), apply_axial_rope(k, cos, sin)
    qc, kc, vc = heads(uc)
    oc = diff_attend(qc, kc, vc)
    k_all = jnp.concatenate([k, kc], axis=3)
    v_all = jnp.concatenate([v, vc], axis=2)
    qb = jnp.moveaxis(q.reshape(B, H, 2, nb, DIFF_BLOCK, Dh), 3, 0)
    o = lax.map(lambda qq: diff_attend(qq, k_all, v_all), qb)
    o = jnp.moveaxis(o, 0, 2).reshape(B, H, L, 2 * Dh)
    return finish(o), finish(oc)


def grouped_expert_ffn(h, expert, gate, w13, w2):
    N, D = h.shape
    A = N * MOE_TOP_K
    flat_e = expert.reshape(A)
    order = jnp.argsort(flat_e)
    e_sorted = flat_e[order]
    counts = jnp.bincount(flat_e, length=MOE_EXPERTS)
    padded = (counts + MOE_BLOCK - 1) // MOE_BLOCK * MOE_BLOCK
    seg_start = jnp.cumsum(counts) - counts
    pad_end = jnp.cumsum(padded)
    pad_start = pad_end - padded
    slot = pad_start[e_sorted] + (jnp.arange(A) - seg_start[e_sorted])
    n_blocks = -(-A // MOE_BLOCK) + MOE_EXPERTS
    n_slots = n_blocks * MOE_BLOCK
    tok = jnp.full((n_slots,), N, dtype=jnp.int32).at[slot].set((order // MOE_TOP_K).astype(jnp.int32))
    g_slot = jnp.zeros((n_slots,), gate.dtype).at[slot].set(gate.reshape(A)[order])
    blk_expert = jnp.minimum(jnp.searchsorted(pad_end, jnp.arange(n_blocks) * MOE_BLOCK, side='right'),
                             MOE_EXPERTS - 1)
    h_pad = jnp.concatenate([h, jnp.zeros((1, D), h.dtype)], 0)
    xs = h_pad[tok].reshape(n_blocks, MOE_BLOCK, D)

    def expert_block(args):
        xb, e = args
        a = xb @ w13[e]
        return (jax.nn.silu(a[:, :MOE_D_FF]) * a[:, MOE_D_FF:]) @ w2[e]

    ys = lax.map(expert_block, (xs, blk_expert)).reshape(n_slots, D)
    out = jnp.zeros((N + 1, D), h.dtype).at[tok].add(ys * g_slot[:, None])
    return out[:N]


def hierarchical_moe(h, rg_w, rg_b, re_w, re_b, w13, w2):
    N = h.shape[0]
    g_prob = softmax_f32(h @ rg_w + rg_b)
    g_p, g_idx = lax.top_k(g_prob, 1)
    e_logits = (h @ re_w + re_b).astype(jnp.float32).reshape(N, MOE_GROUPS, MOE_EXPERTS_PER_GROUP)
    e_logits = e_logits[jnp.arange(N), g_idx[:, 0]]
    e_p, e_idx = lax.top_k(jax.nn.softmax(e_logits, axis=-1), MOE_TOP_K)
    gate = g_p * e_p / jnp.sum(e_p, -1, keepdims=True)
    expert = g_idx * MOE_EXPERTS_PER_GROUP + e_idx
    return grouped_expert_ffn(h, expert, gate.astype(h.dtype), w13, w2)


def hybrid_layer(layer_idx, x, xc, c, c_ctx, cos, sin, mod_w, mod_b, mixer_params,
                 ln1_g, ln1_b, moe_params, ln2_g, ln2_b):
    B, L, D = x.shape
    C = xc.shape[1]
    sh1, sc1, g1, sh2, sc2, g2 = [m[:, None, :] for m in adaln_modulation(c, mod_w, mod_b)]
    sh1c, sc1c, g1c, sh2c, sc2c, g2c = adaln_modulation(c_ctx, mod_w, mod_b)
    u = x * (1.0 + sc1) + sh1
    uc = xc * (1.0 + sc1c) + sh1c
    kind = layer_idx % N_MIXERS
    if kind == 0:
        y, yc = neighbourhood_attention(u, uc, *mixer_params)
    elif kind == 1:
        y, yc = conformer_conv(u, uc, *mixer_params)
    elif kind == 2:
        y, yc = window_sink_attention(u, uc, *mixer_params, cos, sin)
    else:
        lambda_init = 0.8 - 0.6 * math.exp(-0.3 * layer_idx)
        y, yc = differential_attention(u, uc, *mixer_params, cos, sin, lambda_init)
    x = layer_norm(DEEPNORM_ALPHA * x + g1 * y, ln1_g, ln1_b)
    xc = layer_norm(DEEPNORM_ALPHA * xc + g1c * yc, ln1_g, ln1_b)
    u = x * (1.0 + sc2) + sh2
    uc = xc * (1.0 + sc2c) + sh2c
    tokens = jnp.concatenate([u.reshape(B * L, D), uc.reshape(B * C, D)], 0)
    f = hierarchical_moe(tokens, *moe_params)
    x = layer_norm(DEEPNORM_ALPHA * x + g2 * f[:B * L].reshape(B, L, D), ln2_g, ln2_b)
    xc = layer_norm(DEEPNORM_ALPHA * xc + g2c * f[B * L:].reshape(B, C, D), ln2_g, ln2_b)
    return x, xc


def setup_inputs(seed: int = 0) -> dict:
    key = jax.random.key(seed)
    keys = iter(jax.random.split(key, 32 * DEPTH + 8))
    D = D_MODEL

    def normal(shape, scale):
        return jax.random.normal(next(keys), shape, jnp.float32) * scale

    def gain(n):
        return 1.0 + normal((n,), 0.01)

    inputs = {
        'x': normal((BATCH, SEQ, D), 1.0),
        'c': normal((BATCH, D), 1.0),
        'ctx': normal((BATCH, CTX_LEN, D), 1.0),
        'c_ctx': normal((D,), 1.0),
    }
    for i in range(DEPTH):
        p = 'l%d_' % i
        inputs[p + 'mod_w'] = normal((D, 6 * D), D ** -0.5)
        inputs[p + 'mod_b'] = normal((6 * D,), 0.01)
        kind = i % N_MIXERS
        if kind == 0:
            inputs[p + 'na_w_qkv'] = normal((D, 3 * NA_HEADS * HEAD_DIM), D ** -0.5)
            inputs[p + 'na_rpb'] = normal((NA_HEADS, 2 * NA_KH - 1, 2 * NA_KW - 1), 0.02)
            inputs[p + 'na_w_o'] = normal((NA_HEADS * HEAD_DIM, D), (NA_HEADS * HEAD_DIM) ** -0.5 * DEEPNORM_BETA)
        elif kind == 1:
            inputs[p + 'cv_w_in'] = normal((D, 2 * D), D ** -0.5)
            inputs[p + 'cv_b_in'] = normal((2 * D,), 0.01)
            inputs[p + 'cv_dw'] = normal((CONV_WIDTH, D), CONV_WIDTH ** -0.5)
            inputs[p + 'cv_dw_b'] = normal((D,), 0.01)
            inputs[p + 'cv_ln_g'] = gain(D)
            inputs[p + 'cv_ln_b'] = normal((D,), 0.01)
            inputs[p + 'cv_w_out'] = normal((D, D), D ** -0.5 * DEEPNORM_BETA)
            inputs[p + 'cv_b_out'] = normal((D,), 0.01)
        elif kind == 2:
            inputs[p + 'sw_w_qkv'] = normal((D, (SWA_HEADS + 2 * SWA_KV_HEADS) * HEAD_DIM), D ** -0.5)
            inputs[p + 'sw_sink'] = normal((SWA_HEADS,), 0.5)
            inputs[p + 'sw_w_o'] = normal((SWA_HEADS * HEAD_DIM, D), (SWA_HEADS * HEAD_DIM) ** -0.5 * DEEPNORM_BETA)
        else:
            inputs[p + 'df_w_qkv'] = normal((D, 6 * DIFF_HEADS * HEAD_DIM), D ** -0.5)
            inputs[p + 'df_lambda'] = normal((4, HEAD_DIM), 0.1)
            inputs[p + 'df_subln_g'] = gain(2 * HEAD_DIM)
            inputs[p + 'df_w_o'] = normal((2 * DIFF_HEADS * HEAD_DIM, D), (2 * DIFF_HEADS * HEAD_DIM) ** -0.5 * DEEPNORM_BETA)
        inputs[p + 'ln1_g'] = gain(D)
        inputs[p + 'ln1_b'] = normal((D,), 0.01)
        inputs[p + 'router_g_w'] = normal((D, MOE_GROUPS), D ** -0.5)
        inputs[p + 'router_g_b'] = normal((MOE_GROUPS,), 0.01)
        inputs[p + 'router_e_w'] = normal((D, MOE_EXPERTS), D ** -0.5)
        inputs[p + 'router_e_b'] = normal((MOE_EXPERTS,), 0.01)
        inputs[p + 'moe_w13'] = normal((MOE_EXPERTS, D, 2 * MOE_D_FF), D ** -0.5)
        inputs[p + 'moe_w2'] = normal((MOE_EXPERTS, MOE_D_FF, D), MOE_D_FF ** -0.5 * DEEPNORM_BETA)
        inputs[p + 'ln2_g'] = gain(D)
        inputs[p + 'ln2_b'] = normal((D,), 0.01)
    return inputs


def reference(x, c, ctx, c_ctx,
              l0_mod_w, l0_mod_b, l0_na_w_qkv, l0_na_rpb, l0_na_w_o, l0_ln1_g, l0_ln1_b,
              l0_router_g_w, l0_router_g_b, l0_router_e_w, l0_router_e_b, l0_moe_w13, l0_moe_w2, l0_ln2_g, l0_ln2_b,
              l1_mod_w, l1_mod_b, l1_cv_w_in, l1_cv_b_in, l1_cv_dw, l1_cv_dw_b, l1_cv_ln_g, l1_cv_ln_b,
              l1_cv_w_out, l1_cv_b_out, l1_ln1_g, l1_ln1_b,
              l1_router_g_w, l1_router_g_b, l1_router_e_w, l1_router_e_b, l1_moe_w13, l1_moe_w2, l1_ln2_g, l1_ln2_b,
              l2_mod_w, l2_mod_b, l2_sw_w_qkv, l2_sw_sink, l2_sw_w_o, l2_ln1_g, l2_ln1_b,
              l2_router_g_w, l2_router_g_b, l2_router_e_w, l2_router_e_b, l2_moe_w13, l2_moe_w2, l2_ln2_g, l2_ln2_b,
              l3_mod_w, l3_mod_b, l3_df_w_qkv, l3_df_lambda, l3_df_subln_g, l3_df_w_o, l3_ln1_g, l3_ln1_b,
              l3_router_g_w, l3_router_g_b, l3_router_e_w, l3_router_e_b, l3_moe_w13, l3_moe_w2, l3_ln2_g, l3_ln2_b):
    layers = (
        (l0_mod_w, l0_mod_b, (l0_na_w_qkv, l0_na_rpb, l0_na_w_o), l0_ln1_g, l0_ln1_b,
         (l0_router_g_w, l0_router_g_b, l0_router_e_w, l0_router_e_b, l0_moe_w13, l0_moe_w2), l0_ln2_g, l0_ln2_b),
        (l1_mod_w, l1_mod_b, (l1_cv_w_in, l1_cv_b_in, l1_cv_dw, l1_cv_dw_b, l1_cv_ln_g, l1_cv_ln_b, l1_cv_w_out, l1_cv_b_out),
         l1_ln1_g, l1_ln1_b,
         (l1_router_g_w, l1_router_g_b, l1_router_e_w, l1_router_e_b, l1_moe_w13, l1_moe_w2), l1_ln2_g, l1_ln2_b),
        (l2_mod_w, l2_mod_b, (l2_sw_w_qkv, l2_sw_sink, l2_sw_w_o), l2_ln1_g, l2_ln1_b,
         (l2_router_g_w, l2_router_g_b, l2_router_e_w, l2_router_e_b, l2_moe_w13, l2_moe_w2), l2_ln2_g, l2_ln2_b),
        (l3_mod_w, l3_mod_b, (l3_df_w_qkv, l3_df_lambda, l3_df_subln_g, l3_df_w_o), l3_ln1_g, l3_ln1_b,
         (l3_router_g_w, l3_router_g_b, l3_router_e_w, l3_router_e_b, l3_moe_w13, l3_moe_w2), l3_ln2_g, l3_ln2_b),
    )
    cos, sin = axial_rope_tables(x.shape[1])
    xc = ctx
    for i in range(DEPTH):
        x, xc = hybrid_layer(i, x, xc, c, c_ctx, cos, sin, *layers[i])
    return x
```

```python
import functools
import math

import jax
import jax.numpy as jnp
from jax import lax
from jax.experimental import pallas as pl
from jax.experimental.pallas import tpu as pltpu

D = 2048
L = 2048
C = 256
S = L + C
DEPTH = 4
GRID_W = 64
HEAD_DIM = 128
ROPE_THETA = 10000.0
LN_EPS = 1e-5
NEG_INF = -1e30
DEEPNORM_ALPHA = (2.0 * DEPTH) ** 0.25
NA_HEADS = 16
NA_KH = 8
NA_KW = 16
NA_QROWS = 4
NA_KROWS = 12
CONV_WIDTH = 31
CONV_HALO = 16
SWA_KV_HEADS = 4
SWA_REP = 4
SWA_WINDOW = 128
SWA_BLOCK = 128
DIFF_HEADS = 8
MOE_GROUPS = 4
MOE_EPG = 8
MOE_EXPERTS = 32
MOE_TOP_K = 2
MOE_D_FF = 512
MOE_BLOCK = 256
ROUTER_PAD = 128

TM = 256
TILES_PER_BATCH = S // TM
LAT_TILES = L // TM
VMEM_LIMIT = 52 * 1024 * 1024
SCALE = HEAD_DIM ** -0.5
LOG2E = math.log2(math.e)

F32 = jnp.float32
BF16 = jnp.bfloat16


def _cparams(n_axes):
    return pltpu.CompilerParams(dimension_semantics=("arbitrary",) * n_axes,
                                vmem_limit_bytes=VMEM_LIMIT)


def _dot(a, b):
    return jnp.dot(a, b, preferred_element_type=F32)


def _dot_t(a, b):
    return lax.dot_general(a, b, (((1,), (1,)), ((), ())), preferred_element_type=F32)


def _sigmoid(x):
    return 1.0 / (1.0 + jnp.exp(-x))


def _layer_norm(z, g, b):
    mu = jnp.mean(z, axis=-1, keepdims=True)
    zc = z - mu
    var = jnp.mean(zc * zc, axis=-1, keepdims=True)
    return zc * lax.rsqrt(var + LN_EPS) * g + b


def _mod_group(i, n_batch):
    return jnp.where(i % TILES_PER_BATCH == TILES_PER_BATCH - 1, n_batch, i // TILES_PER_BATCH)


def _adaln_kernel(c_ref, w_ref, b_ref, o_ref):
    c = c_ref[...]
    s = c * _sigmoid(c)
    o_ref[...] = _dot(s.astype(BF16), w_ref[...].astype(BF16)) + b_ref[...]


def _adaln(cvec, w, b):
    r = cvec.shape[0]
    n = w.shape[1]
    tn = 1024
    return pl.pallas_call(
        _adaln_kernel,
        grid=(n // tn,),
        in_specs=[pl.BlockSpec((r, D), lambda j: (0, 0)),
                  pl.BlockSpec((D, tn), lambda j: (0, j)),
                  pl.BlockSpec((1, tn), lambda j: (0, j))],
        out_specs=pl.BlockSpec((r, tn), lambda j: (0, j)),
        out_shape=jax.ShapeDtypeStruct((r, n), F32),
        compiler_params=_cparams(1),
        name="adaln",
    )(cvec, w, b.reshape(1, n))


def _rope_rotate(y, cos, sin):
    lane = lax.broadcasted_iota(jnp.int32, y.shape, 1)
    partner = jnp.where(lane % 64 < 32, pltpu.roll(y, 96, 1), pltpu.roll(y, 32, 1))
    return y * cos + partner * sin


def _proj_kernel(*refs, has_bias, glu, rope, tn):
    it = iter(refs)
    x_ref, sc_ref, sh_ref, w_ref = next(it), next(it), next(it), next(it)
    wg_ref = next(it) if glu else None
    b_ref = next(it) if has_bias else None
    bg_ref = next(it) if glu else None
    cos_ref = next(it) if rope else None
    sin_ref = next(it) if rope else None
    o_ref = next(it)

    u = (x_ref[...] * (1.0 + sc_ref[0]) + sh_ref[0]).astype(BF16)
    y = _dot(u, w_ref[...])
    if has_bias:
        y = y + b_ref[...]
    if glu:
        y = y * _sigmoid(_dot(u, wg_ref[...]) + bg_ref[...])
    if rope:
        cos = cos_ref[...]
        sin = sin_ref[...]
        for h in range(tn // HEAD_DIM):
            sl = slice(h * HEAD_DIM, (h + 1) * HEAD_DIM)
            o_ref[:, sl] = _rope_rotate(y[:, sl], cos, sin).astype(o_ref.dtype)
    else:
        o_ref[...] = y.astype(o_ref.dtype)


def _proj(x, sc, sh, w, *, n_out, tn, out_dtype, bias=None, glu=False, rope=None):
    n_rows = x.shape[0]
    n_batch = sc.shape[0] - 1
    n_tiles = n_rows // TM
    n_col = n_out // tn
    grp = lambda j, i: (_mod_group(i, n_batch), 0, 0)
    in_specs = [pl.BlockSpec((TM, D), lambda j, i: (i, 0)),
                pl.BlockSpec((1, 1, D), grp),
                pl.BlockSpec((1, 1, D), grp),
                pl.BlockSpec((D, tn), lambda j, i: (0, j))]
    args = [x, sc, sh, w]
    if glu:
        in_specs.append(pl.BlockSpec((D, tn), lambda j, i: (0, j + n_col)))
        args.append(w)
    if bias is not None:
        b2 = bias.reshape(1, -1)
        in_specs.append(pl.BlockSpec((1, tn), lambda j, i: (0, j)))
        args.append(b2)
        if glu:
            in_specs.append(pl.BlockSpec((1, tn), lambda j, i: (0, j + n_col)))
            args.append(b2)
    if rope is not None:
        rope_spec = pl.BlockSpec((TM, HEAD_DIM), lambda j, i: (i % TILES_PER_BATCH, 0))
        in_specs += [rope_spec, rope_spec]
        args += [rope[0], rope[1]]
    return pl.pallas_call(
        functools.partial(_proj_kernel, has_bias=bias is not None, glu=glu, rope=rope is not None, tn=tn),
        grid=(n_col, n_tiles),
        in_specs=in_specs,
        out_specs=pl.BlockSpec((TM, tn), lambda j, i: (i, j)),
        out_shape=jax.ShapeDtypeStruct((n_rows, n_out), out_dtype),
        compiler_params=_cparams(2),
        name="proj",
    )(*args)


def _post_mixer(y, x_ref, g_ref, lng_ref, lnb_ref, sc2_ref, sh2_ref, wr_ref, br_ref,
                x1_ref, u2_ref, lg_ref):
    z = DEEPNORM_ALPHA * x_ref[...] + g_ref[0] * y
    x1 = _layer_norm(z, lng_ref[...], lnb_ref[...])
    x1_ref[...] = x1
    u2 = (x1 * (1.0 + sc2_ref[0]) + sh2_ref[0]).astype(BF16)
    u2_ref[...] = u2
    lg_ref[...] = _dot(u2, wr_ref[...]) + br_ref[...]


def _post_mixer_specs(n_batch):
    grp = lambda i: (_mod_group(i, n_batch), 0, 0)
    row = lambda i: (i, 0)
    const = lambda i: (0, 0)
    in_specs = [pl.BlockSpec((TM, D), row),
                pl.BlockSpec((1, 1, D), grp),
                pl.BlockSpec((1, D), const),
                pl.BlockSpec((1, D), const),
                pl.BlockSpec((1, 1, D), grp),
                pl.BlockSpec((1, 1, D), grp),
                pl.BlockSpec((D, ROUTER_PAD), const),
                pl.BlockSpec((1, ROUTER_PAD), const)]
    out_specs = [pl.BlockSpec((TM, D), row),
                 pl.BlockSpec((TM, D), row),
                 pl.BlockSpec((TM, ROUTER_PAD), row)]
    return in_specs, out_specs


def _post_mixer_out_shape(n_rows):
    return [jax.ShapeDtypeStruct((n_rows, D), F32),
            jax.ShapeDtypeStruct((n_rows, D), BF16),
            jax.ShapeDtypeStruct((n_rows, ROUTER_PAD), F32)]


def _out_proj_kernel(o_ref, wo_ref, *rest):
    _post_mixer(_dot(o_ref[...], wo_ref[...]), *rest)


def _out_proj(o, wo, x, g1, lng, lnb, sc2, sh2, wr, br):
    n_rows = x.shape[0]
    n_batch = g1.shape[0] - 1
    pm_in, pm_out = _post_mixer_specs(n_batch)
    return pl.pallas_call(
        _out_proj_kernel,
        grid=(n_rows // TM,),
        in_specs=[pl.BlockSpec((TM, D), lambda i: (i, 0)),
                  pl.BlockSpec((D, D), lambda i: (0, 0))] + pm_in,
        out_specs=pm_out,
        out_shape=_post_mixer_out_shape(n_rows),
        compiler_params=_cparams(1),
        name="out_proj",
    )(o, wo, x, g1, lng, lnb, sc2, sh2, wr, br)


def _na_bias_table(rpb):
    rows = L // GRID_W
    n_blocks = rows // NA_QROWS
    j = jnp.arange(n_blocks)
    k_start = jnp.clip(NA_QROWS * j - NA_KH // 2, 0, rows - NA_KROWS)
    qr = (NA_QROWS * j)[:, None] + jnp.arange(NA_QROWS)[None, :]
    kr = k_start[:, None] + jnp.arange(NA_KROWS)[None, :]
    r0 = jnp.clip(qr - NA_KH // 2, 0, rows - NA_KH)
    row_ok = (kr[:, None, :] >= r0[:, :, None]) & (kr[:, None, :] < r0[:, :, None] + NA_KH)
    dr = jnp.clip(kr[:, None, :] - qr[:, :, None] + NA_KH - 1, 0, 2 * NA_KH - 2)
    cols = jnp.arange(GRID_W)
    col_start = jnp.clip(cols - NA_KW // 2, 0, GRID_W - NA_KW)
    col_ok = (cols[None, :] >= col_start[:, None]) & (cols[None, :] < col_start[:, None] + NA_KW)
    dc = jnp.clip(cols[None, :] - cols[:, None] + NA_KW - 1, 0, 2 * NA_KW - 2)
    bias = rpb[:, dr[:, :, None, :, None], dc[None, None, :, None, :]]
    ok = row_ok[:, :, None, :, None] & col_ok[None, None, :, None, :]
    bias = jnp.where(ok[None], bias, NEG_INF)
    return bias.reshape(NA_HEADS, n_blocks, NA_QROWS * GRID_W, NA_KROWS * GRID_W)


def _softmax_pv(parts, values):
    m = parts[0].max(axis=-1, keepdims=True)
    for s in parts[1:]:
        m = jnp.maximum(m, s.max(axis=-1, keepdims=True))
    den = None
    acc = None
    for s, v in zip(parts, values):
        p = jnp.exp(s - m)
        d = p.sum(axis=-1, keepdims=True)
        o = _dot(p.astype(BF16), v)
        den = d if den is None else den + d
        acc = o if acc is None else acc + o
    return acc / den


def _na_kernel(q_ref, k_ref, v_ref, bias_ref, o_ref):
    rows = L // GRID_W
    qb = NA_QROWS * GRID_W
    kb = NA_KROWS * GRID_W
    kc = k_ref[L:S, :]
    vc = v_ref[L:S, :]

    def block(j, carry):
        qs = pl.multiple_of(j * qb, qb)
        ks = pl.multiple_of(jnp.clip(NA_QROWS * j - NA_KH // 2, 0, rows - NA_KROWS) * GRID_W, GRID_W)
        q = q_ref[pl.ds(qs, qb), :]
        s_loc = _dot_t(q, k_ref[pl.ds(ks, kb), :]) * SCALE + bias_ref[0, j]
        s_ctx = _dot_t(q, kc) * SCALE
        o = _softmax_pv([s_loc, s_ctx], [v_ref[pl.ds(ks, kb), :], vc])
        o_ref[pl.ds(qs, qb), :] = o.astype(o_ref.dtype)
        return carry

    lax.fori_loop(0, rows // NA_QROWS, block, 0)
    s = _dot_t(q_ref[L:S, :], kc) * SCALE
    o_ref[L:S, :] = _softmax_pv([s], [vc]).astype(o_ref.dtype)


def _na_attention(qkv, bias, n_batch):
    h = NA_HEADS
    n_blocks = bias.shape[1]
    blk = lambda off: pl.BlockSpec((S, HEAD_DIM), lambda hh, b: (b, off + hh))
    return pl.pallas_call(
        _na_kernel,
        grid=(h, n_batch),
        in_specs=[blk(0), blk(h), blk(2 * h),
                  pl.BlockSpec((1, n_blocks) + bias.shape[2:], lambda hh, b: (hh, 0, 0, 0))],
        out_specs=pl.BlockSpec((S, HEAD_DIM), lambda hh, b: (b, hh)),
        out_shape=jax.ShapeDtypeStruct((n_batch * S, h * HEAD_DIM), BF16),
        compiler_params=_cparams(2),
        name="na_attn",
    )(qkv, qkv, qkv, bias)


def _swa_mask_table():
    span = SWA_BLOCK + 2 * SWA_WINDOW
    qi = jnp.arange(SWA_BLOCK)[:, None]
    kj = jnp.arange(span)[None, :]
    band = jnp.abs(qi - (kj - SWA_WINDOW)) <= SWA_WINDOW
    first = band & (kj >= SWA_WINDOW)
    last = band & (kj < SWA_BLOCK + SWA_WINDOW)
    m = jnp.stack([first, band, last]).astype(F32)
    m = jnp.where(m > 0, 0.0, NEG_INF).astype(F32)
    return jnp.tile(m, (1, SWA_REP, 1))


def _swa_kernel(sink_ref, q_ref, k_ref, v_ref, mask_ref, o_ref, kpad_ref, vpad_ref):
    g = pl.program_id(1)
    nb = L // SWA_BLOCK
    span = SWA_BLOCK + 2 * SWA_WINDOW
    pad = SWA_WINDOW
    zeros = jnp.zeros((pad, HEAD_DIM), BF16)
    for src, dst in ((k_ref, kpad_ref), (v_ref, vpad_ref)):
        dst[0:pad, :] = zeros
        dst[pad + L:, :] = zeros
        dst[pad:pad + L, :] = src[0:L, :]
    kc = k_ref[L:S, :]
    vc = v_ref[L:S, :]

    def sink_column(n_rows):
        return jnp.concatenate(
            [jnp.full((n_rows, 1), sink_ref[g * SWA_REP + r], F32) for r in range(SWA_REP)], axis=0)

    def attend(parts, values, sink_col):
        m = sink_col
        for s in parts:
            m = jnp.maximum(m, s.max(axis=-1, keepdims=True))
        den = jnp.exp(sink_col - m)
        acc = None
        for s, v in zip(parts, values):
            p = jnp.exp(s - m)
            den = den + p.sum(axis=-1, keepdims=True)
            o = _dot(p.astype(BF16), v)
            acc = o if acc is None else acc + o
        return acc / den

    def stacked_q(start, n_rows):
        return jnp.concatenate(
            [q_ref[pl.ds(start, n_rows), r * HEAD_DIM:(r + 1) * HEAD_DIM] for r in range(SWA_REP)], axis=0)

    def store(start, n_rows, o):
        for r in range(SWA_REP):
            o_ref[pl.ds(start, n_rows), r * HEAD_DIM:(r + 1) * HEAD_DIM] = (
                o[r * n_rows:(r + 1) * n_rows].astype(o_ref.dtype))

    sink_blk = sink_column(SWA_BLOCK)

    def block(n, carry):
        start = pl.multiple_of(n * SWA_BLOCK, SWA_BLOCK)
        q = stacked_q(start, SWA_BLOCK)
        kind = jnp.where(n == 0, 0, jnp.where(n == nb - 1, 2, 1))
        s_loc = _dot_t(q, kpad_ref[pl.ds(start, span), :]) * SCALE + mask_ref[kind]
        s_ctx = _dot_t(q, kc) * SCALE
        store(start, SWA_BLOCK, attend([s_loc, s_ctx], [vpad_ref[pl.ds(start, span), :], vc], sink_blk))
        return carry

    lax.fori_loop(0, nb, block, 0)
    qc = stacked_q(L, C)
    store(L, C, attend([_dot_t(qc, kc) * SCALE], [vc], sink_column(C)))


def _swa_attention(qk, v, sink, mask, n_batch):
    qw = SWA_REP * HEAD_DIM
    n_q_blocks = SWA_KV_HEADS
    return pl.pallas_call(
        _swa_kernel,
        grid_spec=pltpu.PrefetchScalarGridSpec(
            num_scalar_prefetch=1,
            grid=(n_batch, SWA_KV_HEADS),
            in_specs=[pl.BlockSpec((S, qw), lambda b, g, sk: (b, g)),
                      pl.BlockSpec((S, HEAD_DIM), lambda b, g, sk: (b, n_q_blocks * SWA_REP + g)),
                      pl.BlockSpec((S, HEAD_DIM), lambda b, g, sk: (b, g)),
                      pl.BlockSpec(mask.shape, lambda b, g, sk: (0, 0, 0))],
            out_specs=pl.BlockSpec((S, qw), lambda b, g, sk: (b, g)),
            scratch_shapes=[pltpu.VMEM((L + 2 * SWA_WINDOW, HEAD_DIM), BF16),
                            pltpu.VMEM((L + 2 * SWA_WINDOW, HEAD_DIM), BF16)]),
        out_shape=jax.ShapeDtypeStruct((n_batch * S, SWA_KV_HEADS * qw), BF16),
        compiler_params=_cparams(2),
        name="swa_attn",
    )(sink, qk, qk, v, mask)


def _diff_kernel(lam_ref, g_ref, q_ref, k_ref, v_ref, o_ref, *, lambda_init, tq):
    lam = lam_ref[...]
    lmbda = (jnp.exp(jnp.sum(lam[0:1] * lam[1:2], axis=-1, keepdims=True))
             - jnp.exp(jnp.sum(lam[2:3] * lam[3:4], axis=-1, keepdims=True)) + lambda_init)
    gain = g_ref[...] * (1.0 - lambda_init)

    def softmax_parts(s):
        m = s.max(axis=-1, keepdims=True)
        e = jnp.exp2((s - m) * (SCALE * LOG2E))
        return e, e.sum(axis=-1, keepdims=True)

    def attend(start, n_rows, k_lo, n_keys):
        q = q_ref[pl.ds(start, n_rows), :]
        e1, d1 = softmax_parts(_dot_t(q[:, :HEAD_DIM], k_ref[pl.ds(k_lo, n_keys), 0:HEAD_DIM]))
        e2, d2 = softmax_parts(_dot_t(q[:, HEAD_DIM:], k_ref[pl.ds(k_lo, n_keys), HEAD_DIM:2 * HEAD_DIM]))
        pd = e1 * (1.0 / d1) - e2 * (lmbda / d2)
        o = _dot(pd.astype(BF16), v_ref[pl.ds(k_lo, n_keys), :])
        o = o * lax.rsqrt(jnp.mean(o * o, axis=-1, keepdims=True) + LN_EPS) * gain
        o_ref[pl.ds(start, n_rows), :] = o.astype(o_ref.dtype)

    def block(i, carry):
        attend(pl.multiple_of(i * tq, tq), tq, 0, S)
        return carry

    lax.fori_loop(0, L // tq, block, 0)
    attend(L, C, L, C)


def _diff_attention(qk, v, lam, subln_g, lambda_init, n_batch):
    hw = 2 * HEAD_DIM
    return pl.pallas_call(
        functools.partial(_diff_kernel, lambda_init=lambda_init, tq=256),
        grid=(n_batch, DIFF_HEADS),
        in_specs=[pl.BlockSpec((4, HEAD_DIM), lambda b, h: (0, 0)),
                  pl.BlockSpec((1, hw), lambda b, h: (0, 0)),
                  pl.BlockSpec((S, hw), lambda b, h: (b, h)),
                  pl.BlockSpec((S, hw), lambda b, h: (b, DIFF_HEADS + h)),
                  pl.BlockSpec((S, hw), lambda b, h: (b, h))],
        out_specs=pl.BlockSpec((S, hw), lambda b, h: (b, h)),
        out_shape=jax.ShapeDtypeStruct((n_batch * S, DIFF_HEADS * hw), BF16),
        compiler_params=_cparams(2),
        name="diff_attn",
    )(lam, subln_g.reshape(1, hw), qk, qk, v)


CONV_ROWS = 128


def _conv_kernel(hc_ref, hp_ref, hn_ref, dw_ref, dwb_ref, cg_ref, cb_ref, wout_ref, bout_ref, *rest):
    win_ref, cv_ref = rest[-2], rest[-1]
    t = pl.program_id(0) % TILES_PER_BATCH
    first = (t == 0) | (t == TILES_PER_BATCH - 1)
    last = t >= TILES_PER_BATCH - 2
    win_ref[0:CONV_HALO, :] = jnp.where(first, 0.0, hp_ref[...])
    win_ref[CONV_HALO:CONV_HALO + TM, :] = hc_ref[...]
    win_ref[CONV_HALO + TM:, :] = jnp.where(last, 0.0, hn_ref[...])
    base = CONV_HALO - CONV_WIDTH // 2

    def strip(c, carry):
        cs = pl.ds(pl.multiple_of(c * HEAD_DIM, HEAD_DIM), HEAD_DIM)
        for r in range(TM // CONV_ROWS):
            acc = jnp.zeros((CONV_ROWS, HEAD_DIM), F32)
            for j in range(CONV_WIDTH):
                acc = acc + dw_ref[j:j + 1, cs] * win_ref[pl.ds(r * CONV_ROWS + base + j, CONV_ROWS), cs]
            cv_ref[r * CONV_ROWS:(r + 1) * CONV_ROWS, cs] = acc
        return carry

    lax.fori_loop(0, D // HEAD_DIM, strip, 0)
    hn = _layer_norm(cv_ref[...] + dwb_ref[...], cg_ref[...], cb_ref[...])
    hn = hn * _sigmoid(hn)
    y = _dot(hn.astype(BF16), wout_ref[...]) + bout_ref[...]
    _post_mixer(y, *rest[:-2])


def _conv_mixer(h, dw, dw_b, cg, cb, wout, bout, x, g1, lng, lnb, sc2, sh2, wr, br):
    n_rows = x.shape[0]
    n_batch = g1.shape[0] - 1
    n_tiles = n_rows // TM
    halo_per_tile = TM // CONV_HALO
    n_halo = n_rows // CONV_HALO
    pm_in, pm_out = _post_mixer_specs(n_batch)
    const = lambda i: (0, 0)
    vec = lambda a: a.reshape(1, D)
    dw_pad = jnp.concatenate([dw, jnp.zeros((1, D), dw.dtype)], axis=0)
    return pl.pallas_call(
        _conv_kernel,
        grid=(n_tiles,),
        in_specs=[pl.BlockSpec((TM, D), lambda i: (i, 0)),
                  pl.BlockSpec((CONV_HALO, D), lambda i: (jnp.maximum(i * halo_per_tile - 1, 0), 0)),
                  pl.BlockSpec((CONV_HALO, D), lambda i: (jnp.minimum((i + 1) * halo_per_tile, n_halo - 1), 0)),
                  pl.BlockSpec((CONV_WIDTH + 1, D), const),
                  pl.BlockSpec((1, D), const),
                  pl.BlockSpec((1, D), const),
                  pl.BlockSpec((1, D), const),
                  pl.BlockSpec((D, D), const),
                  pl.BlockSpec((1, D), const)] + pm_in,
        out_specs=pm_out,
        out_shape=_post_mixer_out_shape(n_rows),
        scratch_shapes=[pltpu.VMEM((TM + 2 * CONV_HALO, D), F32),
                        pltpu.VMEM((TM, D), F32)],
        compiler_params=_cparams(1),
        name="conv_mixer",
    )(h, h, h, dw_pad, vec(dw_b), vec(cg), vec(cb), wout, vec(bout), x, g1, lng, lnb, sc2, sh2, wr, br)


def _moe_kernel(be_ref, nu_ref, xs_ref, gs_ref, w13_ref, w2_ref, ys_ref, w13b_ref, w2b_ref):
    i = pl.program_id(0)

    @pl.when(i < nu_ref[0])
    def _():
        changed = (i == 0) | (be_ref[i] != be_ref[jnp.maximum(i - 1, 0)])

        @pl.when(changed)
        def _():
            w13b_ref[...] = w13_ref[0].astype(BF16)
            w2b_ref[...] = w2_ref[0].astype(BF16)

        a = _dot(xs_ref[...], w13b_ref[...])
        gate = a[:, :MOE_D_FF]
        hmid = (gate * _sigmoid(gate) * a[:, MOE_D_FF:]).astype(BF16)
        y = _dot(hmid, w2b_ref[...])
        gs = gs_ref[...]
        for cidx in range(D // HEAD_DIM):
            sl = slice(cidx * HEAD_DIM, (cidx + 1) * HEAD_DIM)
            ys_ref[:, sl] = y[:, sl] * gs


def _moe_experts(xs, g_slot, blk_expert, n_used, w13, w2):
    n_slots = xs.shape[0]
    n_blocks = n_slots // MOE_BLOCK
    live = lambda i, nu: jnp.minimum(i, nu[0] - 1)
    return pl.pallas_call(
        _moe_kernel,
        grid_spec=pltpu.PrefetchScalarGridSpec(
            num_scalar_prefetch=2,
            grid=(n_blocks,),
            in_specs=[pl.BlockSpec((MOE_BLOCK, D), lambda i, be, nu: (live(i, nu), 0)),
                      pl.BlockSpec((MOE_BLOCK, HEAD_DIM), lambda i, be, nu: (live(i, nu), 0)),
                      pl.BlockSpec((1, D, 2 * MOE_D_FF), lambda i, be, nu: (be[live(i, nu)], 0, 0)),
                      pl.BlockSpec((1, MOE_D_FF, D), lambda i, be, nu: (be[live(i, nu)], 0, 0))],
            out_specs=pl.BlockSpec((MOE_BLOCK, D), lambda i, be, nu: (live(i, nu), 0)),
            scratch_shapes=[pltpu.VMEM((D, 2 * MOE_D_FF), BF16),
                            pltpu.VMEM((MOE_D_FF, D), BF16)]),
        out_shape=jax.ShapeDtypeStruct((n_slots, D), F32),
        compiler_params=_cparams(1),
        name="moe_experts",
    )(blk_expert, n_used, xs, g_slot, w13, w2)


def _combine_kernel(x_ref, ya_ref, yb_ref, g_ref, lng_ref, lnb_ref, o_ref):
    z = DEEPNORM_ALPHA * x_ref[...] + g_ref[0] * (ya_ref[...] + yb_ref[...])
    o_ref[...] = _layer_norm(z, lng_ref[...], lnb_ref[...])


def _combine(x1, ya, yb, g2, lng, lnb):
    n_rows = x1.shape[0]
    n_batch = g2.shape[0] - 1
    row = pl.BlockSpec((TM, D), lambda i: (i, 0))
    const = pl.BlockSpec((1, D), lambda i: (0, 0))
    return pl.pallas_call(
        _combine_kernel,
        grid=(n_rows // TM,),
        in_specs=[row, row, row,
                  pl.BlockSpec((1, 1, D), lambda i: (_mod_group(i, n_batch), 0, 0)),
                  const, const],
        out_specs=row,
        out_shape=jax.ShapeDtypeStruct((n_rows, D), F32),
        compiler_params=_cparams(1),
        name="moe_combine",
    )(x1, ya, yb, g2, lng, lnb)


def _route(logits):
    n = logits.shape[0]
    g_prob = jax.nn.softmax(logits[:, :MOE_GROUPS], axis=-1)
    g_p, g_idx = lax.top_k(g_prob, 1)
    e_logits = logits[:, MOE_GROUPS:MOE_GROUPS + MOE_EXPERTS].reshape(n, MOE_GROUPS, MOE_EPG)
    e_logits = jnp.take_along_axis(e_logits, g_idx[:, :, None], axis=1)[:, 0]
    e_p, e_idx = lax.top_k(jax.nn.softmax(e_logits, axis=-1), MOE_TOP_K)
    gate = g_p * e_p / jnp.sum(e_p, -1, keepdims=True)
    return g_idx * MOE_EPG + e_idx, gate


def _dispatch_plan(expert, gate):
    n = expert.shape[0]
    a = n * MOE_TOP_K
    flat_e = expert.reshape(a)
    onehot = (flat_e[:, None] == jnp.arange(MOE_EXPERTS)[None, :]).astype(jnp.int32)
    csum = jnp.cumsum(onehot, axis=0)
    rank = jnp.take_along_axis(csum, flat_e[:, None], axis=1)[:, 0] - 1
    counts = csum[-1]
    padded = (counts + MOE_BLOCK - 1) // MOE_BLOCK * MOE_BLOCK
    pad_end = jnp.cumsum(padded)
    pad_start = pad_end - padded
    slot = pad_start[flat_e] + rank
    n_blocks = -(-a // MOE_BLOCK) + MOE_EXPERTS
    n_slots = n_blocks * MOE_BLOCK
    tok = jnp.zeros((n_slots,), jnp.int32).at[slot].set(jnp.arange(a, dtype=jnp.int32) // MOE_TOP_K)
    g_slot = jnp.zeros((n_slots,), F32).at[slot].set(gate.reshape(a))
    blk_expert = jnp.minimum(
        jnp.searchsorted(pad_end, jnp.arange(n_blocks) * MOE_BLOCK, side='right'), MOE_EXPERTS - 1)
    n_used = (pad_end[-1] // MOE_BLOCK).reshape(1)
    return slot.reshape(n, MOE_TOP_K), tok, g_slot, blk_expert.astype(jnp.int32), n_used.astype(jnp.int32)


def _moe_layer(x1, u2, logits, g2, lng, lnb, w13, w2):
    expert, gate = _route(logits)
    slot, tok, g_slot, blk_expert, n_used = _dispatch_plan(expert, gate)
    xs = jnp.take(u2, tok, axis=0)
    g_tile = jnp.broadcast_to(g_slot[:, None], (g_slot.shape[0], HEAD_DIM))
    ys = _moe_experts(xs, g_tile, blk_expert, n_used, w13, w2)
    ya = jnp.take(ys, slot[:, 0], axis=0)
    yb = jnp.take(ys, slot[:, 1], axis=0)
    return _combine(x1, ya, yb, g2, lng, lnb)


def _rope_tables():
    t = jnp.arange(L, dtype=jnp.int32)
    pos = jnp.stack([t // GRID_W, t % GRID_W], -1).astype(F32)
    n_freq = HEAD_DIM // 4
    inv_freq = ROPE_THETA ** (-jnp.arange(n_freq, dtype=F32) / n_freq)
    ang = pos[:, :, None] * inv_freq
    cos, sin = jnp.cos(ang), jnp.sin(ang)
    cos_t = jnp.stack([cos, cos], axis=2).reshape(L, HEAD_DIM)
    sin_t = jnp.stack([-sin, sin], axis=2).reshape(L, HEAD_DIM)
    cos_t = jnp.concatenate([cos_t, jnp.ones((C, HEAD_DIM), F32)], axis=0)
    sin_t = jnp.concatenate([sin_t, jnp.zeros((C, HEAD_DIM), F32)], axis=0)
    return cos_t, sin_t


def _router_params(rg_w, rg_b, re_w, re_b):
    n = MOE_GROUPS + MOE_EXPERTS
    w = jnp.concatenate([rg_w, re_w, jnp.zeros((D, ROUTER_PAD - n), F32)], axis=1).astype(BF16)
    b = jnp.concatenate([rg_b, re_b, jnp.zeros((ROUTER_PAD - n,), F32)]).reshape(1, ROUTER_PAD)
    return w, b


def _modulation_inputs(c, c_ctx):
    n = c.shape[0] + 1
    pad = -n % 8
    return jnp.concatenate([c, c_ctx[None, :], jnp.zeros((pad, D), F32)], axis=0)


def _hybrid_layer(idx, xs, cvec, n_batch, mod_w, mod_b, mixer, ln1_g, ln1_b, moe, ln2_g, ln2_b):
    vec = lambda a: a.reshape(1, D)
    m = _adaln(cvec, mod_w, mod_b)[:n_batch + 1]
    sh1, sc1, g1, sh2, sc2, g2 = [m[:, None, k * D:(k + 1) * D] for k in range(6)]
    rg_w, rg_b, re_w, re_b, w13, w2 = moe
    wr, br = _router_params(rg_w, rg_b, re_w, re_b)
    post = (xs, g1, vec(ln1_g), vec(ln1_b), sc2, sh2, wr, br)
    kind = idx % 4
    if kind == 0:
        w_qkv, rpb, w_o = mixer
        qkv = _proj(xs, sc1, sh1, w_qkv.astype(BF16), n_out=3 * D, tn=D, out_dtype=BF16)
        o = _na_attention(qkv, _na_bias_table(rpb), n_batch)
        x1, u2, logits = _out_proj(o, w_o.astype(BF16), *post)
    elif kind == 1:
        w_in, b_in, dw, dw_b, cg, cb, w_out, b_out = mixer
        h = _proj(xs, sc1, sh1, w_in.astype(BF16), n_out=D, tn=D // 2, out_dtype=F32, bias=b_in, glu=True)
        x1, u2, logits = _conv_mixer(h, dw, dw_b, cg, cb, w_out.astype(BF16), b_out, *post)
    elif kind == 2:
        w_qkv, sink, w_o = mixer
        n_qk = (SWA_KV_HEADS * SWA_REP + SWA_KV_HEADS) * HEAD_DIM
        n_v = SWA_KV_HEADS * HEAD_DIM
        rope = _rope_tables()
        w_qkv = w_qkv.astype(BF16)
        qk = _proj(xs, sc1, sh1, w_qkv[:, :n_qk], n_out=n_qk, tn=n_qk, out_dtype=BF16, rope=rope)
        v = _proj(xs, sc1, sh1, w_qkv[:, n_qk:], n_out=n_v, tn=n_v, out_dtype=BF16)
        o = _swa_attention(qk, v, sink, _swa_mask_table(), n_batch)
        x1, u2, logits = _out_proj(o, w_o.astype(BF16), *post)
    else:
        w_qkv, lam, subln_g, w_o = mixer
        lambda_init = 0.8 - 0.6 * math.exp(-0.3 * idx)
        rope = _rope_tables()
        w_qkv = w_qkv.astype(BF16)
        qk = _proj(xs, sc1, sh1, w_qkv[:, :2 * D], n_out=2 * D, tn=D, out_dtype=BF16, rope=rope)
        v = _proj(xs, sc1, sh1, w_qkv[:, 2 * D:], n_out=D, tn=D, out_dtype=BF16)
        o = _diff_attention(qk, v, lam, subln_g, lambda_init, n_batch)
        x1, u2, logits = _out_proj(o, w_o.astype(BF16), *post)
    return _moe_layer(x1, u2, logits, g2, vec(ln2_g), vec(ln2_b), w13, w2)


def kernel(x, c, ctx, c_ctx, l0_mod_w, l0_mod_b, l0_na_w_qkv, l0_na_rpb, l0_na_w_o, l0_ln1_g, l0_ln1_b, l0_router_g_w, l0_router_g_b, l0_router_e_w, l0_router_e_b, l0_moe_w13, l0_moe_w2, l0_ln2_g, l0_ln2_b, l1_mod_w, l1_mod_b, l1_cv_w_in, l1_cv_b_in, l1_cv_dw, l1_cv_dw_b, l1_cv_ln_g, l1_cv_ln_b, l1_cv_w_out, l1_cv_b_out, l1_ln1_g, l1_ln1_b, l1_router_g_w, l1_router_g_b, l1_router_e_w, l1_router_e_b, l1_moe_w13, l1_moe_w2, l1_ln2_g, l1_ln2_b, l2_mod_w, l2_mod_b, l2_sw_w_qkv, l2_sw_sink, l2_sw_w_o, l2_ln1_g, l2_ln1_b, l2_router_g_w, l2_router_g_b, l2_router_e_w, l2_router_e_b, l2_moe_w13, l2_moe_w2, l2_ln2_g, l2_ln2_b, l3_mod_w, l3_mod_b, l3_df_w_qkv, l3_df_lambda, l3_df_subln_g, l3_df_w_o, l3_ln1_g, l3_ln1_b, l3_router_g_w, l3_router_g_b, l3_router_e_w, l3_router_e_b, l3_moe_w13, l3_moe_w2, l3_ln2_g, l3_ln2_b):
    layers = (
        (l0_mod_w, l0_mod_b, (l0_na_w_qkv, l0_na_rpb, l0_na_w_o), l0_ln1_g, l0_ln1_b,
         (l0_router_g_w, l0_router_g_b, l0_router_e_w, l0_router_e_b, l0_moe_w13, l0_moe_w2), l0_ln2_g, l0_ln2_b),
        (l1_mod_w, l1_mod_b, (l1_cv_w_in, l1_cv_b_in, l1_cv_dw, l1_cv_dw_b, l1_cv_ln_g, l1_cv_ln_b, l1_cv_w_out,
                              l1_cv_b_out), l1_ln1_g, l1_ln1_b,
         (l1_router_g_w, l1_router_g_b, l1_router_e_w, l1_router_e_b, l1_moe_w13, l1_moe_w2), l1_ln2_g, l1_ln2_b),
        (l2_mod_w, l2_mod_b, (l2_sw_w_qkv, l2_sw_sink, l2_sw_w_o), l2_ln1_g, l2_ln1_b,
         (l2_router_g_w, l2_router_g_b, l2_router_e_w, l2_router_e_b, l2_moe_w13, l2_moe_w2), l2_ln2_g, l2_ln2_b),
        (l3_mod_w, l3_mod_b, (l3_df_w_qkv, l3_df_lambda, l3_df_subln_g, l3_df_w_o), l3_ln1_g, l3_ln1_b,
         (l3_router_g_w, l3_router_g_b, l3_router_e_w, l3_router_e_b, l3_moe_w13, l3_moe_w2), l3_ln2_g, l3_ln2_b),
    )
    n_batch = x.shape[0]
    assert x.shape[1:] == (L, D) and ctx.shape[1:] == (C, D)
    xs = jnp.concatenate([x, ctx], axis=1).reshape(n_batch * S, D)
    cvec = _modulation_inputs(c, c_ctx)
    for idx in range(DEPTH):
        xs = _hybrid_layer(idx, xs, cvec, n_batch, *layers[idx])
    return xs.reshape(n_batch, S, D)[:, :L]
```

```python
import functools
import math

import jax
import jax.numpy as jnp
from jax import lax
from jax.experimental import pallas as pl
from jax.experimental.pallas import tpu as pltpu

D = 2048
L = 2048
C = 256
S = L + C
DEPTH = 4
GRID_W = 64
HEAD_DIM = 128
ROPE_THETA = 10000.0
LN_EPS = 1e-5
NEG_INF = -1e30
DEEPNORM_ALPHA = (2.0 * DEPTH) ** 0.25
NA_HEADS = 16
NA_KH = 8
NA_KW = 16
NA_QROWS = 4
NA_KROWS = 12
CONV_WIDTH = 31
CONV_HALO = 16
SWA_KV_HEADS = 4
SWA_REP = 4
SWA_WINDOW = 128
SWA_BLOCK = 128
DIFF_HEADS = 8
MOE_GROUPS = 4
MOE_EPG = 8
MOE_EXPERTS = 32
MOE_TOP_K = 2
MOE_D_FF = 512
MOE_BLOCK = 256
ROUTER_PAD = 128

TM = 256
TILES_PER_BATCH = S // TM
LAT_TILES = L // TM
VMEM_LIMIT = 52 * 1024 * 1024
SCALE = HEAD_DIM ** -0.5
LOG2E = math.log2(math.e)

F32 = jnp.float32
BF16 = jnp.bfloat16


def _cparams(n_axes):
    return pltpu.CompilerParams(dimension_semantics=("arbitrary",) * n_axes,
                                vmem_limit_bytes=VMEM_LIMIT)


def _dot(a, b):
    return jnp.dot(a, b, preferred_element_type=F32)


def _dot_t(a, b):
    return lax.dot_general(a, b, (((1,), (1,)), ((), ())), preferred_element_type=F32)


def _sigmoid(x):
    return 1.0 / (1.0 + jnp.exp(-x))


def _layer_norm(z, g, b):
    mu = jnp.mean(z, axis=-1, keepdims=True)
    zc = z - mu
    var = jnp.mean(zc * zc, axis=-1, keepdims=True)
    return zc * lax.rsqrt(var + LN_EPS) * g + b


def _mod_group(i, n_batch):
    return jnp.where(i % TILES_PER_BATCH == TILES_PER_BATCH - 1, n_batch, i // TILES_PER_BATCH)


def _adaln_kernel(c_ref, w_ref, b_ref, o_ref):
    c = c_ref[...]
    s = c * _sigmoid(c)
    o_ref[...] = _dot(s.astype(BF16), w_ref[...].astype(BF16)) + b_ref[...]


def _adaln(cvec, w, b):
    r = cvec.shape[0]
    n = w.shape[1]
    tn = 1024
    return pl.pallas_call(
        _adaln_kernel,
        grid=(n // tn,),
        in_specs=[pl.BlockSpec((r, D), lambda j: (0, 0)),
                  pl.BlockSpec((D, tn), lambda j: (0, j)),
                  pl.BlockSpec((1, tn), lambda j: (0, j))],
        out_specs=pl.BlockSpec((r, tn), lambda j: (0, j)),
        out_shape=jax.ShapeDtypeStruct((r, n), F32),
        compiler_params=_cparams(1),
        name="adaln",
    )(cvec, w, b.reshape(1, n))


def _rope_rotate(y, cos, sin):
    lane = lax.broadcasted_iota(jnp.int32, y.shape, 1)
    partner = jnp.where(lane % 64 < 32, pltpu.roll(y, 96, 1), pltpu.roll(y, 32, 1))
    return y * cos + partner * sin


def _proj_kernel(*refs, has_bias, glu, rope, tn):
    it = iter(refs)
    x_ref, sc_ref, sh_ref, w_ref = next(it), next(it), next(it), next(it)
    wg_ref = next(it) if glu else None
    b_ref = next(it) if has_bias else None
    bg_ref = next(it) if glu else None
    cos_ref = next(it) if rope else None
    sin_ref = next(it) if rope else None
    o_ref = next(it)

    u = (x_ref[...] * (1.0 + sc_ref[0]) + sh_ref[0]).astype(BF16)
    y = _dot(u, w_ref[...])
    if has_bias:
        y = y + b_ref[...]
    if glu:
        y = y * _sigmoid(_dot(u, wg_ref[...]) + bg_ref[...])
    if rope:
        cos = cos_ref[...]
        sin = sin_ref[...]
        for h in range(tn // HEAD_DIM):
            sl = slice(h * HEAD_DIM, (h + 1) * HEAD_DIM)
            o_ref[:, sl] = _rope_rotate(y[:, sl], cos, sin).astype(o_ref.dtype)
    else:
        o_ref[...] = y.astype(o_ref.dtype)


def _proj(x, sc, sh, w, *, n_out, tn, out_dtype, bias=None, glu=False, rope=None):
    n_rows = x.shape[0]
    n_batch = sc.shape[0] - 1
    n_tiles = n_rows // TM
    n_col = n_out // tn
    grp = lambda j, i: (_mod_group(i, n_batch), 0, 0)
    in_specs = [pl.BlockSpec((TM, D), lambda j, i: (i, 0)),
                pl.BlockSpec((1, 1, D), grp),
                pl.BlockSpec((1, 1, D), grp),
                pl.BlockSpec((D, tn), lambda j, i: (0, j))]
    args = [x, sc, sh, w]
    if glu:
        in_specs.append(pl.BlockSpec((D, tn), lambda j, i: (0, j + n_col)))
        args.append(w)
    if bias is not None:
        b2 = bias.reshape(1, -1)
        in_specs.append(pl.BlockSpec((1, tn), lambda j, i: (0, j)))
        args.append(b2)
        if glu:
            in_specs.append(pl.BlockSpec((1, tn), lambda j, i: (0, j + n_col)))
            args.append(b2)
    if rope is not None:
        rope_spec = pl.BlockSpec((TM, HEAD_DIM), lambda j, i: (i % TILES_PER_BATCH, 0))
        in_specs += [rope_spec, rope_spec]
        args += [rope[0], rope[1]]
    return pl.pallas_call(
        functools.partial(_proj_kernel, has_bias=bias is not None, glu=glu, rope=rope is not None, tn=tn),
        grid=(n_col, n_tiles),
        in_specs=in_specs,
        out_specs=pl.BlockSpec((TM, tn), lambda j, i: (i, j)),
        out_shape=jax.ShapeDtypeStruct((n_rows, n_out), out_dtype),
        compiler_params=_cparams(2),
        name="proj",
    )(*args)


ROUTE_EXPERT, ROUTE_GATE, ROUTE_RANK = 0, 2, 4


def _first_lane_where(cond, lane):
    return jnp.min(jnp.where(cond, lane, ROUTER_PAD), axis=-1, keepdims=True)


def _route_tile(lg, cnt_ref):
    lane = lax.broadcasted_iota(jnp.int32, lg.shape, 1)
    gmask = lane < MOE_GROUPS
    gl = jnp.where(gmask, lg, NEG_INF)
    ge = jnp.exp(gl - gl.max(axis=-1, keepdims=True))
    gp = ge / ge.sum(axis=-1, keepdims=True)
    g_p = gp.max(axis=-1, keepdims=True)
    g_idx = _first_lane_where(gmask & (gp == g_p), lane)
    lo = MOE_GROUPS + MOE_EPG * g_idx
    emask = (lane >= lo) & (lane < lo + MOE_EPG)
    el = jnp.where(emask, lg, NEG_INF)
    ee = jnp.exp(el - el.max(axis=-1, keepdims=True))
    ep = jnp.where(emask, ee / ee.sum(axis=-1, keepdims=True), -1.0)
    p1 = ep.max(axis=-1, keepdims=True)
    i1 = _first_lane_where(ep == p1, lane)
    ep2 = jnp.where(lane == i1, -1.0, ep)
    p2 = ep2.max(axis=-1, keepdims=True)
    i2 = _first_lane_where(ep2 == p2, lane)
    den = p1 + p2
    gate1 = g_p * p1 / den
    gate2 = g_p * p2 / den
    oh1 = jnp.where(lane == i1, 1.0, 0.0)
    oh2 = jnp.where(lane == i2, 1.0, 0.0)
    n = lg.shape[0]
    tri = jnp.where(lax.broadcasted_iota(jnp.int32, (n, n), 0) > lax.broadcasted_iota(jnp.int32, (n, n), 1),
                    1.0, 0.0).astype(BF16)
    base = cnt_ref[...]
    tot1 = oh1.sum(axis=0, keepdims=True)
    pre1 = _dot(tri, oh1.astype(BF16)) + base
    pre2 = _dot(tri, oh2.astype(BF16)) + (base + tot1)
    rank1 = (oh1 * pre1).sum(axis=-1, keepdims=True)
    rank2 = (oh2 * pre2).sum(axis=-1, keepdims=True)
    cnt_ref[...] = base + tot1 + oh2.sum(axis=0, keepdims=True)
    cols = ((i1 - MOE_GROUPS).astype(F32), (i2 - MOE_GROUPS).astype(F32), gate1, gate2, rank1, rank2)
    route = jnp.zeros(lg.shape, F32)
    for k, col in enumerate(cols):
        route = jnp.where(lane == k, col, route)
    return route


def _post_mixer(y, x_ref, g_ref, lng_ref, lnb_ref, sc2_ref, sh2_ref, wr_ref, br_ref,
                x1_ref, u2_ref, route_ref, cnt_out_ref, cnt_ref):
    @pl.when(pl.program_id(0) == 0)
    def _():
        cnt_ref[...] = jnp.zeros(cnt_ref.shape, F32)

    z = DEEPNORM_ALPHA * x_ref[...] + g_ref[0] * y
    x1 = _layer_norm(z, lng_ref[...], lnb_ref[...])
    x1_ref[...] = x1
    u2 = (x1 * (1.0 + sc2_ref[0]) + sh2_ref[0]).astype(BF16)
    u2_ref[...] = u2
    route_ref[...] = _route_tile(_dot(u2, wr_ref[...]) + br_ref[...], cnt_ref)
    cnt_out_ref[...] = jnp.broadcast_to(cnt_ref[...], cnt_out_ref.shape)


def _post_mixer_specs(n_batch):
    grp = lambda i: (_mod_group(i, n_batch), 0, 0)
    row = lambda i: (i, 0)
    const = lambda i: (0, 0)
    in_specs = [pl.BlockSpec((TM, D), row),
                pl.BlockSpec((1, 1, D), grp),
                pl.BlockSpec((1, D), const),
                pl.BlockSpec((1, D), const),
                pl.BlockSpec((1, 1, D), grp),
                pl.BlockSpec((1, 1, D), grp),
                pl.BlockSpec((D, ROUTER_PAD), const),
                pl.BlockSpec((1, ROUTER_PAD), const)]
    out_specs = [pl.BlockSpec((TM, D), row),
                 pl.BlockSpec((TM, D), row),
                 pl.BlockSpec((TM, ROUTER_PAD), row),
                 pl.BlockSpec((8, ROUTER_PAD), const)]
    return in_specs, out_specs


def _post_mixer_out_shape(n_rows):
    return [jax.ShapeDtypeStruct((n_rows, D), F32),
            jax.ShapeDtypeStruct((n_rows, D), BF16),
            jax.ShapeDtypeStruct((n_rows, ROUTER_PAD), F32),
            jax.ShapeDtypeStruct((8, ROUTER_PAD), F32)]


def _post_mixer_scratch():
    return [pltpu.VMEM((1, ROUTER_PAD), F32)]


def _out_proj_kernel(o_ref, wo_ref, *rest):
    _post_mixer(_dot(o_ref[...], wo_ref[...]), *rest)


def _out_proj(o, wo, x, g1, lng, lnb, sc2, sh2, wr, br):
    n_rows = x.shape[0]
    n_batch = g1.shape[0] - 1
    pm_in, pm_out = _post_mixer_specs(n_batch)
    return pl.pallas_call(
        _out_proj_kernel,
        grid=(n_rows // TM,),
        in_specs=[pl.BlockSpec((TM, D), lambda i: (i, 0)),
                  pl.BlockSpec((D, D), lambda i: (0, 0))] + pm_in,
        out_specs=pm_out,
        out_shape=_post_mixer_out_shape(n_rows),
        scratch_shapes=_post_mixer_scratch(),
        compiler_params=_cparams(1),
        name="out_proj",
    )(o, wo, x, g1, lng, lnb, sc2, sh2, wr, br)


def _na_key_row_start(j, rows):
    return jnp.clip(NA_QROWS * j - NA_KH // 2, 0, rows - NA_KROWS)


def _na_bias_table(rpb):
    rows = L // GRID_W
    n_blocks = rows // NA_QROWS
    j = jnp.array([0, 1, n_blocks - 1])
    qr = (NA_QROWS * j)[:, None] + jnp.arange(NA_QROWS)[None, :]
    kr = _na_key_row_start(j, rows)[:, None] + jnp.arange(NA_KROWS)[None, :]
    r0 = jnp.clip(qr - NA_KH // 2, 0, rows - NA_KH)
    row_ok = (kr[:, None, :] >= r0[:, :, None]) & (kr[:, None, :] < r0[:, :, None] + NA_KH)
    dr = jnp.clip(kr[:, None, :] - qr[:, :, None] + NA_KH - 1, 0, 2 * NA_KH - 2)
    cols = jnp.arange(GRID_W)
    col_start = jnp.clip(cols - NA_KW // 2, 0, GRID_W - NA_KW)
    col_ok = (cols[None, :] >= col_start[:, None]) & (cols[None, :] < col_start[:, None] + NA_KW)
    dc = jnp.clip(cols[None, :] - cols[:, None] + NA_KW - 1, 0, 2 * NA_KW - 2)
    by_row = jnp.where(col_ok, rpb[:, :, dc], NEG_INF)
    bias = jnp.where(row_ok[None, :, :, :, None, None], by_row[:, dr], NEG_INF)
    bias = jnp.transpose(bias, (0, 1, 2, 4, 3, 5))
    return bias.reshape(NA_HEADS, 3, NA_QROWS * GRID_W, NA_KROWS * GRID_W)


def _softmax_pv(parts, values):
    m = parts[0].max(axis=-1, keepdims=True)
    for s in parts[1:]:
        m = jnp.maximum(m, s.max(axis=-1, keepdims=True))
    den = None
    acc = None
    for s, v in zip(parts, values):
        p = jnp.exp(s - m)
        d = p.sum(axis=-1, keepdims=True)
        o = _dot(p.astype(BF16), v)
        den = d if den is None else den + d
        acc = o if acc is None else acc + o
    return acc / den


def _na_kernel(q_ref, k_ref, v_ref, bias_ref, o_ref):
    rows = L // GRID_W
    qb = NA_QROWS * GRID_W
    kb = NA_KROWS * GRID_W
    kc = k_ref[L:S, :]
    vc = v_ref[L:S, :]

    n_blocks = rows // NA_QROWS

    def block(j, carry):
        qs = pl.multiple_of(j * qb, qb)
        ks = pl.multiple_of(_na_key_row_start(j, rows) * GRID_W, GRID_W)
        kind = jnp.where(j == 0, 0, jnp.where(j == n_blocks - 1, 2, 1))
        q = q_ref[pl.ds(qs, qb), :]
        s_loc = _dot_t(q, k_ref[pl.ds(ks, kb), :]) * SCALE + bias_ref[0, kind]
        s_ctx = _dot_t(q, kc) * SCALE
        o = _softmax_pv([s_loc, s_ctx], [v_ref[pl.ds(ks, kb), :], vc])
        o_ref[pl.ds(qs, qb), :] = o.astype(o_ref.dtype)
        return carry

    lax.fori_loop(0, rows // NA_QROWS, block, 0)
    s = _dot_t(q_ref[L:S, :], kc) * SCALE
    o_ref[L:S, :] = _softmax_pv([s], [vc]).astype(o_ref.dtype)


def _na_attention(qkv, bias, n_batch):
    h = NA_HEADS
    n_blocks = bias.shape[1]
    blk = lambda off: pl.BlockSpec((S, HEAD_DIM), lambda hh, b: (b, off + hh))
    return pl.pallas_call(
        _na_kernel,
        grid=(h, n_batch),
        in_specs=[blk(0), blk(h), blk(2 * h),
                  pl.BlockSpec((1, n_blocks) + bias.shape[2:], lambda hh, b: (hh, 0, 0, 0))],
        out_specs=pl.BlockSpec((S, HEAD_DIM), lambda hh, b: (b, hh)),
        out_shape=jax.ShapeDtypeStruct((n_batch * S, h * HEAD_DIM), BF16),
        compiler_params=_cparams(2),
        name="na_attn",
    )(qkv, qkv, qkv, bias)


def _swa_mask_table():
    span = SWA_BLOCK + 2 * SWA_WINDOW
    qi = jnp.arange(SWA_BLOCK)[:, None]
    kj = jnp.arange(span)[None, :]
    band = jnp.abs(qi - (kj - SWA_WINDOW)) <= SWA_WINDOW
    first = band & (kj >= SWA_WINDOW)
    last = band & (kj < SWA_BLOCK + SWA_WINDOW)
    m = jnp.stack([first, band, last]).astype(F32)
    m = jnp.where(m > 0, 0.0, NEG_INF).astype(F32)
    return jnp.tile(m, (1, SWA_REP, 1))


def _swa_kernel(sink_ref, q_ref, k_ref, v_ref, mask_ref, o_ref, kpad_ref, vpad_ref):
    g = pl.program_id(1)
    nb = L // SWA_BLOCK
    span = SWA_BLOCK + 2 * SWA_WINDOW
    pad = SWA_WINDOW
    zeros = jnp.zeros((pad, HEAD_DIM), BF16)
    for src, dst in ((k_ref, kpad_ref), (v_ref, vpad_ref)):
        dst[0:pad, :] = zeros
        dst[pad + L:, :] = zeros
        dst[pad:pad + L, :] = src[0:L, :]
    kc = k_ref[L:S, :]
    vc = v_ref[L:S, :]

    def sink_column(n_rows):
        return jnp.concatenate(
            [jnp.full((n_rows, 1), sink_ref[g * SWA_REP + r], F32) for r in range(SWA_REP)], axis=0)

    def attend(parts, values, sink_col):
        m = sink_col
        for s in parts:
            m = jnp.maximum(m, s.max(axis=-1, keepdims=True))
        den = jnp.exp(sink_col - m)
        acc = None
        for s, v in zip(parts, values):
            p = jnp.exp(s - m)
            den = den + p.sum(axis=-1, keepdims=True)
            o = _dot(p.astype(BF16), v)
            acc = o if acc is None else acc + o
        return acc / den

    def stacked_q(start, n_rows):
        return jnp.concatenate(
            [q_ref[pl.ds(start, n_rows), r * HEAD_DIM:(r + 1) * HEAD_DIM] for r in range(SWA_REP)], axis=0)

    def store(start, n_rows, o):
        for r in range(SWA_REP):
            o_ref[pl.ds(start, n_rows), r * HEAD_DIM:(r + 1) * HEAD_DIM] = (
                o[r * n_rows:(r + 1) * n_rows].astype(o_ref.dtype))

    sink_blk = sink_column(SWA_BLOCK)

    def block(n, carry):
        start = pl.multiple_of(n * SWA_BLOCK, SWA_BLOCK)
        q = stacked_q(start, SWA_BLOCK)
        kind = jnp.where(n == 0, 0, jnp.where(n == nb - 1, 2, 1))
        s_loc = _dot_t(q, kpad_ref[pl.ds(start, span), :]) * SCALE + mask_ref[kind]
        s_ctx = _dot_t(q, kc) * SCALE
        store(start, SWA_BLOCK, attend([s_loc, s_ctx], [vpad_ref[pl.ds(start, span), :], vc], sink_blk))
        return carry

    lax.fori_loop(0, nb, block, 0)
    qc = stacked_q(L, C)
    store(L, C, attend([_dot_t(qc, kc) * SCALE], [vc], sink_column(C)))


def _swa_attention(qk, v, sink, mask, n_batch):
    qw = SWA_REP * HEAD_DIM
    n_q_blocks = SWA_KV_HEADS
    return pl.pallas_call(
        _swa_kernel,
        grid_spec=pltpu.PrefetchScalarGridSpec(
            num_scalar_prefetch=1,
            grid=(n_batch, SWA_KV_HEADS),
            in_specs=[pl.BlockSpec((S, qw), lambda b, g, sk: (b, g)),
                      pl.BlockSpec((S, HEAD_DIM), lambda b, g, sk: (b, n_q_blocks * SWA_REP + g)),
                      pl.BlockSpec((S, HEAD_DIM), lambda b, g, sk: (b, g)),
                      pl.BlockSpec(mask.shape, lambda b, g, sk: (0, 0, 0))],
            out_specs=pl.BlockSpec((S, qw), lambda b, g, sk: (b, g)),
            scratch_shapes=[pltpu.VMEM((L + 2 * SWA_WINDOW, HEAD_DIM), BF16),
                            pltpu.VMEM((L + 2 * SWA_WINDOW, HEAD_DIM), BF16)]),
        out_shape=jax.ShapeDtypeStruct((n_batch * S, SWA_KV_HEADS * qw), BF16),
        compiler_params=_cparams(2),
        name="swa_attn",
    )(sink, qk, qk, v, mask)


def _diff_kernel(lam_ref, g_ref, q_ref, k_ref, v_ref, o_ref, *, lambda_init, tq):
    lam = lam_ref[...]
    lmbda = (jnp.exp(jnp.sum(lam[0:1] * lam[1:2], axis=-1, keepdims=True))
             - jnp.exp(jnp.sum(lam[2:3] * lam[3:4], axis=-1, keepdims=True)) + lambda_init)
    gain = g_ref[...] * (1.0 - lambda_init)

    def softmax_parts(s):
        m = s.max(axis=-1, keepdims=True)
        e = jnp.exp2((s - m) * (SCALE * LOG2E))
        return e, e.sum(axis=-1, keepdims=True)

    def attend(start, n_rows, k_lo, n_keys):
        q = q_ref[pl.ds(start, n_rows), :]
        e1, d1 = softmax_parts(_dot_t(q[:, :HEAD_DIM], k_ref[pl.ds(k_lo, n_keys), 0:HEAD_DIM]))
        e2, d2 = softmax_parts(_dot_t(q[:, HEAD_DIM:], k_ref[pl.ds(k_lo, n_keys), HEAD_DIM:2 * HEAD_DIM]))
        pd = e1 * (1.0 / d1) - e2 * (lmbda / d2)
        o = _dot(pd.astype(BF16), v_ref[pl.ds(k_lo, n_keys), :])
        o = o * lax.rsqrt(jnp.mean(o * o, axis=-1, keepdims=True) + LN_EPS) * gain
        o_ref[pl.ds(start, n_rows), :] = o.astype(o_ref.dtype)

    def block(i, carry):
        attend(pl.multiple_of(i * tq, tq), tq, 0, S)
        return carry

    lax.fori_loop(0, L // tq, block, 0)
    attend(L, C, L, C)


def _diff_attention(qk, v, lam, subln_g, lambda_init, n_batch):
    hw = 2 * HEAD_DIM
    return pl.pallas_call(
        functools.partial(_diff_kernel, lambda_init=lambda_init, tq=256),
        grid=(n_batch, DIFF_HEADS),
        in_specs=[pl.BlockSpec((4, HEAD_DIM), lambda b, h: (0, 0)),
                  pl.BlockSpec((1, hw), lambda b, h: (0, 0)),
                  pl.BlockSpec((S, hw), lambda b, h: (b, h)),
                  pl.BlockSpec((S, hw), lambda b, h: (b, DIFF_HEADS + h)),
                  pl.BlockSpec((S, hw), lambda b, h: (b, h))],
        out_specs=pl.BlockSpec((S, hw), lambda b, h: (b, h)),
        out_shape=jax.ShapeDtypeStruct((n_batch * S, DIFF_HEADS * hw), BF16),
        compiler_params=_cparams(2),
        name="diff_attn",
    )(lam, subln_g.reshape(1, hw), qk, qk, v)


CONV_ROWS = 128


def _conv_kernel(hc_ref, hp_ref, hn_ref, dw_ref, dwb_ref, cg_ref, cb_ref, wout_ref, bout_ref, *rest):
    win_ref, cv_ref = rest[-2], rest[-1]
    t = pl.program_id(0) % TILES_PER_BATCH
    first = (t == 0) | (t == TILES_PER_BATCH - 1)
    last = t >= TILES_PER_BATCH - 2
    win_ref[0:CONV_HALO, :] = jnp.where(first, 0.0, hp_ref[...])
    win_ref[CONV_HALO:CONV_HALO + TM, :] = hc_ref[...]
    win_ref[CONV_HALO + TM:, :] = jnp.where(last, 0.0, hn_ref[...])
    base = CONV_HALO - CONV_WIDTH // 2

    def strip(c, carry):
        cs = pl.ds(pl.multiple_of(c * HEAD_DIM, HEAD_DIM), HEAD_DIM)
        for r in range(TM // CONV_ROWS):
            acc = jnp.zeros((CONV_ROWS, HEAD_DIM), F32)
            for j in range(CONV_WIDTH):
                acc = acc + dw_ref[j:j + 1, cs] * win_ref[pl.ds(r * CONV_ROWS + base + j, CONV_ROWS), cs]
            cv_ref[r * CONV_ROWS:(r + 1) * CONV_ROWS, cs] = acc
        return carry

    lax.fori_loop(0, D // HEAD_DIM, strip, 0)
    hn = _layer_norm(cv_ref[...] + dwb_ref[...], cg_ref[...], cb_ref[...])
    hn = hn * _sigmoid(hn)
    y = _dot(hn.astype(BF16), wout_ref[...]) + bout_ref[...]
    _post_mixer(y, *rest[:-2])


def _conv_mixer(h, dw, dw_b, cg, cb, wout, bout, x, g1, lng, lnb, sc2, sh2, wr, br):
    n_rows = x.shape[0]
    n_batch = g1.shape[0] - 1
    n_tiles = n_rows // TM
    halo_per_tile = TM // CONV_HALO
    n_halo = n_rows // CONV_HALO
    pm_in, pm_out = _post_mixer_specs(n_batch)
    const = lambda i: (0, 0)
    vec = lambda a: a.reshape(1, D)
    dw_pad = jnp.concatenate([dw, jnp.zeros((1, D), dw.dtype)], axis=0)
    return pl.pallas_call(
        _conv_kernel,
        grid=(n_tiles,),
        in_specs=[pl.BlockSpec((TM, D), lambda i: (i, 0)),
                  pl.BlockSpec((CONV_HALO, D), lambda i: (jnp.maximum(i * halo_per_tile - 1, 0), 0)),
                  pl.BlockSpec((CONV_HALO, D), lambda i: (jnp.minimum((i + 1) * halo_per_tile, n_halo - 1), 0)),
                  pl.BlockSpec((CONV_WIDTH + 1, D), const),
                  pl.BlockSpec((1, D), const),
                  pl.BlockSpec((1, D), const),
                  pl.BlockSpec((1, D), const),
                  pl.BlockSpec((D, D), const),
                  pl.BlockSpec((1, D), const)] + pm_in,
        out_specs=pm_out,
        out_shape=_post_mixer_out_shape(n_rows),
        scratch_shapes=_post_mixer_scratch() + [pltpu.VMEM((TM + 2 * CONV_HALO, D), F32),
                                                pltpu.VMEM((TM, D), F32)],
        compiler_params=_cparams(1),
        name="conv_mixer",
    )(h, h, h, dw_pad, vec(dw_b), vec(cg), vec(cb), wout, vec(bout), x, g1, lng, lnb, sc2, sh2, wr, br)


def _moe_kernel(be_ref, nu_ref, xs_ref, w13_ref, w2_ref, ys_ref, w13b_ref, w2b_ref):
    i = pl.program_id(0)

    @pl.when(i < nu_ref[0])
    def _():
        changed = (i == 0) | (be_ref[i] != be_ref[jnp.maximum(i - 1, 0)])

        @pl.when(changed)
        def _():
            w13b_ref[...] = w13_ref[0].astype(BF16)
            w2b_ref[...] = w2_ref[0].astype(BF16)

        a = _dot(xs_ref[...], w13b_ref[...])
        gate = a[:, :MOE_D_FF]
        hmid = (gate * _sigmoid(gate) * a[:, MOE_D_FF:]).astype(BF16)
        ys_ref[...] = _dot(hmid, w2b_ref[...])


def _moe_experts(xs, blk_expert, n_used, w13, w2):
    n_slots = xs.shape[0]
    n_blocks = n_slots // MOE_BLOCK
    live = lambda i, nu: jnp.minimum(i, nu[0] - 1)
    return pl.pallas_call(
        _moe_kernel,
        grid_spec=pltpu.PrefetchScalarGridSpec(
            num_scalar_prefetch=2,
            grid=(n_blocks,),
            in_specs=[pl.BlockSpec((MOE_BLOCK, D), lambda i, be, nu: (live(i, nu), 0)),
                      pl.BlockSpec((1, D, 2 * MOE_D_FF), lambda i, be, nu: (be[live(i, nu)], 0, 0)),
                      pl.BlockSpec((1, MOE_D_FF, D), lambda i, be, nu: (be[live(i, nu)], 0, 0))],
            out_specs=pl.BlockSpec((MOE_BLOCK, D), lambda i, be, nu: (live(i, nu), 0)),
            scratch_shapes=[pltpu.VMEM((D, 2 * MOE_D_FF), BF16),
                            pltpu.VMEM((MOE_D_FF, D), BF16)]),
        out_shape=jax.ShapeDtypeStruct((n_slots, D), F32),
        compiler_params=_cparams(1),
        name="moe_experts",
    )(blk_expert, n_used, xs, w13, w2)


def _combine_kernel(x_ref, ya_ref, yb_ref, route_ref, g_ref, lng_ref, lnb_ref, o_ref):
    route = route_ref[...]
    f = (ya_ref[...] * route[:, ROUTE_GATE:ROUTE_GATE + 1]
         + yb_ref[...] * route[:, ROUTE_GATE + 1:ROUTE_GATE + 2])
    z = DEEPNORM_ALPHA * x_ref[...] + g_ref[0] * f
    o_ref[...] = _layer_norm(z, lng_ref[...], lnb_ref[...])


def _combine(x1, ya, yb, route, g2, lng, lnb):
    n_rows = x1.shape[0]
    n_batch = g2.shape[0] - 1
    row = pl.BlockSpec((TM, D), lambda i: (i, 0))
    const = pl.BlockSpec((1, D), lambda i: (0, 0))
    return pl.pallas_call(
        _combine_kernel,
        grid=(n_rows // TM,),
        in_specs=[row, row, row,
                  pl.BlockSpec((TM, ROUTER_PAD), lambda i: (i, 0)),
                  pl.BlockSpec((1, 1, D), lambda i: (_mod_group(i, n_batch), 0, 0)),
                  const, const],
        out_specs=row,
        out_shape=jax.ShapeDtypeStruct((n_rows, D), F32),
        compiler_params=_cparams(1),
        name="moe_combine",
    )(x1, ya, yb, route, g2, lng, lnb)


def _dispatch_plan(route, counts):
    n = route.shape[0]
    a = n * MOE_TOP_K
    experts = jnp.arange(MOE_EXPERTS, dtype=jnp.int32)
    counts = counts[0, MOE_GROUPS:MOE_GROUPS + MOE_EXPERTS].astype(jnp.int32)
    padded = (counts + MOE_BLOCK - 1) // MOE_BLOCK * MOE_BLOCK
    pad_end = jnp.cumsum(padded)
    pad_start = pad_end - padded
    expert = route[:, ROUTE_EXPERT:ROUTE_EXPERT + MOE_TOP_K].astype(jnp.int32)
    rank = route[:, ROUTE_RANK:ROUTE_RANK + MOE_TOP_K].astype(jnp.int32)
    slot = jnp.sum(jnp.where(expert[:, :, None] == experts, pad_start, 0), axis=-1) + rank
    n_blocks = -(-a // MOE_BLOCK) + MOE_EXPERTS
    tok_of = jnp.broadcast_to(jnp.arange(n, dtype=jnp.int32)[:, None], (n, MOE_TOP_K))
    tok = jnp.zeros((n_blocks * MOE_BLOCK,), jnp.int32).at[slot.reshape(a)].set(
        tok_of.reshape(a), unique_indices=True, mode='promise_in_bounds')
    blk_start = jnp.arange(n_blocks, dtype=jnp.int32) * MOE_BLOCK
    blk_expert = jnp.minimum(jnp.sum((pad_end[None, :] <= blk_start[:, None]).astype(jnp.int32), axis=-1),
                             MOE_EXPERTS - 1)
    n_used = (pad_end[-1] // MOE_BLOCK).reshape(1)
    return slot, tok, blk_expert, n_used


def _take_rows(a, idx):
    return a.at[idx].get(mode='promise_in_bounds')


def _moe_layer(x1, u2, route, counts, g2, lng, lnb, w13, w2):
    slot, tok, blk_expert, n_used = _dispatch_plan(route, counts)
    ys = _moe_experts(_take_rows(u2, tok), blk_expert, n_used, w13, w2)
    return _combine(x1, _take_rows(ys, slot[:, 0]), _take_rows(ys, slot[:, 1]), route, g2, lng, lnb)


def _rope_tables():
    t = jnp.arange(L, dtype=jnp.int32)
    pos = jnp.stack([t // GRID_W, t % GRID_W], -1).astype(F32)
    n_freq = HEAD_DIM // 4
    inv_freq = ROPE_THETA ** (-jnp.arange(n_freq, dtype=F32) / n_freq)
    ang = pos[:, :, None] * inv_freq
    cos, sin = jnp.cos(ang), jnp.sin(ang)
    cos_t = jnp.stack([cos, cos], axis=2).reshape(L, HEAD_DIM)
    sin_t = jnp.stack([-sin, sin], axis=2).reshape(L, HEAD_DIM)
    cos_t = jnp.concatenate([cos_t, jnp.ones((C, HEAD_DIM), F32)], axis=0)
    sin_t = jnp.concatenate([sin_t, jnp.zeros((C, HEAD_DIM), F32)], axis=0)
    return cos_t, sin_t


def _router_params(rg_w, rg_b, re_w, re_b):
    n = MOE_GROUPS + MOE_EXPERTS
    w = jnp.concatenate([rg_w, re_w, jnp.zeros((D, ROUTER_PAD - n), F32)], axis=1).astype(BF16)
    b = jnp.concatenate([rg_b, re_b, jnp.zeros((ROUTER_PAD - n,), F32)]).reshape(1, ROUTER_PAD)
    return w, b


def _modulation_inputs(c, c_ctx):
    n = c.shape[0] + 1
    pad = -n % 8
    return jnp.concatenate([c, c_ctx[None, :], jnp.zeros((pad, D), F32)], axis=0)


def _hybrid_layer(idx, xs, cvec, n_batch, mod_w, mod_b, mixer, ln1_g, ln1_b, moe, ln2_g, ln2_b):
    vec = lambda a: a.reshape(1, D)
    m = _adaln(cvec, mod_w, mod_b)[:n_batch + 1]
    sh1, sc1, g1, sh2, sc2, g2 = [m[:, None, k * D:(k + 1) * D] for k in range(6)]
    rg_w, rg_b, re_w, re_b, w13, w2 = moe
    wr, br = _router_params(rg_w, rg_b, re_w, re_b)
    post = (xs, g1, vec(ln1_g), vec(ln1_b), sc2, sh2, wr, br)
    kind = idx % 4
    if kind == 0:
        w_qkv, rpb, w_o = mixer
        qkv = _proj(xs, sc1, sh1, w_qkv.astype(BF16), n_out=3 * D, tn=D, out_dtype=BF16)
        o = _na_attention(qkv, _na_bias_table(rpb), n_batch)
        routed = _out_proj(o, w_o.astype(BF16), *post)
    elif kind == 1:
        w_in, b_in, dw, dw_b, cg, cb, w_out, b_out = mixer
        h = _proj(xs, sc1, sh1, w_in.astype(BF16), n_out=D, tn=D // 2, out_dtype=F32, bias=b_in, glu=True)
        routed = _conv_mixer(h, dw, dw_b, cg, cb, w_out.astype(BF16), b_out, *post)
    elif kind == 2:
        w_qkv, sink, w_o = mixer
        n_qk = (SWA_KV_HEADS * SWA_REP + SWA_KV_HEADS) * HEAD_DIM
        n_v = SWA_KV_HEADS * HEAD_DIM
        rope = _rope_tables()
        w_qkv = w_qkv.astype(BF16)
        qk = _proj(xs, sc1, sh1, w_qkv[:, :n_qk], n_out=n_qk, tn=n_qk, out_dtype=BF16, rope=rope)
        v = _proj(xs, sc1, sh1, w_qkv[:, n_qk:], n_out=n_v, tn=n_v, out_dtype=BF16)
        o = _swa_attention(qk, v, sink, _swa_mask_table(), n_batch)
        routed = _out_proj(o, w_o.astype(BF16), *post)
    else:
        w_qkv, lam, subln_g, w_o = mixer
        lambda_init = 0.8 - 0.6 * math.exp(-0.3 * idx)
        rope = _rope_tables()
        w_qkv = w_qkv.astype(BF16)
        qk = _proj(xs, sc1, sh1, w_qkv[:, :2 * D], n_out=2 * D, tn=D, out_dtype=BF16, rope=rope)
        v = _proj(xs, sc1, sh1, w_qkv[:, 2 * D:], n_out=D, tn=D, out_dtype=BF16)
        o = _diff_attention(qk, v, lam, subln_g, lambda_init, n_batch)
        routed = _out_proj(o, w_o.astype(BF16), *post)
    return _moe_layer(*routed, g2, vec(ln2_g), vec(ln2_b), w13, w2)


def kernel(x, c, ctx, c_ctx, l0_mod_w, l0_mod_b, l0_na_w_qkv, l0_na_rpb, l0_na_w_o, l0_ln1_g, l0_ln1_b, l0_router_g_w, l0_router_g_b, l0_router_e_w, l0_router_e_b, l0_moe_w13, l0_moe_w2, l0_ln2_g, l0_ln2_b, l1_mod_w, l1_mod_b, l1_cv_w_in, l1_cv_b_in, l1_cv_dw, l1_cv_dw_b, l1_cv_ln_g, l1_cv_ln_b, l1_cv_w_out, l1_cv_b_out, l1_ln1_g, l1_ln1_b, l1_router_g_w, l1_router_g_b, l1_router_e_w, l1_router_e_b, l1_moe_w13, l1_moe_w2, l1_ln2_g, l1_ln2_b, l2_mod_w, l2_mod_b, l2_sw_w_qkv, l2_sw_sink, l2_sw_w_o, l2_ln1_g, l2_ln1_b, l2_router_g_w, l2_router_g_b, l2_router_e_w, l2_router_e_b, l2_moe_w13, l2_moe_w2, l2_ln2_g, l2_ln2_b, l3_mod_w, l3_mod_b, l3_df_w_qkv, l3_df_lambda, l3_df_subln_g, l3_df_w_o, l3_ln1_g, l3_ln1_b, l3_router_g_w, l3_router_g_b, l3_router_e_w, l3_router_e_b, l3_moe_w13, l3_moe_w2, l3_ln2_g, l3_ln2_b):
    layers = (
        (l0_mod_w, l0_mod_b, (l0_na_w_qkv, l0_na_rpb, l0_na_w_o), l0_ln1_g, l0_ln1_b,
         (l0_router_g_w, l0_router_g_b, l0_router_e_w, l0_router_e_b, l0_moe_w13, l0_moe_w2), l0_ln2_g, l0_ln2_b),
        (l1_mod_w, l1_mod_b, (l1_cv_w_in, l1_cv_b_in, l1_cv_dw, l1_cv_dw_b, l1_cv_ln_g, l1_cv_ln_b, l1_cv_w_out,
                              l1_cv_b_out), l1_ln1_g, l1_ln1_b,
         (l1_router_g_w, l1_router_g_b, l1_router_e_w, l1_router_e_b, l1_moe_w13, l1_moe_w2), l1_ln2_g, l1_ln2_b),
        (l2_mod_w, l2_mod_b, (l2_sw_w_qkv, l2_sw_sink, l2_sw_w_o), l2_ln1_g, l2_ln1_b,
         (l2_router_g_w, l2_router_g_b, l2_router_e_w, l2_router_e_b, l2_moe_w13, l2_moe_w2), l2_ln2_g, l2_ln2_b),
        (l3_mod_w, l3_mod_b, (l3_df_w_qkv, l3_df_lambda, l3_df_subln_g, l3_df_w_o), l3_ln1_g, l3_ln1_b,
         (l3_router_g_w, l3_router_g_b, l3_router_e_w, l3_router_e_b, l3_moe_w13, l3_moe_w2), l3_ln2_g, l3_ln2_b),
    )
    n_batch = x.shape[0]
    assert x.shape[1:] == (L, D) and ctx.shape[1:] == (C, D)
    xs = jnp.concatenate([x, ctx], axis=1).reshape(n_batch * S, D)
    cvec = _modulation_inputs(c, c_ctx)
    for idx in range(DEPTH):
        xs = _hybrid_layer(idx, xs, cvec, n_batch, *layers[idx])
    return xs.reshape(n_batch, S, D)[:, :L]
```

```python
import functools
import math

import jax
import jax.numpy as jnp
from jax import lax
from jax.experimental import pallas as pl
from jax.experimental.pallas import tpu as pltpu

D = 2048
L = 2048
C = 256
S = L + C
DEPTH = 4
GRID_W = 64
HEAD_DIM = 128
ROPE_THETA = 10000.0
LN_EPS = 1e-5
NEG_INF = -1e30
DEEPNORM_ALPHA = (2.0 * DEPTH) ** 0.25
NA_HEADS = 16
NA_KH = 8
NA_KW = 16
NA_QROWS = 4
NA_KROWS = 12
CONV_WIDTH = 31
CONV_HALO = 16
SWA_KV_HEADS = 4
SWA_REP = 4
SWA_WINDOW = 128
SWA_BLOCK = 128
DIFF_HEADS = 8
MOE_GROUPS = 4
MOE_EPG = 8
MOE_EXPERTS = 32
MOE_TOP_K = 2
MOE_D_FF = 512
MOE_BLOCK = 256
ROUTER_PAD = 128

TM = 256
TILES_PER_BATCH = S // TM
LAT_TILES = L // TM
N_STREAMS = 1
VMEM_LIMIT = 52 * 1024 * 1024
SCALE = HEAD_DIM ** -0.5
LOG2E = math.log2(math.e)

F32 = jnp.float32
BF16 = jnp.bfloat16


def _opts(n_axes, *, flops, nbytes, transcendentals=0):
    return dict(
        compiler_params=pltpu.CompilerParams(dimension_semantics=("arbitrary",) * n_axes,
                                             vmem_limit_bytes=VMEM_LIMIT),
        cost_estimate=pl.CostEstimate(flops=int(flops), transcendentals=int(transcendentals),
                                      bytes_accessed=int(nbytes)))


def _dot(a, b):
    return jnp.dot(a, b, preferred_element_type=F32)


def _dot_t(a, b):
    return lax.dot_general(a, b, (((1,), (1,)), ((), ())), preferred_element_type=F32)


def _sigmoid(x):
    return 1.0 / (1.0 + jnp.exp(-x))


def _layer_norm(z, g, b):
    mu = jnp.mean(z, axis=-1, keepdims=True)
    zc = z - mu
    var = jnp.mean(zc * zc, axis=-1, keepdims=True)
    return zc * lax.rsqrt(var + LN_EPS) * g + b


def _mod_group(i, n_batch):
    return jnp.where(i % TILES_PER_BATCH == TILES_PER_BATCH - 1, n_batch, i // TILES_PER_BATCH)


def _adaln_kernel(c_ref, w_ref, b_ref, o_ref):
    c = c_ref[...]
    s = c * _sigmoid(c)
    o_ref[...] = _dot(s.astype(BF16), w_ref[...].astype(BF16)) + b_ref[...]


def _adaln(cvec, w, b):
    r = cvec.shape[0]
    n = w.shape[1]
    tn = 1024
    return pl.pallas_call(
        _adaln_kernel,
        grid=(n // tn,),
        in_specs=[pl.BlockSpec((r, D), lambda j: (0, 0)),
                  pl.BlockSpec((D, tn), lambda j: (0, j)),
                  pl.BlockSpec((1, tn), lambda j: (0, j))],
        out_specs=pl.BlockSpec((r, tn), lambda j: (0, j)),
        out_shape=jax.ShapeDtypeStruct((r, n), F32),
        **_opts(1, flops=2 * r * D * n, nbytes=4 * D * n),
        name="adaln",
    )(cvec, w, b.reshape(1, n))


def _rope_rotate(y, cos, sin):
    lane = lax.broadcasted_iota(jnp.int32, y.shape, 1)
    partner = jnp.where(lane % 64 < 32, pltpu.roll(y, 96, 1), pltpu.roll(y, 32, 1))
    return y * cos + partner * sin


def _proj_kernel(*refs, has_bias, glu, rope, tn):
    it = iter(refs)
    x_ref, sc_ref, sh_ref, w_ref = next(it), next(it), next(it), next(it)
    wg_ref = next(it) if glu else None
    b_ref = next(it) if has_bias else None
    bg_ref = next(it) if glu else None
    cos_ref = next(it) if rope else None
    sin_ref = next(it) if rope else None
    o_ref = next(it)

    u = (x_ref[...] * (1.0 + sc_ref[0]) + sh_ref[0]).astype(BF16)
    y = _dot(u, w_ref[...])
    if has_bias:
        y = y + b_ref[...]
    if glu:
        y = y * _sigmoid(_dot(u, wg_ref[...]) + bg_ref[...])
    if rope:
        cos = cos_ref[...]
        sin = sin_ref[...]
        for h in range(tn // HEAD_DIM):
            sl = slice(h * HEAD_DIM, (h + 1) * HEAD_DIM)
            o_ref[:, sl] = _rope_rotate(y[:, sl], cos, sin).astype(o_ref.dtype)
    else:
        o_ref[...] = y.astype(o_ref.dtype)


def _proj(x, sc, sh, w, *, n_out, tn, out_dtype, bias=None, glu=False, rope=None):
    n_rows = x.shape[0]
    n_batch = sc.shape[0] - 1
    n_tiles = n_rows // TM
    n_col = n_out // tn
    grp = lambda j, i: (_mod_group(i, n_batch), 0, 0)
    in_specs = [pl.BlockSpec((TM, D), lambda j, i: (i, 0)),
                pl.BlockSpec((1, 1, D), grp),
                pl.BlockSpec((1, 1, D), grp),
                pl.BlockSpec((D, tn), lambda j, i: (0, j))]
    args = [x, sc, sh, w]
    if glu:
        in_specs.append(pl.BlockSpec((D, tn), lambda j, i: (0, j + n_col)))
        args.append(w)
    if bias is not None:
        b2 = bias.reshape(1, -1)
        in_specs.append(pl.BlockSpec((1, tn), lambda j, i: (0, j)))
        args.append(b2)
        if glu:
            in_specs.append(pl.BlockSpec((1, tn), lambda j, i: (0, j + n_col)))
            args.append(b2)
    if rope is not None:
        rope_spec = pl.BlockSpec((TM, HEAD_DIM), lambda j, i: (i % TILES_PER_BATCH, 0))
        in_specs += [rope_spec, rope_spec]
        args += [rope[0], rope[1]]
    return pl.pallas_call(
        functools.partial(_proj_kernel, has_bias=bias is not None, glu=glu, rope=rope is not None, tn=tn),
        grid=(n_col, n_tiles),
        in_specs=in_specs,
        out_specs=pl.BlockSpec((TM, tn), lambda j, i: (i, j)),
        out_shape=jax.ShapeDtypeStruct((n_rows, n_out), out_dtype),
        **_opts(2, flops=2 * n_rows * D * n_out * (2 if glu else 1),
                nbytes=4 * n_rows * D * n_col + 2 * D * n_out * (2 if glu else 1) + 4 * n_rows * n_out),
        name="proj",
    )(*args)


ROUTE_EXPERT, ROUTE_GATE, ROUTE_RANK = 0, 2, 4


def _first_lane_where(cond, lane):
    return jnp.min(jnp.where(cond, lane, ROUTER_PAD), axis=-1, keepdims=True)


def _route_tile(lg, cnt_ref):
    lane = lax.broadcasted_iota(jnp.int32, lg.shape, 1)
    gmask = lane < MOE_GROUPS
    gl = jnp.where(gmask, lg, NEG_INF)
    ge = jnp.exp(gl - gl.max(axis=-1, keepdims=True))
    gp = ge / ge.sum(axis=-1, keepdims=True)
    g_p = gp.max(axis=-1, keepdims=True)
    g_idx = _first_lane_where(gmask & (gp == g_p), lane)
    lo = MOE_GROUPS + MOE_EPG * g_idx
    emask = (lane >= lo) & (lane < lo + MOE_EPG)
    el = jnp.where(emask, lg, NEG_INF)
    ee = jnp.exp(el - el.max(axis=-1, keepdims=True))
    ep = jnp.where(emask, ee / ee.sum(axis=-1, keepdims=True), -1.0)
    p1 = ep.max(axis=-1, keepdims=True)
    i1 = _first_lane_where(ep == p1, lane)
    ep2 = jnp.where(lane == i1, -1.0, ep)
    p2 = ep2.max(axis=-1, keepdims=True)
    i2 = _first_lane_where(ep2 == p2, lane)
    den = p1 + p2
    gate1 = g_p * p1 / den
    gate2 = g_p * p2 / den
    oh1 = jnp.where(lane == i1, 1.0, 0.0)
    oh2 = jnp.where(lane == i2, 1.0, 0.0)
    n = lg.shape[0]
    tri = jnp.where(lax.broadcasted_iota(jnp.int32, (n, n), 0) > lax.broadcasted_iota(jnp.int32, (n, n), 1),
                    1.0, 0.0).astype(BF16)
    base = cnt_ref[...]
    tot1 = oh1.sum(axis=0, keepdims=True)
    pre1 = _dot(tri, oh1.astype(BF16)) + base
    pre2 = _dot(tri, oh2.astype(BF16)) + (base + tot1)
    rank1 = (oh1 * pre1).sum(axis=-1, keepdims=True)
    rank2 = (oh2 * pre2).sum(axis=-1, keepdims=True)
    cnt_ref[...] = base + tot1 + oh2.sum(axis=0, keepdims=True)
    cols = ((i1 - MOE_GROUPS).astype(F32), (i2 - MOE_GROUPS).astype(F32), gate1, gate2, rank1, rank2)
    route = jnp.zeros(lg.shape, F32)
    for k, col in enumerate(cols):
        route = jnp.where(lane == k, col, route)
    return route


def _pack_bf16_halves(u):
    bits = pltpu.bitcast(u.astype(F32), jnp.uint32)
    half = u.shape[1] // 2
    return bits[:, :half] | (bits[:, half:] >> 16)


def _unpack_bf16_halves(p):
    hi = pltpu.bitcast(p & jnp.uint32(0xFFFF0000), F32).astype(BF16)
    lo = pltpu.bitcast(p << 16, F32).astype(BF16)
    return hi, lo


def _post_mixer(y, x_ref, g_ref, lng_ref, lnb_ref, sc2_ref, sh2_ref, wr_ref, br_ref,
                x1_ref, u2_ref, route_ref, cnt_out_ref, cnt_ref):
    @pl.when(pl.program_id(0) == 0)
    def _():
        cnt_ref[...] = jnp.zeros(cnt_ref.shape, F32)

    z = DEEPNORM_ALPHA * x_ref[...] + g_ref[0] * y
    x1 = _layer_norm(z, lng_ref[...], lnb_ref[...])
    x1_ref[...] = x1
    u2 = (x1 * (1.0 + sc2_ref[0]) + sh2_ref[0]).astype(BF16)
    u2_ref[...] = _pack_bf16_halves(u2)
    route_ref[...] = _route_tile(_dot(u2, wr_ref[...]) + br_ref[...], cnt_ref)
    cnt_out_ref[...] = jnp.broadcast_to(cnt_ref[...], cnt_out_ref.shape)


def _post_mixer_specs(n_batch):
    grp = lambda i: (_mod_group(i, n_batch), 0, 0)
    row = lambda i: (i, 0)
    const = lambda i: (0, 0)
    in_specs = [pl.BlockSpec((TM, D), row),
                pl.BlockSpec((1, 1, D), grp),
                pl.BlockSpec((1, D), const),
                pl.BlockSpec((1, D), const),
                pl.BlockSpec((1, 1, D), grp),
                pl.BlockSpec((1, 1, D), grp),
                pl.BlockSpec((D, ROUTER_PAD), const),
                pl.BlockSpec((1, ROUTER_PAD), const)]
    out_specs = [pl.BlockSpec((TM, D), row),
                 pl.BlockSpec((TM, D // 2), row),
                 pl.BlockSpec((TM, ROUTER_PAD), row),
                 pl.BlockSpec((8, ROUTER_PAD), const)]
    return in_specs, out_specs


def _post_mixer_out_shape(n_rows):
    return [jax.ShapeDtypeStruct((n_rows, D), F32),
            jax.ShapeDtypeStruct((n_rows, D // 2), jnp.uint32),
            jax.ShapeDtypeStruct((n_rows, ROUTER_PAD), F32),
            jax.ShapeDtypeStruct((8, ROUTER_PAD), F32)]


def _post_mixer_scratch():
    return [pltpu.VMEM((1, ROUTER_PAD), F32)]


def _out_proj_kernel(o_ref, wo_ref, *rest):
    _post_mixer(_dot(o_ref[...], wo_ref[...]), *rest)


def _out_proj(o, wo, x, g1, lng, lnb, sc2, sh2, wr, br):
    n_rows = x.shape[0]
    n_batch = g1.shape[0] - 1
    pm_in, pm_out = _post_mixer_specs(n_batch)
    return pl.pallas_call(
        _out_proj_kernel,
        grid=(n_rows // TM,),
        in_specs=[pl.BlockSpec((TM, D), lambda i: (i, 0)),
                  pl.BlockSpec((D, D), lambda i: (0, 0))] + pm_in,
        out_specs=pm_out,
        out_shape=_post_mixer_out_shape(n_rows),
        scratch_shapes=_post_mixer_scratch(),
        **_opts(1, flops=2 * n_rows * D * D, nbytes=12 * n_rows * D + 2 * D * D),
        name="out_proj",
    )(o, wo, x, g1, lng, lnb, sc2, sh2, wr, br)


def _na_key_row_start(j, rows):
    return jnp.clip(NA_QROWS * j - NA_KH // 2, 0, rows - NA_KROWS)


def _na_bias_table(rpb):
    rows = L // GRID_W
    n_blocks = rows // NA_QROWS
    j = jnp.array([0, 1, n_blocks - 1])
    qr = (NA_QROWS * j)[:, None] + jnp.arange(NA_QROWS)[None, :]
    kr = _na_key_row_start(j, rows)[:, None] + jnp.arange(NA_KROWS)[None, :]
    r0 = jnp.clip(qr - NA_KH // 2, 0, rows - NA_KH)
    row_ok = (kr[:, None, :] >= r0[:, :, None]) & (kr[:, None, :] < r0[:, :, None] + NA_KH)
    dr = jnp.clip(kr[:, None, :] - qr[:, :, None] + NA_KH - 1, 0, 2 * NA_KH - 2)
    cols = jnp.arange(GRID_W)
    col_start = jnp.clip(cols - NA_KW // 2, 0, GRID_W - NA_KW)
    col_ok = (cols[None, :] >= col_start[:, None]) & (cols[None, :] < col_start[:, None] + NA_KW)
    dc = jnp.clip(cols[None, :] - cols[:, None] + NA_KW - 1, 0, 2 * NA_KW - 2)
    by_row = jnp.where(col_ok, rpb[:, :, dc], NEG_INF)
    bias = jnp.where(row_ok[None, :, :, :, None, None], by_row[:, dr], NEG_INF)
    bias = jnp.transpose(bias, (0, 1, 2, 4, 3, 5))
    return bias.reshape(NA_HEADS, 3, NA_QROWS * GRID_W, NA_KROWS * GRID_W)


def _softmax_pv(parts, values):
    m = parts[0].max(axis=-1, keepdims=True)
    for s in parts[1:]:
        m = jnp.maximum(m, s.max(axis=-1, keepdims=True))
    den = None
    acc = None
    for s, v in zip(parts, values):
        p = jnp.exp(s - m)
        d = p.sum(axis=-1, keepdims=True)
        o = _dot(p.astype(BF16), v)
        den = d if den is None else den + d
        acc = o if acc is None else acc + o
    return acc / den


def _na_kernel(q_ref, k_ref, v_ref, bias_ref, o_ref):
    rows = L // GRID_W
    qb = NA_QROWS * GRID_W
    kb = NA_KROWS * GRID_W
    kc = k_ref[L:S, :]
    vc = v_ref[L:S, :]

    n_blocks = rows // NA_QROWS

    def block(j, carry):
        qs = pl.multiple_of(j * qb, qb)
        ks = pl.multiple_of(_na_key_row_start(j, rows) * GRID_W, GRID_W)
        kind = jnp.where(j == 0, 0, jnp.where(j == n_blocks - 1, 2, 1))
        q = q_ref[pl.ds(qs, qb), :]
        s_loc = _dot_t(q, k_ref[pl.ds(ks, kb), :]) * SCALE + bias_ref[0, kind]
        s_ctx = _dot_t(q, kc) * SCALE
        o = _softmax_pv([s_loc, s_ctx], [v_ref[pl.ds(ks, kb), :], vc])
        o_ref[pl.ds(qs, qb), :] = o.astype(o_ref.dtype)
        return carry

    lax.fori_loop(0, rows // NA_QROWS, block, 0, unroll=2)
    s = _dot_t(q_ref[L:S, :], kc) * SCALE
    o_ref[L:S, :] = _softmax_pv([s], [vc]).astype(o_ref.dtype)


def _na_attention(qkv, bias, n_batch):
    h = NA_HEADS
    n_blocks = bias.shape[1]
    blk = lambda off: pl.BlockSpec((S, HEAD_DIM), lambda hh, b: (b, off + hh))
    return pl.pallas_call(
        _na_kernel,
        grid=(h, n_batch),
        in_specs=[blk(0), blk(h), blk(2 * h),
                  pl.BlockSpec((1, n_blocks) + bias.shape[2:], lambda hh, b: (hh, 0, 0, 0))],
        out_specs=pl.BlockSpec((S, HEAD_DIM), lambda hh, b: (b, hh)),
        out_shape=jax.ShapeDtypeStruct((n_batch * S, h * HEAD_DIM), BF16),
        **_opts(2, flops=4 * n_batch * h * S * (NA_KROWS * GRID_W + C) * HEAD_DIM,
                nbytes=8 * n_batch * S * h * HEAD_DIM,
                transcendentals=n_batch * h * S * (NA_KROWS * GRID_W + C)),
        name="na_attn",
    )(qkv, qkv, qkv, bias)


def _swa_mask_table():
    span = SWA_BLOCK + 2 * SWA_WINDOW
    qi = jnp.arange(SWA_BLOCK)[:, None]
    kj = jnp.arange(span)[None, :]
    band = jnp.abs(qi - (kj - SWA_WINDOW)) <= SWA_WINDOW
    first = band & (kj >= SWA_WINDOW)
    last = band & (kj < SWA_BLOCK + SWA_WINDOW)
    m = jnp.stack([first, band, last]).astype(F32)
    m = jnp.where(m > 0, 0.0, NEG_INF).astype(F32)
    return jnp.tile(m, (1, SWA_REP, 1))


def _swa_kernel(sink_ref, q_ref, k_ref, v_ref, mask_ref, o_ref, kpad_ref, vpad_ref):
    g = pl.program_id(1)
    nb = L // SWA_BLOCK
    span = SWA_BLOCK + 2 * SWA_WINDOW
    pad = SWA_WINDOW
    zeros = jnp.zeros((pad, HEAD_DIM), BF16)
    for src, dst in ((k_ref, kpad_ref), (v_ref, vpad_ref)):
        dst[0:pad, :] = zeros
        dst[pad + L:, :] = zeros
        dst[pad:pad + L, :] = src[0:L, :]
    kc = k_ref[L:S, :]
    vc = v_ref[L:S, :]

    def sink_column(n_rows):
        return jnp.concatenate(
            [jnp.full((n_rows, 1), sink_ref[g * SWA_REP + r], F32) for r in range(SWA_REP)], axis=0)

    def attend(parts, values, sink_col):
        m = sink_col
        for s in parts:
            m = jnp.maximum(m, s.max(axis=-1, keepdims=True))
        den = jnp.exp(sink_col - m)
        acc = None
        for s, v in zip(parts, values):
            p = jnp.exp(s - m)
            den = den + p.sum(axis=-1, keepdims=True)
            o = _dot(p.astype(BF16), v)
            acc = o if acc is None else acc + o
        return acc / den

    def stacked_q(start, n_rows):
        return jnp.concatenate(
            [q_ref[pl.ds(start, n_rows), r * HEAD_DIM:(r + 1) * HEAD_DIM] for r in range(SWA_REP)], axis=0)

    def store(start, n_rows, o):
        for r in range(SWA_REP):
            o_ref[pl.ds(start, n_rows), r * HEAD_DIM:(r + 1) * HEAD_DIM] = (
                o[r * n_rows:(r + 1) * n_rows].astype(o_ref.dtype))

    sink_blk = sink_column(SWA_BLOCK)

    def block(n, carry):
        start = pl.multiple_of(n * SWA_BLOCK, SWA_BLOCK)
        q = stacked_q(start, SWA_BLOCK)
        kind = jnp.where(n == 0, 0, jnp.where(n == nb - 1, 2, 1))
        s_loc = _dot_t(q, kpad_ref[pl.ds(start, span), :]) * SCALE + mask_ref[kind]
        s_ctx = _dot_t(q, kc) * SCALE
        store(start, SWA_BLOCK, attend([s_loc, s_ctx], [vpad_ref[pl.ds(start, span), :], vc], sink_blk))
        return carry

    lax.fori_loop(0, nb, block, 0, unroll=2)
    qc = stacked_q(L, C)
    store(L, C, attend([_dot_t(qc, kc) * SCALE], [vc], sink_column(C)))


def _swa_attention(qk, v, sink, mask, n_batch):
    qw = SWA_REP * HEAD_DIM
    n_q_blocks = SWA_KV_HEADS
    return pl.pallas_call(
        _swa_kernel,
        grid_spec=pltpu.PrefetchScalarGridSpec(
            num_scalar_prefetch=1,
            grid=(n_batch, SWA_KV_HEADS),
            in_specs=[pl.BlockSpec((S, qw), lambda b, g, sk: (b, g)),
                      pl.BlockSpec((S, HEAD_DIM), lambda b, g, sk: (b, n_q_blocks * SWA_REP + g)),
                      pl.BlockSpec((S, HEAD_DIM), lambda b, g, sk: (b, g)),
                      pl.BlockSpec(mask.shape, lambda b, g, sk: (0, 0, 0))],
            out_specs=pl.BlockSpec((S, qw), lambda b, g, sk: (b, g)),
            scratch_shapes=[pltpu.VMEM((L + 2 * SWA_WINDOW, HEAD_DIM), BF16),
                            pltpu.VMEM((L + 2 * SWA_WINDOW, HEAD_DIM), BF16)]),
        out_shape=jax.ShapeDtypeStruct((n_batch * S, SWA_KV_HEADS * qw), BF16),
        **_opts(2, flops=4 * n_batch * SWA_KV_HEADS * SWA_REP * S * (SWA_BLOCK + 2 * SWA_WINDOW + C) * HEAD_DIM,
                nbytes=2 * n_batch * S * (2 * SWA_KV_HEADS * qw + 2 * SWA_KV_HEADS * HEAD_DIM),
                transcendentals=n_batch * SWA_KV_HEADS * SWA_REP * S * (SWA_BLOCK + 2 * SWA_WINDOW + C)),
        name="swa_attn",
    )(sink, qk, qk, v, mask)


def _diff_kernel(lam_ref, g_ref, q_ref, k_ref, v_ref, o_ref, *, lambda_init, tq):
    lam = lam_ref[...]
    lmbda = (jnp.exp(jnp.sum(lam[0:1] * lam[1:2], axis=-1, keepdims=True))
             - jnp.exp(jnp.sum(lam[2:3] * lam[3:4], axis=-1, keepdims=True)) + lambda_init)
    gain = g_ref[...] * (1.0 - lambda_init)

    def softmax_parts(s):
        m = s.max(axis=-1, keepdims=True)
        e = jnp.exp2((s - m) * (SCALE * LOG2E))
        return e, e.sum(axis=-1, keepdims=True)

    def attend(start, n_rows, k_lo, n_keys):
        q = q_ref[pl.ds(start, n_rows), :]
        e1, d1 = softmax_parts(_dot_t(q[:, :HEAD_DIM], k_ref[pl.ds(k_lo, n_keys), 0:HEAD_DIM]))
        e2, d2 = softmax_parts(_dot_t(q[:, HEAD_DIM:], k_ref[pl.ds(k_lo, n_keys), HEAD_DIM:2 * HEAD_DIM]))
        pd = e1 * (1.0 / d1) - e2 * (lmbda / d2)
        o = _dot(pd.astype(BF16), v_ref[pl.ds(k_lo, n_keys), :])
        o = o * lax.rsqrt(jnp.mean(o * o, axis=-1, keepdims=True) + LN_EPS) * gain
        o_ref[pl.ds(start, n_rows), :] = o.astype(o_ref.dtype)

    def block(i, carry):
        attend(pl.multiple_of(i * tq, tq), tq, 0, S)
        return carry

    lax.fori_loop(0, L // tq, block, 0, unroll=2)
    attend(L, C, L, C)


def _diff_attention(qk, v, lam, subln_g, lambda_init, n_batch):
    hw = 2 * HEAD_DIM
    return pl.pallas_call(
        functools.partial(_diff_kernel, lambda_init=lambda_init, tq=256),
        grid=(n_batch, DIFF_HEADS),
        in_specs=[pl.BlockSpec((4, HEAD_DIM), lambda b, h: (0, 0)),
                  pl.BlockSpec((1, hw), lambda b, h: (0, 0)),
                  pl.BlockSpec((S, hw), lambda b, h: (b, h)),
                  pl.BlockSpec((S, hw), lambda b, h: (b, DIFF_HEADS + h)),
                  pl.BlockSpec((S, hw), lambda b, h: (b, h))],
        out_specs=pl.BlockSpec((S, hw), lambda b, h: (b, h)),
        out_shape=jax.ShapeDtypeStruct((n_batch * S, DIFF_HEADS * hw), BF16),
        **_opts(2, flops=8 * n_batch * DIFF_HEADS * S * S * HEAD_DIM, nbytes=8 * n_batch * S * DIFF_HEADS * hw,
                transcendentals=2 * n_batch * DIFF_HEADS * S * S),
        name="diff_attn",
    )(lam, subln_g.reshape(1, hw), qk, qk, v)


CONV_ROWS = 128


def _conv_kernel(hc_ref, hp_ref, hn_ref, dw_ref, dwb_ref, cg_ref, cb_ref, wout_ref, bout_ref, *rest):
    win_ref, cv_ref = rest[-2], rest[-1]
    t = pl.program_id(0) % TILES_PER_BATCH
    first = (t == 0) | (t == TILES_PER_BATCH - 1)
    last = t >= TILES_PER_BATCH - 2
    win_ref[0:CONV_HALO, :] = jnp.where(first, 0.0, hp_ref[...])
    win_ref[CONV_HALO:CONV_HALO + TM, :] = hc_ref[...]
    win_ref[CONV_HALO + TM:, :] = jnp.where(last, 0.0, hn_ref[...])
    base = CONV_HALO - CONV_WIDTH // 2

    def strip(c, carry):
        cs = pl.ds(pl.multiple_of(c * HEAD_DIM, HEAD_DIM), HEAD_DIM)
        for r in range(TM // CONV_ROWS):
            acc = jnp.zeros((CONV_ROWS, HEAD_DIM), F32)
            for j in range(CONV_WIDTH):
                acc = acc + dw_ref[j:j + 1, cs] * win_ref[pl.ds(r * CONV_ROWS + base + j, CONV_ROWS), cs]
            cv_ref[r * CONV_ROWS:(r + 1) * CONV_ROWS, cs] = acc
        return carry

    lax.fori_loop(0, D // HEAD_DIM, strip, 0)
    hn = _layer_norm(cv_ref[...] + dwb_ref[...], cg_ref[...], cb_ref[...])
    hn = hn * _sigmoid(hn)
    y = _dot(hn.astype(BF16), wout_ref[...]) + bout_ref[...]
    _post_mixer(y, *rest[:-2])


def _conv_mixer(h, dw, dw_b, cg, cb, wout, bout, x, g1, lng, lnb, sc2, sh2, wr, br):
    n_rows = x.shape[0]
    n_batch = g1.shape[0] - 1
    n_tiles = n_rows // TM
    halo_per_tile = TM // CONV_HALO
    n_halo = n_rows // CONV_HALO
    pm_in, pm_out = _post_mixer_specs(n_batch)
    const = lambda i: (0, 0)
    vec = lambda a: a.reshape(1, D)
    dw_pad = jnp.concatenate([dw, jnp.zeros((1, D), dw.dtype)], axis=0)
    return pl.pallas_call(
        _conv_kernel,
        grid=(n_tiles,),
        in_specs=[pl.BlockSpec((TM, D), lambda i: (i, 0)),
                  pl.BlockSpec((CONV_HALO, D), lambda i: (jnp.maximum(i * halo_per_tile - 1, 0), 0)),
                  pl.BlockSpec((CONV_HALO, D), lambda i: (jnp.minimum((i + 1) * halo_per_tile, n_halo - 1), 0)),
                  pl.BlockSpec((CONV_WIDTH + 1, D), const),
                  pl.BlockSpec((1, D), const),
                  pl.BlockSpec((1, D), const),
                  pl.BlockSpec((1, D), const),
                  pl.BlockSpec((D, D), const),
                  pl.BlockSpec((1, D), const)] + pm_in,
        out_specs=pm_out,
        out_shape=_post_mixer_out_shape(n_rows),
        scratch_shapes=_post_mixer_scratch() + [pltpu.VMEM((TM + 2 * CONV_HALO, D), F32),
                                                pltpu.VMEM((TM, D), F32)],
        **_opts(1, flops=2 * n_rows * D * (D + CONV_WIDTH), nbytes=14 * n_rows * D + 2 * D * D),
        name="conv_mixer",
    )(h, h, h, dw_pad, vec(dw_b), vec(cg), vec(cb), wout, vec(bout), x, g1, lng, lnb, sc2, sh2, wr, br)


def _moe_kernel(be_ref, nu_ref, xs_ref, w13_ref, w2_ref, ys_ref, w13b_ref, w2b_ref):
    i = pl.program_id(0)

    @pl.when(i < nu_ref[0])
    def _():
        changed = (i == 0) | (be_ref[i] != be_ref[jnp.maximum(i - 1, 0)])

        @pl.when(changed)
        def _():
            w13b_ref[...] = w13_ref[0].astype(BF16)
            w2b_ref[...] = w2_ref[0].astype(BF16)

        x_hi, x_lo = _unpack_bf16_halves(xs_ref[...])
        a = _dot(x_hi, w13b_ref[0:D // 2, :]) + _dot(x_lo, w13b_ref[D // 2:, :])
        gate = a[:, :MOE_D_FF]
        hmid = (gate * _sigmoid(gate) * a[:, MOE_D_FF:]).astype(BF16)
        ys_ref[...] = _dot(hmid, w2b_ref[...])


def _moe_experts(xs, blk_expert, n_used, w13, w2):
    n_slots = xs.shape[0]
    n_blocks = n_slots // MOE_BLOCK
    live = lambda i, nu: jnp.minimum(i, nu[0] - 1)
    return pl.pallas_call(
        _moe_kernel,
        grid_spec=pltpu.PrefetchScalarGridSpec(
            num_scalar_prefetch=2,
            grid=(n_blocks,),
            in_specs=[pl.BlockSpec((MOE_BLOCK, D // 2), lambda i, be, nu: (live(i, nu), 0)),
                      pl.BlockSpec((1, D, 2 * MOE_D_FF), lambda i, be, nu: (be[live(i, nu)], 0, 0)),
                      pl.BlockSpec((1, MOE_D_FF, D), lambda i, be, nu: (be[live(i, nu)], 0, 0))],
            out_specs=pl.BlockSpec((MOE_BLOCK, D), lambda i, be, nu: (live(i, nu), 0)),
            scratch_shapes=[pltpu.VMEM((D, 2 * MOE_D_FF), BF16),
                            pltpu.VMEM((MOE_D_FF, D), BF16)]),
        out_shape=jax.ShapeDtypeStruct((n_slots, D), F32),
        **_opts(1, flops=6 * n_slots * D * MOE_D_FF, nbytes=6 * n_slots * D + 12 * MOE_EXPERTS * D * MOE_D_FF),
        name="moe_experts",
    )(blk_expert, n_used, xs, w13, w2)


def _combine_kernel(x_ref, ya_ref, yb_ref, route_ref, g_ref, lng_ref, lnb_ref, o_ref):
    route = route_ref[...]
    f = (ya_ref[...] * route[:, ROUTE_GATE:ROUTE_GATE + 1]
         + yb_ref[...] * route[:, ROUTE_GATE + 1:ROUTE_GATE + 2])
    z = DEEPNORM_ALPHA * x_ref[...] + g_ref[0] * f
    o_ref[...] = _layer_norm(z, lng_ref[...], lnb_ref[...])


def _combine(x1, ya, yb, route, g2, lng, lnb):
    n_rows = x1.shape[0]
    n_batch = g2.shape[0] - 1
    row = pl.BlockSpec((TM, D), lambda i: (i, 0))
    const = pl.BlockSpec((1, D), lambda i: (0, 0))
    return pl.pallas_call(
        _combine_kernel,
        grid=(n_rows // TM,),
        in_specs=[row, row, row,
                  pl.BlockSpec((TM, ROUTER_PAD), lambda i: (i, 0)),
                  pl.BlockSpec((1, 1, D), lambda i: (_mod_group(i, n_batch), 0, 0)),
                  const, const],
        out_specs=row,
        out_shape=jax.ShapeDtypeStruct((n_rows, D), F32),
        **_opts(1, flops=10 * n_rows * D, nbytes=16 * n_rows * D),
        name="moe_combine",
    )(x1, ya, yb, route, g2, lng, lnb)


def _dispatch_plan(route, counts):
    n = route.shape[0]
    a = n * MOE_TOP_K
    experts = jnp.arange(MOE_EXPERTS, dtype=jnp.int32)
    counts = counts[0, MOE_GROUPS:MOE_GROUPS + MOE_EXPERTS].astype(jnp.int32)
    padded = (counts + MOE_BLOCK - 1) // MOE_BLOCK * MOE_BLOCK
    pad_end = jnp.cumsum(padded)
    pad_start = pad_end - padded
    expert = route[:, ROUTE_EXPERT:ROUTE_EXPERT + MOE_TOP_K].astype(jnp.int32)
    rank = route[:, ROUTE_RANK:ROUTE_RANK + MOE_TOP_K].astype(jnp.int32)
    slot = jnp.sum(jnp.where(expert[:, :, None] == experts, pad_start, 0), axis=-1) + rank
    n_blocks = -(-a // MOE_BLOCK) + MOE_EXPERTS
    tok_of = jnp.broadcast_to(jnp.arange(n, dtype=jnp.int32)[:, None], (n, MOE_TOP_K))
    tok = jnp.zeros((n_blocks * MOE_BLOCK,), jnp.int32).at[slot.reshape(a)].set(
        tok_of.reshape(a), unique_indices=True, mode='promise_in_bounds')
    blk_start = jnp.arange(n_blocks, dtype=jnp.int32) * MOE_BLOCK
    blk_expert = jnp.minimum(jnp.sum((pad_end[None, :] <= blk_start[:, None]).astype(jnp.int32), axis=-1),
                             MOE_EXPERTS - 1)
    n_used = (pad_end[-1] // MOE_BLOCK).reshape(1)
    return slot, tok, blk_expert, n_used


def _take_rows(a, idx):
    return a.at[idx].get(mode='promise_in_bounds')


def _moe_layer(x1, u2, route, counts, g2, lng, lnb, w13, w2):
    slot, tok, blk_expert, n_used = _dispatch_plan(route, counts)
    ys = _moe_experts(_take_rows(u2, tok), blk_expert, n_used, w13, w2)
    return _combine(x1, _take_rows(ys, slot[:, 0]), _take_rows(ys, slot[:, 1]), route, g2, lng, lnb)


def _rope_tables():
    t = jnp.arange(L, dtype=jnp.int32)
    pos = jnp.stack([t // GRID_W, t % GRID_W], -1).astype(F32)
    n_freq = HEAD_DIM // 4
    inv_freq = ROPE_THETA ** (-jnp.arange(n_freq, dtype=F32) / n_freq)
    ang = pos[:, :, None] * inv_freq
    cos, sin = jnp.cos(ang), jnp.sin(ang)
    cos_t = jnp.stack([cos, cos], axis=2).reshape(L, HEAD_DIM)
    sin_t = jnp.stack([-sin, sin], axis=2).reshape(L, HEAD_DIM)
    cos_t = jnp.concatenate([cos_t, jnp.ones((C, HEAD_DIM), F32)], axis=0)
    sin_t = jnp.concatenate([sin_t, jnp.zeros((C, HEAD_DIM), F32)], axis=0)
    return cos_t, sin_t


def _router_params(rg_w, rg_b, re_w, re_b):
    n = MOE_GROUPS + MOE_EXPERTS
    w = jnp.concatenate([rg_w, re_w, jnp.zeros((D, ROUTER_PAD - n), F32)], axis=1).astype(BF16)
    b = jnp.concatenate([rg_b, re_b, jnp.zeros((ROUTER_PAD - n,), F32)]).reshape(1, ROUTER_PAD)
    return w, b


def _modulation_inputs(c, c_ctx):
    n = c.shape[0] + 1
    pad = -n % 8
    return jnp.concatenate([c, c_ctx[None, :], jnp.zeros((pad, D), F32)], axis=0)


def _mixer_fn(idx, mixer):
    kind = idx % 4
    if kind == 0:
        w_qkv, rpb, w_o = mixer
        w_qkv, w_o, bias = w_qkv.astype(BF16), w_o.astype(BF16), _na_bias_table(rpb)

        def run(xs, sc1, sh1, n_batch, post):
            qkv = _proj(xs, sc1, sh1, w_qkv, n_out=3 * D, tn=D, out_dtype=BF16)
            return _out_proj(_na_attention(qkv, bias, n_batch), w_o, *post)
    elif kind == 1:
        w_in, b_in, dw, dw_b, cg, cb, w_out, b_out = mixer
        w_in, w_out = w_in.astype(BF16), w_out.astype(BF16)

        def run(xs, sc1, sh1, n_batch, post):
            h = _proj(xs, sc1, sh1, w_in, n_out=D, tn=D // 2, out_dtype=F32, bias=b_in, glu=True)
            return _conv_mixer(h, dw, dw_b, cg, cb, w_out, b_out, *post)
    elif kind == 2:
        w_qkv, sink, w_o = mixer
        n_qk = (SWA_KV_HEADS * SWA_REP + SWA_KV_HEADS) * HEAD_DIM
        n_v = SWA_KV_HEADS * HEAD_DIM
        w_qkv, w_o, rope, mask = w_qkv.astype(BF16), w_o.astype(BF16), _rope_tables(), _swa_mask_table()
        w_qk, w_v = w_qkv[:, :n_qk], w_qkv[:, n_qk:]

        def run(xs, sc1, sh1, n_batch, post):
            qk = _proj(xs, sc1, sh1, w_qk, n_out=n_qk, tn=n_qk, out_dtype=BF16, rope=rope)
            v = _proj(xs, sc1, sh1, w_v, n_out=n_v, tn=n_v, out_dtype=BF16)
            return _out_proj(_swa_attention(qk, v, sink, mask, n_batch), w_o, *post)
    else:
        w_qkv, lam, subln_g, w_o = mixer
        lambda_init = 0.8 - 0.6 * math.exp(-0.3 * idx)
        w_qkv, w_o, rope = w_qkv.astype(BF16), w_o.astype(BF16), _rope_tables()
        w_qk, w_v = w_qkv[:, :2 * D], w_qkv[:, 2 * D:]

        def run(xs, sc1, sh1, n_batch, post):
            qk = _proj(xs, sc1, sh1, w_qk, n_out=2 * D, tn=D, out_dtype=BF16, rope=rope)
            v = _proj(xs, sc1, sh1, w_v, n_out=D, tn=D, out_dtype=BF16)
            return _out_proj(_diff_attention(qk, v, lam, subln_g, lambda_init, n_batch), w_o, *post)
    return run


def _hybrid_layer(idx, streams, cvec, mod_w, mod_b, mixer, ln1_g, ln1_b, moe, ln2_g, ln2_b):
    vec = lambda a: a.reshape(1, D)
    m_all = _adaln(cvec, mod_w, mod_b)
    ctx_row = sum(nb for _, _, nb in streams)
    rg_w, rg_b, re_w, re_b, w13, w2 = moe
    wr, br = _router_params(rg_w, rg_b, re_w, re_b)
    run_mixer = _mixer_fn(idx, mixer)
    out = []
    for xs, b0, nb in streams:
        m = jnp.concatenate([m_all[b0:b0 + nb], m_all[ctx_row:ctx_row + 1]], axis=0)
        sh1, sc1, g1, sh2, sc2, g2 = [m[:, None, k * D:(k + 1) * D] for k in range(6)]
        post = (xs, g1, vec(ln1_g), vec(ln1_b), sc2, sh2, wr, br)
        routed = run_mixer(xs, sc1, sh1, nb, post)
        out.append((_moe_layer(*routed, g2, vec(ln2_g), vec(ln2_b), w13, w2), b0, nb))
    return out


def kernel(x, c, ctx, c_ctx, l0_mod_w, l0_mod_b, l0_na_w_qkv, l0_na_rpb, l0_na_w_o, l0_ln1_g, l0_ln1_b, l0_router_g_w, l0_router_g_b, l0_router_e_w, l0_router_e_b, l0_moe_w13, l0_moe_w2, l0_ln2_g, l0_ln2_b, l1_mod_w, l1_mod_b, l1_cv_w_in, l1_cv_b_in, l1_cv_dw, l1_cv_dw_b, l1_cv_ln_g, l1_cv_ln_b, l1_cv_w_out, l1_cv_b_out, l1_ln1_g, l1_ln1_b, l1_router_g_w, l1_router_g_b, l1_router_e_w, l1_router_e_b, l1_moe_w13, l1_moe_w2, l1_ln2_g, l1_ln2_b, l2_mod_w, l2_mod_b, l2_sw_w_qkv, l2_sw_sink, l2_sw_w_o, l2_ln1_g, l2_ln1_b, l2_router_g_w, l2_router_g_b, l2_router_e_w, l2_router_e_b, l2_moe_w13, l2_moe_w2, l2_ln2_g, l2_ln2_b, l3_mod_w, l3_mod_b, l3_df_w_qkv, l3_df_lambda, l3_df_subln_g, l3_df_w_o, l3_ln1_g, l3_ln1_b, l3_router_g_w, l3_router_g_b, l3_router_e_w, l3_router_e_b, l3_moe_w13, l3_moe_w2, l3_ln2_g, l3_ln2_b):
    layers = (
        (l0_mod_w, l0_mod_b, (l0_na_w_qkv, l0_na_rpb, l0_na_w_o), l0_ln1_g, l0_ln1_b,
         (l0_router_g_w, l0_router_g_b, l0_router_e_w, l0_router_e_b, l0_moe_w13, l0_moe_w2), l0_ln2_g, l0_ln2_b),
        (l1_mod_w, l1_mod_b, (l1_cv_w_in, l1_cv_b_in, l1_cv_dw, l1_cv_dw_b, l1_cv_ln_g, l1_cv_ln_b, l1_cv_w_out,
                              l1_cv_b_out), l1_ln1_g, l1_ln1_b,
         (l1_router_g_w, l1_router_g_b, l1_router_e_w, l1_router_e_b, l1_moe_w13, l1_moe_w2), l1_ln2_g, l1_ln2_b),
        (l2_mod_w, l2_mod_b, (l2_sw_w_qkv, l2_sw_sink, l2_sw_w_o), l2_ln1_g, l2_ln1_b,
         (l2_router_g_w, l2_router_g_b, l2_router_e_w, l2_router_e_b, l2_moe_w13, l2_moe_w2), l2_ln2_g, l2_ln2_b),
        (l3_mod_w, l3_mod_b, (l3_df_w_qkv, l3_df_lambda, l3_df_subln_g, l3_df_w_o), l3_ln1_g, l3_ln1_b,
         (l3_router_g_w, l3_router_g_b, l3_router_e_w, l3_router_e_b, l3_moe_w13, l3_moe_w2), l3_ln2_g, l3_ln2_b),
    )
    n_batch = x.shape[0]
    assert x.shape[1:] == (L, D) and ctx.shape[1:] == (C, D)
    n_streams = N_STREAMS if n_batch % N_STREAMS == 0 else 1
    nb = n_batch // n_streams
    streams = [(jnp.concatenate([x[b0:b0 + nb], ctx[b0:b0 + nb]], axis=1).reshape(nb * S, D), b0, nb)
               for b0 in range(0, n_batch, nb)]
    cvec = _modulation_inputs(c, c_ctx)
    for idx in range(DEPTH):
        streams = _hybrid_layer(idx, streams, cvec, *layers[idx])
    return jnp.concatenate([xs.reshape(nb, S, D)[:, :L] for xs, _, _ in streams], axis=0)
```

```python
import functools
import math

import jax
import jax.numpy as jnp
from jax import lax
from jax.experimental import pallas as pl
from jax.experimental.pallas import tpu as pltpu

D = 2048
L = 2048
C = 256
S = L + C
DEPTH = 4
GRID_W = 64
HEAD_DIM = 128
ROPE_THETA = 10000.0
LN_EPS = 1e-5
NEG_INF = -1e30
DEEPNORM_ALPHA = (2.0 * DEPTH) ** 0.25
NA_HEADS = 16
NA_KH = 8
NA_KW = 16
NA_QROWS = 4
NA_KROWS = 12
CONV_WIDTH = 31
CONV_HALO = 16
SWA_KV_HEADS = 4
SWA_REP = 4
SWA_WINDOW = 128
SWA_BLOCK = 128
DIFF_HEADS = 8
MOE_GROUPS = 4
MOE_EPG = 8
MOE_EXPERTS = 32
MOE_TOP_K = 2
MOE_D_FF = 512
MOE_BLOCK = 256
ROUTER_PAD = 128

TM = 256
SUB_ROWS = 256
TILES_PER_BATCH = S // TM
LAT_TILES = L // TM
N_STREAMS = 1
VMEM_LIMIT = 52 * 1024 * 1024
SCALE = HEAD_DIM ** -0.5
LOG2E = math.log2(math.e)

F32 = jnp.float32
BF16 = jnp.bfloat16


def _opts(n_axes, *, flops, nbytes, transcendentals=0):
    return dict(
        compiler_params=pltpu.CompilerParams(dimension_semantics=("arbitrary",) * n_axes,
                                             vmem_limit_bytes=VMEM_LIMIT),
        cost_estimate=pl.CostEstimate(flops=int(flops), transcendentals=int(transcendentals),
                                      bytes_accessed=int(nbytes)))


def _dot(a, b):
    return jnp.dot(a, b, preferred_element_type=F32)


def _dot_t(a, b):
    return lax.dot_general(a, b, (((1,), (1,)), ((), ())), preferred_element_type=F32)


def _sigmoid(x):
    return 1.0 / (1.0 + jnp.exp(-x))


def _layer_norm(z, g, b):
    mu = jnp.mean(z, axis=-1, keepdims=True)
    zc = z - mu
    var = jnp.mean(zc * zc, axis=-1, keepdims=True)
    return zc * lax.rsqrt(var + LN_EPS) * g + b


def _mod_group(i, n_batch):
    return jnp.where(i % TILES_PER_BATCH == TILES_PER_BATCH - 1, n_batch, i // TILES_PER_BATCH)


def _adaln_kernel(c_ref, w_ref, b_ref, o_ref):
    c = c_ref[...]
    s = c * _sigmoid(c)
    o_ref[...] = _dot(s.astype(BF16), w_ref[...].astype(BF16)) + b_ref[...]


def _adaln(cvec, w, b):
    r = cvec.shape[0]
    n = w.shape[1]
    tn = 1024
    return pl.pallas_call(
        _adaln_kernel,
        grid=(n // tn,),
        in_specs=[pl.BlockSpec((r, D), lambda j: (0, 0)),
                  pl.BlockSpec((D, tn), lambda j: (0, j)),
                  pl.BlockSpec((1, tn), lambda j: (0, j))],
        out_specs=pl.BlockSpec((r, tn), lambda j: (0, j)),
        out_shape=jax.ShapeDtypeStruct((r, n), F32),
        **_opts(1, flops=2 * r * D * n, nbytes=4 * D * n),
        name="adaln",
    )(cvec, w, b.reshape(1, n))


def _rope_rotate(y, cos, sin):
    lane = lax.broadcasted_iota(jnp.int32, y.shape, 1)
    partner = jnp.where(lane % 64 < 32, pltpu.roll(y, 96, 1), pltpu.roll(y, 32, 1))
    return y * cos + partner * sin


def _proj_kernel(*refs, has_bias, glu, rope, tn):
    it = iter(refs)
    x_ref, sc_ref, sh_ref, w_ref = next(it), next(it), next(it), next(it)
    wg_ref = next(it) if glu else None
    b_ref = next(it) if has_bias else None
    bg_ref = next(it) if glu else None
    cos_ref = next(it) if rope else None
    sin_ref = next(it) if rope else None
    o_ref = next(it)

    u = (x_ref[...] * (1.0 + sc_ref[0]) + sh_ref[0]).astype(BF16)
    y = _dot(u, w_ref[...])
    if has_bias:
        y = y + b_ref[...]
    if glu:
        y = y * _sigmoid(_dot(u, wg_ref[...]) + bg_ref[...])
    if rope:
        cos = cos_ref[...]
        sin = sin_ref[...]
        for h in range(tn // HEAD_DIM):
            sl = slice(h * HEAD_DIM, (h + 1) * HEAD_DIM)
            o_ref[:, sl] = _rope_rotate(y[:, sl], cos, sin).astype(o_ref.dtype)
    else:
        o_ref[...] = y.astype(o_ref.dtype)


def _proj(x, sc, sh, w, *, n_out, tn, out_dtype, bias=None, glu=False, rope=None):
    n_rows = x.shape[0]
    n_batch = sc.shape[0] - 1
    n_tiles = n_rows // TM
    n_col = n_out // tn
    grp = lambda j, i: (_mod_group(i, n_batch), 0, 0)
    in_specs = [pl.BlockSpec((TM, D), lambda j, i: (i, 0)),
                pl.BlockSpec((1, 1, D), grp),
                pl.BlockSpec((1, 1, D), grp),
                pl.BlockSpec((D, tn), lambda j, i: (0, j))]
    args = [x, sc, sh, w]
    if glu:
        in_specs.append(pl.BlockSpec((D, tn), lambda j, i: (0, j + n_col)))
        args.append(w)
    if bias is not None:
        b2 = bias.reshape(1, -1)
        in_specs.append(pl.BlockSpec((1, tn), lambda j, i: (0, j)))
        args.append(b2)
        if glu:
            in_specs.append(pl.BlockSpec((1, tn), lambda j, i: (0, j + n_col)))
            args.append(b2)
    if rope is not None:
        rope_spec = pl.BlockSpec((TM, HEAD_DIM), lambda j, i: (i % TILES_PER_BATCH, 0))
        in_specs += [rope_spec, rope_spec]
        args += [rope[0], rope[1]]
    return pl.pallas_call(
        functools.partial(_proj_kernel, has_bias=bias is not None, glu=glu, rope=rope is not None, tn=tn),
        grid=(n_col, n_tiles),
        in_specs=in_specs,
        out_specs=pl.BlockSpec((TM, tn), lambda j, i: (i, j)),
        out_shape=jax.ShapeDtypeStruct((n_rows, n_out), out_dtype),
        **_opts(2, flops=2 * n_rows * D * n_out * (2 if glu else 1),
                nbytes=4 * n_rows * D * n_col + 2 * D * n_out * (2 if glu else 1) + 4 * n_rows * n_out),
        name="proj",
    )(*args)


ROUTE_EXPERT, ROUTE_GATE, ROUTE_RANK = 0, 2, 4


def _first_lane_where(cond, lane):
    return jnp.min(jnp.where(cond, lane, ROUTER_PAD), axis=-1, keepdims=True)


def _route_tile(lg, cnt_ref):
    lane = lax.broadcasted_iota(jnp.int32, lg.shape, 1)
    gmask = lane < MOE_GROUPS
    gl = jnp.where(gmask, lg, NEG_INF)
    ge = jnp.exp(gl - gl.max(axis=-1, keepdims=True))
    gp = ge / ge.sum(axis=-1, keepdims=True)
    g_p = gp.max(axis=-1, keepdims=True)
    g_idx = _first_lane_where(gmask & (gp == g_p), lane)
    lo = MOE_GROUPS + MOE_EPG * g_idx
    emask = (lane >= lo) & (lane < lo + MOE_EPG)
    el = jnp.where(emask, lg, NEG_INF)
    ee = jnp.exp(el - el.max(axis=-1, keepdims=True))
    ep = jnp.where(emask, ee / ee.sum(axis=-1, keepdims=True), -1.0)
    p1 = ep.max(axis=-1, keepdims=True)
    i1 = _first_lane_where(ep == p1, lane)
    ep2 = jnp.where(lane == i1, -1.0, ep)
    p2 = ep2.max(axis=-1, keepdims=True)
    i2 = _first_lane_where(ep2 == p2, lane)
    den = p1 + p2
    gate1 = g_p * p1 / den
    gate2 = g_p * p2 / den
    oh1 = jnp.where(lane == i1, 1.0, 0.0)
    oh2 = jnp.where(lane == i2, 1.0, 0.0)
    n = lg.shape[0]
    tri = jnp.where(lax.broadcasted_iota(jnp.int32, (n, n), 0) > lax.broadcasted_iota(jnp.int32, (n, n), 1),
                    1.0, 0.0).astype(BF16)
    base = cnt_ref[...]
    tot1 = oh1.sum(axis=0, keepdims=True)
    pre1 = _dot(tri, oh1.astype(BF16)) + base
    pre2 = _dot(tri, oh2.astype(BF16)) + (base + tot1)
    rank1 = (oh1 * pre1).sum(axis=-1, keepdims=True)
    rank2 = (oh2 * pre2).sum(axis=-1, keepdims=True)
    cnt_ref[...] = base + tot1 + oh2.sum(axis=0, keepdims=True)
    cols = ((i1 - MOE_GROUPS).astype(F32), (i2 - MOE_GROUPS).astype(F32), gate1, gate2, rank1, rank2)
    route = jnp.zeros(lg.shape, F32)
    for k, col in enumerate(cols):
        route = jnp.where(lane == k, col, route)
    return route


def _pack_bf16_halves(u):
    bits = pltpu.bitcast(u.astype(F32), jnp.uint32)
    half = u.shape[1] // 2
    return bits[:, :half] | (bits[:, half:] >> 16)


def _unpack_bf16_halves(p):
    hi = pltpu.bitcast(p & jnp.uint32(0xFFFF0000), F32).astype(BF16)
    lo = pltpu.bitcast(p << 16, F32).astype(BF16)
    return hi, lo


def _post_mixer(mixer_out, x_ref, g_ref, lng_ref, lnb_ref, sc2_ref, sh2_ref, wr_ref, br_ref,
                x1_ref, u2_ref, route_ref, cnt_out_ref, cnt_ref):
    @pl.when(pl.program_id(0) == 0)
    def _():
        cnt_ref[...] = jnp.zeros(cnt_ref.shape, F32)

    for h in range(TM // SUB_ROWS):
        rows = slice(h * SUB_ROWS, (h + 1) * SUB_ROWS)
        z = DEEPNORM_ALPHA * x_ref[rows, :] + g_ref[0] * mixer_out(rows)
        x1 = _layer_norm(z, lng_ref[...], lnb_ref[...])
        x1_ref[rows, :] = x1
        u2 = (x1 * (1.0 + sc2_ref[0]) + sh2_ref[0]).astype(BF16)
        u2_ref[rows, :] = _pack_bf16_halves(u2)
        route_ref[rows, :] = _route_tile(_dot(u2, wr_ref[...]) + br_ref[...], cnt_ref)
    cnt_out_ref[...] = jnp.broadcast_to(cnt_ref[...], cnt_out_ref.shape)


def _post_mixer_specs(n_batch):
    grp = lambda i: (_mod_group(i, n_batch), 0, 0)
    row = lambda i: (i, 0)
    const = lambda i: (0, 0)
    in_specs = [pl.BlockSpec((TM, D), row),
                pl.BlockSpec((1, 1, D), grp),
                pl.BlockSpec((1, D), const),
                pl.BlockSpec((1, D), const),
                pl.BlockSpec((1, 1, D), grp),
                pl.BlockSpec((1, 1, D), grp),
                pl.BlockSpec((D, ROUTER_PAD), const),
                pl.BlockSpec((1, ROUTER_PAD), const)]
    out_specs = [pl.BlockSpec((TM, D), row),
                 pl.BlockSpec((TM, D // 2), row),
                 pl.BlockSpec((TM, ROUTER_PAD), row),
                 pl.BlockSpec((8, ROUTER_PAD), const)]
    return in_specs, out_specs


def _post_mixer_out_shape(n_rows):
    return [jax.ShapeDtypeStruct((n_rows, D), F32),
            jax.ShapeDtypeStruct((n_rows, D // 2), jnp.uint32),
            jax.ShapeDtypeStruct((n_rows, ROUTER_PAD), F32),
            jax.ShapeDtypeStruct((8, ROUTER_PAD), F32)]


def _post_mixer_scratch():
    return [pltpu.VMEM((1, ROUTER_PAD), F32)]


def _out_proj_kernel(o_ref, wo_ref, *rest):
    _post_mixer(lambda rows: _dot(o_ref[rows, :], wo_ref[...]), *rest)


def _out_proj(o, wo, x, g1, lng, lnb, sc2, sh2, wr, br):
    n_rows = x.shape[0]
    n_batch = g1.shape[0] - 1
    pm_in, pm_out = _post_mixer_specs(n_batch)
    return pl.pallas_call(
        _out_proj_kernel,
        grid=(n_rows // TM,),
        in_specs=[pl.BlockSpec((TM, D), lambda i: (i, 0)),
                  pl.BlockSpec((D, D), lambda i: (0, 0))] + pm_in,
        out_specs=pm_out,
        out_shape=_post_mixer_out_shape(n_rows),
        scratch_shapes=_post_mixer_scratch(),
        **_opts(1, flops=2 * n_rows * D * D, nbytes=12 * n_rows * D + 2 * D * D),
        name="out_proj",
    )(o, wo, x, g1, lng, lnb, sc2, sh2, wr, br)


def _na_key_row_start(j, rows):
    return jnp.clip(NA_QROWS * j - NA_KH // 2, 0, rows - NA_KROWS)


def _na_bias_table(rpb):
    rows = L // GRID_W
    n_blocks = rows // NA_QROWS
    j = jnp.array([0, 1, n_blocks - 1])
    qr = (NA_QROWS * j)[:, None] + jnp.arange(NA_QROWS)[None, :]
    kr = _na_key_row_start(j, rows)[:, None] + jnp.arange(NA_KROWS)[None, :]
    r0 = jnp.clip(qr - NA_KH // 2, 0, rows - NA_KH)
    row_ok = (kr[:, None, :] >= r0[:, :, None]) & (kr[:, None, :] < r0[:, :, None] + NA_KH)
    dr = jnp.clip(kr[:, None, :] - qr[:, :, None] + NA_KH - 1, 0, 2 * NA_KH - 2)
    cols = jnp.arange(GRID_W)
    col_start = jnp.clip(cols - NA_KW // 2, 0, GRID_W - NA_KW)
    col_ok = (cols[None, :] >= col_start[:, None]) & (cols[None, :] < col_start[:, None] + NA_KW)
    dc = jnp.clip(cols[None, :] - cols[:, None] + NA_KW - 1, 0, 2 * NA_KW - 2)
    by_row = jnp.where(col_ok, rpb[:, :, dc], NEG_INF)
    bias = jnp.where(row_ok[None, :, :, :, None, None], by_row[:, dr], NEG_INF)
    bias = jnp.transpose(bias, (0, 1, 2, 4, 3, 5))
    return bias.reshape(NA_HEADS, 3, NA_QROWS * GRID_W, NA_KROWS * GRID_W)


def _softmax_pv(parts, values):
    m = parts[0].max(axis=-1, keepdims=True)
    for s in parts[1:]:
        m = jnp.maximum(m, s.max(axis=-1, keepdims=True))
    den = None
    acc = None
    for s, v in zip(parts, values):
        p = jnp.exp(s - m)
        d = p.sum(axis=-1, keepdims=True)
        o = _dot(p.astype(BF16), v)
        den = d if den is None else den + d
        acc = o if acc is None else acc + o
    return acc / den


def _na_kernel(q_ref, k_ref, v_ref, bias_ref, o_ref):
    rows = L // GRID_W
    qb = NA_QROWS * GRID_W
    kb = NA_KROWS * GRID_W
    kc = k_ref[L:S, :]
    vc = v_ref[L:S, :]

    n_blocks = rows // NA_QROWS

    def block(j, carry):
        qs = pl.multiple_of(j * qb, qb)
        ks = pl.multiple_of(_na_key_row_start(j, rows) * GRID_W, GRID_W)
        kind = jnp.where(j == 0, 0, jnp.where(j == n_blocks - 1, 2, 1))
        q = q_ref[pl.ds(qs, qb), :]
        s_loc = _dot_t(q, k_ref[pl.ds(ks, kb), :]) * SCALE + bias_ref[0, kind]
        s_ctx = _dot_t(q, kc) * SCALE
        o = _softmax_pv([s_loc, s_ctx], [v_ref[pl.ds(ks, kb), :], vc])
        o_ref[pl.ds(qs, qb), :] = o.astype(o_ref.dtype)
        return carry

    lax.fori_loop(0, rows // NA_QROWS, block, 0, unroll=2)
    s = _dot_t(q_ref[L:S, :], kc) * SCALE
    o_ref[L:S, :] = _softmax_pv([s], [vc]).astype(o_ref.dtype)


def _na_attention(qkv, bias, n_batch):
    h = NA_HEADS
    n_blocks = bias.shape[1]
    blk = lambda off: pl.BlockSpec((S, HEAD_DIM), lambda hh, b: (b, off + hh))
    return pl.pallas_call(
        _na_kernel,
        grid=(h, n_batch),
        in_specs=[blk(0), blk(h), blk(2 * h),
                  pl.BlockSpec((1, n_blocks) + bias.shape[2:], lambda hh, b: (hh, 0, 0, 0))],
        out_specs=pl.BlockSpec((S, HEAD_DIM), lambda hh, b: (b, hh)),
        out_shape=jax.ShapeDtypeStruct((n_batch * S, h * HEAD_DIM), BF16),
        **_opts(2, flops=4 * n_batch * h * S * (NA_KROWS * GRID_W + C) * HEAD_DIM,
                nbytes=8 * n_batch * S * h * HEAD_DIM,
                transcendentals=n_batch * h * S * (NA_KROWS * GRID_W + C)),
        name="na_attn",
    )(qkv, qkv, qkv, bias)


def _swa_mask_table():
    span = SWA_BLOCK + 2 * SWA_WINDOW
    qi = jnp.arange(SWA_BLOCK)[:, None]
    kj = jnp.arange(span)[None, :]
    band = jnp.abs(qi - (kj - SWA_WINDOW)) <= SWA_WINDOW
    first = band & (kj >= SWA_WINDOW)
    last = band & (kj < SWA_BLOCK + SWA_WINDOW)
    m = jnp.stack([first, band, last]).astype(F32)
    m = jnp.where(m > 0, 0.0, NEG_INF).astype(F32)
    return jnp.tile(m, (1, SWA_REP, 1))


def _swa_kernel(sink_ref, q_ref, k_ref, v_ref, mask_ref, o_ref, kpad_ref, vpad_ref):
    g = pl.program_id(1)
    nb = L // SWA_BLOCK
    span = SWA_BLOCK + 2 * SWA_WINDOW
    pad = SWA_WINDOW
    zeros = jnp.zeros((pad, HEAD_DIM), BF16)
    for src, dst in ((k_ref, kpad_ref), (v_ref, vpad_ref)):
        dst[0:pad, :] = zeros
        dst[pad + L:, :] = zeros
        dst[pad:pad + L, :] = src[0:L, :]
    kc = k_ref[L:S, :]
    vc = v_ref[L:S, :]

    def sink_column(n_rows):
        return jnp.concatenate(
            [jnp.full((n_rows, 1), sink_ref[g * SWA_REP + r], F32) for r in range(SWA_REP)], axis=0)

    def attend(parts, values, sink_col):
        m = sink_col
        for s in parts:
            m = jnp.maximum(m, s.max(axis=-1, keepdims=True))
        den = jnp.exp(sink_col - m)
        acc = None
        for s, v in zip(parts, values):
            p = jnp.exp(s - m)
            den = den + p.sum(axis=-1, keepdims=True)
            o = _dot(p.astype(BF16), v)
            acc = o if acc is None else acc + o
        return acc / den

    def stacked_q(start, n_rows):
        return jnp.concatenate(
            [q_ref[pl.ds(start, n_rows), r * HEAD_DIM:(r + 1) * HEAD_DIM] for r in range(SWA_REP)], axis=0)

    def store(start, n_rows, o):
        for r in range(SWA_REP):
            o_ref[pl.ds(start, n_rows), r * HEAD_DIM:(r + 1) * HEAD_DIM] = (
                o[r * n_rows:(r + 1) * n_rows].astype(o_ref.dtype))

    sink_blk = sink_column(SWA_BLOCK)

    def block(n, carry):
        start = pl.multiple_of(n * SWA_BLOCK, SWA_BLOCK)
        q = stacked_q(start, SWA_BLOCK)
        kind = jnp.where(n == 0, 0, jnp.where(n == nb - 1, 2, 1))
        s_loc = _dot_t(q, kpad_ref[pl.ds(start, span), :]) * SCALE + mask_ref[kind]
        s_ctx = _dot_t(q, kc) * SCALE
        store(start, SWA_BLOCK, attend([s_loc, s_ctx], [vpad_ref[pl.ds(start, span), :], vc], sink_blk))
        return carry

    lax.fori_loop(0, nb, block, 0, unroll=2)
    qc = stacked_q(L, C)
    store(L, C, attend([_dot_t(qc, kc) * SCALE], [vc], sink_column(C)))


def _swa_attention(qk, v, sink, mask, n_batch):
    qw = SWA_REP * HEAD_DIM
    n_q_blocks = SWA_KV_HEADS
    return pl.pallas_call(
        _swa_kernel,
        grid_spec=pltpu.PrefetchScalarGridSpec(
            num_scalar_prefetch=1,
            grid=(n_batch, SWA_KV_HEADS),
            in_specs=[pl.BlockSpec((S, qw), lambda b, g, sk: (b, g)),
                      pl.BlockSpec((S, HEAD_DIM), lambda b, g, sk: (b, n_q_blocks * SWA_REP + g)),
                      pl.BlockSpec((S, HEAD_DIM), lambda b, g, sk: (b, g)),
                      pl.BlockSpec(mask.shape, lambda b, g, sk: (0, 0, 0))],
            out_specs=pl.BlockSpec((S, qw), lambda b, g, sk: (b, g)),
            scratch_shapes=[pltpu.VMEM((L + 2 * SWA_WINDOW, HEAD_DIM), BF16),
                            pltpu.VMEM((L + 2 * SWA_WINDOW, HEAD_DIM), BF16)]),
        out_shape=jax.ShapeDtypeStruct((n_batch * S, SWA_KV_HEADS * qw), BF16),
        **_opts(2, flops=4 * n_batch * SWA_KV_HEADS * SWA_REP * S * (SWA_BLOCK + 2 * SWA_WINDOW + C) * HEAD_DIM,
                nbytes=2 * n_batch * S * (2 * SWA_KV_HEADS * qw + 2 * SWA_KV_HEADS * HEAD_DIM),
                transcendentals=n_batch * SWA_KV_HEADS * SWA_REP * S * (SWA_BLOCK + 2 * SWA_WINDOW + C)),
        name="swa_attn",
    )(sink, qk, qk, v, mask)


def _diff_kernel(lam_ref, g_ref, q_ref, k_ref, v_ref, o_ref, *, lambda_init, tq):
    lam = lam_ref[...]
    lmbda = (jnp.exp(jnp.sum(lam[0:1] * lam[1:2], axis=-1, keepdims=True))
             - jnp.exp(jnp.sum(lam[2:3] * lam[3:4], axis=-1, keepdims=True)) + lambda_init)
    gain = g_ref[...] * (1.0 - lambda_init)

    def softmax_parts(s):
        m = s.max(axis=-1, keepdims=True)
        e = jnp.exp2((s - m) * (SCALE * LOG2E))
        return e, e.sum(axis=-1, keepdims=True)

    def attend(start, n_rows, k_lo, n_keys):
        q = q_ref[pl.ds(start, n_rows), :]
        e1, d1 = softmax_parts(_dot_t(q[:, :HEAD_DIM], k_ref[pl.ds(k_lo, n_keys), 0:HEAD_DIM]))
        e2, d2 = softmax_parts(_dot_t(q[:, HEAD_DIM:], k_ref[pl.ds(k_lo, n_keys), HEAD_DIM:2 * HEAD_DIM]))
        pd = e1 * (1.0 / d1) - e2 * (lmbda / d2)
        o = _dot(pd.astype(BF16), v_ref[pl.ds(k_lo, n_keys), :])
        o = o * lax.rsqrt(jnp.mean(o * o, axis=-1, keepdims=True) + LN_EPS) * gain
        o_ref[pl.ds(start, n_rows), :] = o.astype(o_ref.dtype)

    def block(i, carry):
        attend(pl.multiple_of(i * tq, tq), tq, 0, S)
        return carry

    lax.fori_loop(0, L // tq, block, 0, unroll=2)
    attend(L, C, L, C)


def _diff_attention(qk, v, lam, subln_g, lambda_init, n_batch):
    hw = 2 * HEAD_DIM
    return pl.pallas_call(
        functools.partial(_diff_kernel, lambda_init=lambda_init, tq=256),
        grid=(n_batch, DIFF_HEADS),
        in_specs=[pl.BlockSpec((4, HEAD_DIM), lambda b, h: (0, 0)),
                  pl.BlockSpec((1, hw), lambda b, h: (0, 0)),
                  pl.BlockSpec((S, hw), lambda b, h: (b, h)),
                  pl.BlockSpec((S, hw), lambda b, h: (b, DIFF_HEADS + h)),
                  pl.BlockSpec((S, hw), lambda b, h: (b, h))],
        out_specs=pl.BlockSpec((S, hw), lambda b, h: (b, h)),
        out_shape=jax.ShapeDtypeStruct((n_batch * S, DIFF_HEADS * hw), BF16),
        **_opts(2, flops=8 * n_batch * DIFF_HEADS * S * S * HEAD_DIM, nbytes=8 * n_batch * S * DIFF_HEADS * hw,
                transcendentals=2 * n_batch * DIFF_HEADS * S * S),
        name="diff_attn",
    )(lam, subln_g.reshape(1, hw), qk, qk, v)


CONV_ROWS = 128


def _conv_kernel(hc_ref, hp_ref, hn_ref, dw_ref, dwb_ref, cg_ref, cb_ref, wout_ref, bout_ref, *rest):
    win_ref, cv_ref = rest[-2], rest[-1]
    t = pl.program_id(0) % TILES_PER_BATCH
    first = (t == 0) | (t == TILES_PER_BATCH - 1)
    last = t >= TILES_PER_BATCH - 2
    win_ref[0:CONV_HALO, :] = jnp.where(first, 0.0, hp_ref[...])
    win_ref[CONV_HALO:CONV_HALO + TM, :] = hc_ref[...]
    win_ref[CONV_HALO + TM:, :] = jnp.where(last, 0.0, hn_ref[...])
    base = CONV_HALO - CONV_WIDTH // 2
    sub = 8

    def strip(c, carry):
        cs = pl.ds(pl.multiple_of(c * HEAD_DIM, HEAD_DIM), HEAD_DIM)
        for r in range(TM // CONV_ROWS):
            acc = jnp.zeros((CONV_ROWS, HEAD_DIM), F32)
            aligned = win_ref[pl.ds(r * CONV_ROWS, CONV_ROWS + 2 * CONV_HALO), cs]
            for res in range(sub):
                taps = [j for j in range(CONV_WIDTH) if (base + j) % sub == res]
                w = aligned if res == 0 else pltpu.roll(aligned, aligned.shape[0] - res, 0)
                for j in taps:
                    lo = (base + j) // sub * sub
                    acc = acc + dw_ref[j:j + 1, cs] * w[lo:lo + CONV_ROWS]
            cv_ref[r * CONV_ROWS:(r + 1) * CONV_ROWS, cs] = acc
        return carry

    lax.fori_loop(0, D // HEAD_DIM, strip, 0)

    def mixer_out(rows):
        hn = _layer_norm(cv_ref[rows, :] + dwb_ref[...], cg_ref[...], cb_ref[...])
        hn = hn * _sigmoid(hn)
        return _dot(hn.astype(BF16), wout_ref[...]) + bout_ref[...]

    _post_mixer(mixer_out, *rest[:-2])


def _conv_mixer(h, dw, dw_b, cg, cb, wout, bout, x, g1, lng, lnb, sc2, sh2, wr, br):
    n_rows = x.shape[0]
    n_batch = g1.shape[0] - 1
    n_tiles = n_rows // TM
    halo_per_tile = TM // CONV_HALO
    n_halo = n_rows // CONV_HALO
    pm_in, pm_out = _post_mixer_specs(n_batch)
    const = lambda i: (0, 0)
    vec = lambda a: a.reshape(1, D)
    dw_pad = jnp.concatenate([dw, jnp.zeros((1, D), dw.dtype)], axis=0)
    return pl.pallas_call(
        _conv_kernel,
        grid=(n_tiles,),
        in_specs=[pl.BlockSpec((TM, D), lambda i: (i, 0)),
                  pl.BlockSpec((CONV_HALO, D), lambda i: (jnp.maximum(i * halo_per_tile - 1, 0), 0)),
                  pl.BlockSpec((CONV_HALO, D), lambda i: (jnp.minimum((i + 1) * halo_per_tile, n_halo - 1), 0)),
                  pl.BlockSpec((CONV_WIDTH + 1, D), const),
                  pl.BlockSpec((1, D), const),
                  pl.BlockSpec((1, D), const),
                  pl.BlockSpec((1, D), const),
                  pl.BlockSpec((D, D), const),
                  pl.BlockSpec((1, D), const)] + pm_in,
        out_specs=pm_out,
        out_shape=_post_mixer_out_shape(n_rows),
        scratch_shapes=_post_mixer_scratch() + [pltpu.VMEM((TM + 2 * CONV_HALO, D), F32),
                                                pltpu.VMEM((TM, D), F32)],
        **_opts(1, flops=2 * n_rows * D * (D + CONV_WIDTH), nbytes=14 * n_rows * D + 2 * D * D),
        name="conv_mixer",
    )(h, h, h, dw_pad, vec(dw_b), vec(cg), vec(cb), wout, vec(bout), x, g1, lng, lnb, sc2, sh2, wr, br)


def _moe_kernel(be_ref, nu_ref, xs_ref, w13_ref, w2_ref, ys_ref, w13b_ref, w2b_ref):
    i = pl.program_id(0)

    @pl.when(i < nu_ref[0])
    def _():
        changed = (i == 0) | (be_ref[i] != be_ref[jnp.maximum(i - 1, 0)])

        @pl.when(changed)
        def _():
            w13b_ref[...] = w13_ref[0].astype(BF16)
            w2b_ref[...] = w2_ref[0].astype(BF16)

        for h in range(MOE_BLOCK // SUB_ROWS):
            rows = slice(h * SUB_ROWS, (h + 1) * SUB_ROWS)
            x_hi, x_lo = _unpack_bf16_halves(xs_ref[rows, :])
            a = _dot(x_hi, w13b_ref[0:D // 2, :]) + _dot(x_lo, w13b_ref[D // 2:, :])
            gate = a[:, :MOE_D_FF]
            hmid = (gate * _sigmoid(gate) * a[:, MOE_D_FF:]).astype(BF16)
            ys_ref[rows, :] = _dot(hmid, w2b_ref[...])


def _moe_experts(xs, blk_expert, n_used, w13, w2):
    n_slots = xs.shape[0]
    n_blocks = n_slots // MOE_BLOCK
    live = lambda i, nu: jnp.minimum(i, nu[0] - 1)
    return pl.pallas_call(
        _moe_kernel,
        grid_spec=pltpu.PrefetchScalarGridSpec(
            num_scalar_prefetch=2,
            grid=(n_blocks,),
            in_specs=[pl.BlockSpec((MOE_BLOCK, D // 2), lambda i, be, nu: (live(i, nu), 0)),
                      pl.BlockSpec((1, D, 2 * MOE_D_FF), lambda i, be, nu: (be[live(i, nu)], 0, 0)),
                      pl.BlockSpec((1, MOE_D_FF, D), lambda i, be, nu: (be[live(i, nu)], 0, 0))],
            out_specs=pl.BlockSpec((MOE_BLOCK, D), lambda i, be, nu: (live(i, nu), 0)),
            scratch_shapes=[pltpu.VMEM((D, 2 * MOE_D_FF), BF16),
                            pltpu.VMEM((MOE_D_FF, D), BF16)]),
        out_shape=jax.ShapeDtypeStruct((n_slots, D), F32),
        **_opts(1, flops=6 * n_slots * D * MOE_D_FF, nbytes=6 * n_slots * D + 12 * MOE_EXPERTS * D * MOE_D_FF),
        name="moe_experts",
    )(blk_expert, n_used, xs, w13, w2)


def _combine_kernel(x_ref, ya_ref, yb_ref, route_ref, g_ref, lng_ref, lnb_ref, o_ref):
    route = route_ref[...]
    f = (ya_ref[...] * route[:, ROUTE_GATE:ROUTE_GATE + 1]
         + yb_ref[...] * route[:, ROUTE_GATE + 1:ROUTE_GATE + 2])
    z = DEEPNORM_ALPHA * x_ref[...] + g_ref[0] * f
    o_ref[...] = _layer_norm(z, lng_ref[...], lnb_ref[...])


def _combine(x1, ya, yb, route, g2, lng, lnb):
    n_rows = x1.shape[0]
    n_batch = g2.shape[0] - 1
    row = pl.BlockSpec((TM, D), lambda i: (i, 0))
    const = pl.BlockSpec((1, D), lambda i: (0, 0))
    return pl.pallas_call(
        _combine_kernel,
        grid=(n_rows // TM,),
        in_specs=[row, row, row,
                  pl.BlockSpec((TM, ROUTER_PAD), lambda i: (i, 0)),
                  pl.BlockSpec((1, 1, D), lambda i: (_mod_group(i, n_batch), 0, 0)),
                  const, const],
        out_specs=row,
        out_shape=jax.ShapeDtypeStruct((n_rows, D), F32),
        **_opts(1, flops=10 * n_rows * D, nbytes=16 * n_rows * D),
        name="moe_combine",
    )(x1, ya, yb, route, g2, lng, lnb)


def _dispatch_plan(route, counts):
    n = route.shape[0]
    a = n * MOE_TOP_K
    experts = jnp.arange(MOE_EXPERTS, dtype=jnp.int32)
    counts = counts[0, MOE_GROUPS:MOE_GROUPS + MOE_EXPERTS].astype(jnp.int32)
    padded = (counts + MOE_BLOCK - 1) // MOE_BLOCK * MOE_BLOCK
    pad_end = jnp.cumsum(padded)
    pad_start = pad_end - padded
    expert = route[:, ROUTE_EXPERT:ROUTE_EXPERT + MOE_TOP_K].astype(jnp.int32)
    rank = route[:, ROUTE_RANK:ROUTE_RANK + MOE_TOP_K].astype(jnp.int32)
    slot = jnp.sum(jnp.where(expert[:, :, None] == experts, pad_start, 0), axis=-1) + rank
    n_blocks = -(-a // MOE_BLOCK) + MOE_EXPERTS
    tok_of = jnp.broadcast_to(jnp.arange(n, dtype=jnp.int32)[:, None], (n, MOE_TOP_K))
    tok = (jnp.arange(n_blocks * MOE_BLOCK, dtype=jnp.int32) % n).at[slot.reshape(a)].set(
        tok_of.reshape(a), unique_indices=True, mode='promise_in_bounds')
    blk_start = jnp.arange(n_blocks, dtype=jnp.int32) * MOE_BLOCK
    blk_expert = jnp.minimum(jnp.sum((pad_end[None, :] <= blk_start[:, None]).astype(jnp.int32), axis=-1),
                             MOE_EXPERTS - 1)
    n_used = (pad_end[-1] // MOE_BLOCK).reshape(1)
    return slot, tok, blk_expert, n_used


def _take_rows(a, idx):
    return a.at[idx].get(mode='promise_in_bounds')


def _moe_layer(x1, u2, route, counts, g2, lng, lnb, w13, w2):
    slot, tok, blk_expert, n_used = _dispatch_plan(route, counts)
    ys = _moe_experts(_take_rows(u2, tok), blk_expert, n_used, w13, w2)
    return _combine(x1, _take_rows(ys, slot[:, 0]), _take_rows(ys, slot[:, 1]), route, g2, lng, lnb)


def _rope_tables():
    t = jnp.arange(L, dtype=jnp.int32)
    pos = jnp.stack([t // GRID_W, t % GRID_W], -1).astype(F32)
    n_freq = HEAD_DIM // 4
    inv_freq = ROPE_THETA ** (-jnp.arange(n_freq, dtype=F32) / n_freq)
    ang = pos[:, :, None] * inv_freq
    cos, sin = jnp.cos(ang), jnp.sin(ang)
    cos_t = jnp.stack([cos, cos], axis=2).reshape(L, HEAD_DIM)
    sin_t = jnp.stack([-sin, sin], axis=2).reshape(L, HEAD_DIM)
    cos_t = jnp.concatenate([cos_t, jnp.ones((C, HEAD_DIM), F32)], axis=0)
    sin_t = jnp.concatenate([sin_t, jnp.zeros((C, HEAD_DIM), F32)], axis=0)
    return cos_t, sin_t


def _router_params(rg_w, rg_b, re_w, re_b):
    n = MOE_GROUPS + MOE_EXPERTS
    w = jnp.concatenate([rg_w, re_w, jnp.zeros((D, ROUTER_PAD - n), F32)], axis=1).astype(BF16)
    b = jnp.concatenate([rg_b, re_b, jnp.zeros((ROUTER_PAD - n,), F32)]).reshape(1, ROUTER_PAD)
    return w, b


def _modulation_inputs(c, c_ctx):
    n = c.shape[0] + 1
    pad = -n % 8
    return jnp.concatenate([c, c_ctx[None, :], jnp.zeros((pad, D), F32)], axis=0)


def _mixer_fn(idx, mixer):
    kind = idx % 4
    if kind == 0:
        w_qkv, rpb, w_o = mixer
        w_qkv, w_o, bias = w_qkv.astype(BF16), w_o.astype(BF16), _na_bias_table(rpb)

        def run(xs, sc1, sh1, n_batch, post):
            qkv = _proj(xs, sc1, sh1, w_qkv, n_out=3 * D, tn=D, out_dtype=BF16)
            return _out_proj(_na_attention(qkv, bias, n_batch), w_o, *post)
    elif kind == 1:
        w_in, b_in, dw, dw_b, cg, cb, w_out, b_out = mixer
        w_in, w_out = w_in.astype(BF16), w_out.astype(BF16)

        def run(xs, sc1, sh1, n_batch, post):
            h = _proj(xs, sc1, sh1, w_in, n_out=D, tn=D // 2, out_dtype=F32, bias=b_in, glu=True)
            return _conv_mixer(h, dw, dw_b, cg, cb, w_out, b_out, *post)
    elif kind == 2:
        w_qkv, sink, w_o = mixer
        n_qk = (SWA_KV_HEADS * SWA_REP + SWA_KV_HEADS) * HEAD_DIM
        n_v = SWA_KV_HEADS * HEAD_DIM
        w_qkv, w_o, rope, mask = w_qkv.astype(BF16), w_o.astype(BF16), _rope_tables(), _swa_mask_table()
        w_qk, w_v = w_qkv[:, :n_qk], w_qkv[:, n_qk:]

        def run(xs, sc1, sh1, n_batch, post):
            qk = _proj(xs, sc1, sh1, w_qk, n_out=n_qk, tn=n_qk, out_dtype=BF16, rope=rope)
            v = _proj(xs, sc1, sh1, w_v, n_out=n_v, tn=n_v, out_dtype=BF16)
            return _out_proj(_swa_attention(qk, v, sink, mask, n_batch), w_o, *post)
    else:
        w_qkv, lam, subln_g, w_o = mixer
        lambda_init = 0.8 - 0.6 * math.exp(-0.3 * idx)
        w_qkv, w_o, rope = w_qkv.astype(BF16), w_o.astype(BF16), _rope_tables()
        w_qk, w_v = w_qkv[:, :2 * D], w_qkv[:, 2 * D:]

        def run(xs, sc1, sh1, n_batch, post):
            qk = _proj(xs, sc1, sh1, w_qk, n_out=2 * D, tn=D, out_dtype=BF16, rope=rope)
            v = _proj(xs, sc1, sh1, w_v, n_out=D, tn=D, out_dtype=BF16)
            return _out_proj(_diff_attention(qk, v, lam, subln_g, lambda_init, n_batch), w_o, *post)
    return run


def _hybrid_layer(idx, streams, cvec, mod_w, mod_b, mixer, ln1_g, ln1_b, moe, ln2_g, ln2_b):
    vec = lambda a: a.reshape(1, D)
    m_all = _adaln(cvec, mod_w, mod_b)
    ctx_row = sum(nb for _, _, nb in streams)
    rg_w, rg_b, re_w, re_b, w13, w2 = moe
    wr, br = _router_params(rg_w, rg_b, re_w, re_b)
    run_mixer = _mixer_fn(idx, mixer)
    out = []
    for xs, b0, nb in streams:
        m = jnp.concatenate([m_all[b0:b0 + nb], m_all[ctx_row:ctx_row + 1]], axis=0)
        sh1, sc1, g1, sh2, sc2, g2 = [m[:, None, k * D:(k + 1) * D] for k in range(6)]
        post = (xs, g1, vec(ln1_g), vec(ln1_b), sc2, sh2, wr, br)
        routed = run_mixer(xs, sc1, sh1, nb, post)
        out.append((_moe_layer(*routed, g2, vec(ln2_g), vec(ln2_b), w13, w2), b0, nb))
    return out


def kernel(x, c, ctx, c_ctx, l0_mod_w, l0_mod_b, l0_na_w_qkv, l0_na_rpb, l0_na_w_o, l0_ln1_g, l0_ln1_b, l0_router_g_w, l0_router_g_b, l0_router_e_w, l0_router_e_b, l0_moe_w13, l0_moe_w2, l0_ln2_g, l0_ln2_b, l1_mod_w, l1_mod_b, l1_cv_w_in, l1_cv_b_in, l1_cv_dw, l1_cv_dw_b, l1_cv_ln_g, l1_cv_ln_b, l1_cv_w_out, l1_cv_b_out, l1_ln1_g, l1_ln1_b, l1_router_g_w, l1_router_g_b, l1_router_e_w, l1_router_e_b, l1_moe_w13, l1_moe_w2, l1_ln2_g, l1_ln2_b, l2_mod_w, l2_mod_b, l2_sw_w_qkv, l2_sw_sink, l2_sw_w_o, l2_ln1_g, l2_ln1_b, l2_router_g_w, l2_router_g_b, l2_router_e_w, l2_router_e_b, l2_moe_w13, l2_moe_w2, l2_ln2_g, l2_ln2_b, l3_mod_w, l3_mod_b, l3_df_w_qkv, l3_df_lambda, l3_df_subln_g, l3_df_w_o, l3_ln1_g, l3_ln1_b, l3_router_g_w, l3_router_g_b, l3_router_e_w, l3_router_e_b, l3_moe_w13, l3_moe_w2, l3_ln2_g, l3_ln2_b):
    layers = (
        (l0_mod_w, l0_mod_b, (l0_na_w_qkv, l0_na_rpb, l0_na_w_o), l0_ln1_g, l0_ln1_b,
         (l0_router_g_w, l0_router_g_b, l0_router_e_w, l0_router_e_b, l0_moe_w13, l0_moe_w2), l0_ln2_g, l0_ln2_b),
        (l1_mod_w, l1_mod_b, (l1_cv_w_in, l1_cv_b_in, l1_cv_dw, l1_cv_dw_b, l1_cv_ln_g, l1_cv_ln_b, l1_cv_w_out,
                              l1_cv_b_out), l1_ln1_g, l1_ln1_b,
         (l1_router_g_w, l1_router_g_b, l1_router_e_w, l1_router_e_b, l1_moe_w13, l1_moe_w2), l1_ln2_g, l1_ln2_b),
        (l2_mod_w, l2_mod_b, (l2_sw_w_qkv, l2_sw_sink, l2_sw_w_o), l2_ln1_g, l2_ln1_b,
         (l2_router_g_w, l2_router_g_b, l2_router_e_w, l2_router_e_b, l2_moe_w13, l2_moe_w2), l2_ln2_g, l2_ln2_b),
        (l3_mod_w, l3_mod_b, (l3_df_w_qkv, l3_df_lambda, l3_df_subln_g, l3_df_w_o), l3_ln1_g, l3_ln1_b,
         (l3_router_g_w, l3_router_g_b, l3_router_e_w, l3_router_e_b, l3_moe_w13, l3_moe_w2), l3_ln2_g, l3_ln2_b),
    )
    n_batch = x.shape[0]
    assert x.shape[1:] == (L, D) and ctx.shape[1:] == (C, D)
    n_streams = N_STREAMS if n_batch % N_STREAMS == 0 else 1
    nb = n_batch // n_streams
    streams = [(jnp.concatenate([x[b0:b0 + nb], ctx[b0:b0 + nb]], axis=1).reshape(nb * S, D), b0, nb)
               for b0 in range(0, n_batch, nb)]
    cvec = _modulation_inputs(c, c_ctx)
    for idx in range(DEPTH):
        streams = _hybrid_layer(idx, streams, cvec, *layers[idx])
    return jnp.concatenate([xs.reshape(nb, S, D)[:, :L] for xs, _, _ in streams], axis=0)
```

```python
import functools
import math

import jax
import jax.numpy as jnp
from jax import lax
from jax.experimental import pallas as pl
from jax.experimental.pallas import tpu as pltpu

D = 2048
L = 2048
C = 256
S = L + C
DEPTH = 4
GRID_W = 64
HEAD_DIM = 128
ROPE_THETA = 10000.0
LN_EPS = 1e-5
NEG_INF = -1e30
DEEPNORM_ALPHA = (2.0 * DEPTH) ** 0.25
NA_HEADS = 16
NA_KH = 8
NA_KW = 16
NA_QROWS = 4
NA_KROWS = 12
CONV_WIDTH = 31
CONV_HALO = 16
SWA_KV_HEADS = 4
SWA_REP = 4
SWA_WINDOW = 128
SWA_BLOCK = 128
DIFF_HEADS = 8
MOE_GROUPS = 4
MOE_EPG = 8
MOE_EXPERTS = 32
MOE_TOP_K = 2
MOE_D_FF = 512
MOE_BLOCK = 256
ROUTER_PAD = 128

TM = 256
SUB_ROWS = 256
TILES_PER_BATCH = S // TM
LAT_TILES = L // TM
N_STREAMS = 1
VMEM_LIMIT = 52 * 1024 * 1024
SCALE = HEAD_DIM ** -0.5
LOG2E = math.log2(math.e)

F32 = jnp.float32
BF16 = jnp.bfloat16


def _opts(n_axes, *, flops, nbytes, transcendentals=0):
    return dict(
        compiler_params=pltpu.CompilerParams(dimension_semantics=("arbitrary",) * n_axes,
                                             vmem_limit_bytes=VMEM_LIMIT),
        cost_estimate=pl.CostEstimate(flops=int(flops), transcendentals=int(transcendentals),
                                      bytes_accessed=int(nbytes)))


def _dot(a, b):
    return jnp.dot(a, b, preferred_element_type=F32)


def _dot_t(a, b):
    return lax.dot_general(a, b, (((1,), (1,)), ((), ())), preferred_element_type=F32)


def _sigmoid(x):
    return 1.0 / (1.0 + jnp.exp(-x))


def _layer_norm(z, g, b):
    mu = jnp.mean(z, axis=-1, keepdims=True)
    zc = z - mu
    var = jnp.mean(zc * zc, axis=-1, keepdims=True)
    return zc * lax.rsqrt(var + LN_EPS) * g + b


def _mod_group(i, n_batch):
    return jnp.where(i % TILES_PER_BATCH == TILES_PER_BATCH - 1, n_batch, i // TILES_PER_BATCH)


def _adaln_kernel(c_ref, w_ref, b_ref, o_ref):
    c = c_ref[...]
    s = c * _sigmoid(c)
    o_ref[...] = _dot(s.astype(BF16), w_ref[...].astype(BF16)) + b_ref[...]


def _adaln(cvec, w, b):
    r = cvec.shape[0]
    n = w.shape[1]
    tn = 1024
    return pl.pallas_call(
        _adaln_kernel,
        grid=(n // tn,),
        in_specs=[pl.BlockSpec((r, D), lambda j: (0, 0)),
                  pl.BlockSpec((D, tn), lambda j: (0, j)),
                  pl.BlockSpec((1, tn), lambda j: (0, j))],
        out_specs=pl.BlockSpec((r, tn), lambda j: (0, j)),
        out_shape=jax.ShapeDtypeStruct((r, n), F32),
        **_opts(1, flops=2 * r * D * n, nbytes=4 * D * n),
        name="adaln",
    )(cvec, w, b.reshape(1, n))


def _rope_rotate(y, cos, sin):
    lane = lax.broadcasted_iota(jnp.int32, y.shape, 1)
    partner = jnp.where(lane % 64 < 32, pltpu.roll(y, 96, 1), pltpu.roll(y, 32, 1))
    return y * cos + partner * sin


def _proj_kernel(*refs, has_bias, glu, rope, tn):
    it = iter(refs)
    x_ref, sc_ref, sh_ref, w_ref = next(it), next(it), next(it), next(it)
    wg_ref = next(it) if glu else None
    b_ref = next(it) if has_bias else None
    bg_ref = next(it) if glu else None
    cos_ref = next(it) if rope else None
    sin_ref = next(it) if rope else None
    o_ref = next(it)

    u = (x_ref[...] * (1.0 + sc_ref[0]) + sh_ref[0]).astype(BF16)
    y = _dot(u, w_ref[...])
    if has_bias:
        y = y + b_ref[...]
    if glu:
        y = y * _sigmoid(_dot(u, wg_ref[...]) + bg_ref[...])
    if rope:
        cos = cos_ref[...]
        sin = sin_ref[...]
        for h in range(tn // HEAD_DIM):
            sl = slice(h * HEAD_DIM, (h + 1) * HEAD_DIM)
            o_ref[:, sl] = _rope_rotate(y[:, sl], cos, sin).astype(o_ref.dtype)
    else:
        o_ref[...] = y.astype(o_ref.dtype)


def _proj(x, sc, sh, w, *, n_out, tn, out_dtype, bias=None, glu=False, rope=None):
    n_rows = x.shape[0]
    n_batch = sc.shape[0] - 1
    n_tiles = n_rows // TM
    n_col = n_out // tn
    grp = lambda j, i: (_mod_group(i, n_batch), 0, 0)
    in_specs = [pl.BlockSpec((TM, D), lambda j, i: (i, 0)),
                pl.BlockSpec((1, 1, D), grp),
                pl.BlockSpec((1, 1, D), grp),
                pl.BlockSpec((D, tn), lambda j, i: (0, j))]
    args = [x, sc, sh, w]
    if glu:
        in_specs.append(pl.BlockSpec((D, tn), lambda j, i: (0, j + n_col)))
        args.append(w)
    if bias is not None:
        b2 = bias.reshape(1, -1)
        in_specs.append(pl.BlockSpec((1, tn), lambda j, i: (0, j)))
        args.append(b2)
        if glu:
            in_specs.append(pl.BlockSpec((1, tn), lambda j, i: (0, j + n_col)))
            args.append(b2)
    if rope is not None:
        rope_spec = pl.BlockSpec((TM, HEAD_DIM), lambda j, i: (i % TILES_PER_BATCH, 0))
        in_specs += [rope_spec, rope_spec]
        args += [rope[0], rope[1]]
    return pl.pallas_call(
        functools.partial(_proj_kernel, has_bias=bias is not None, glu=glu, rope=rope is not None, tn=tn),
        grid=(n_col, n_tiles),
        in_specs=in_specs,
        out_specs=pl.BlockSpec((TM, tn), lambda j, i: (i, j)),
        out_shape=jax.ShapeDtypeStruct((n_rows, n_out), out_dtype),
        **_opts(2, flops=2 * n_rows * D * n_out * (2 if glu else 1),
                nbytes=4 * n_rows * D * n_col + 2 * D * n_out * (2 if glu else 1) + 4 * n_rows * n_out),
        name="proj",
    )(*args)


ROUTE_EXPERT, ROUTE_GATE, ROUTE_RANK = 0, 2, 4


def _first_lane_where(cond, lane):
    return jnp.min(jnp.where(cond, lane, ROUTER_PAD), axis=-1, keepdims=True)


def _route_tile(lg, cnt_ref):
    lane = lax.broadcasted_iota(jnp.int32, lg.shape, 1)
    gmask = lane < MOE_GROUPS
    gl = jnp.where(gmask, lg, NEG_INF)
    ge = jnp.exp(gl - gl.max(axis=-1, keepdims=True))
    gp = ge / ge.sum(axis=-1, keepdims=True)
    g_p = gp.max(axis=-1, keepdims=True)
    g_idx = _first_lane_where(gmask & (gp == g_p), lane)
    lo = MOE_GROUPS + MOE_EPG * g_idx
    emask = (lane >= lo) & (lane < lo + MOE_EPG)
    el = jnp.where(emask, lg, NEG_INF)
    ee = jnp.exp(el - el.max(axis=-1, keepdims=True))
    ep = jnp.where(emask, ee / ee.sum(axis=-1, keepdims=True), -1.0)
    p1 = ep.max(axis=-1, keepdims=True)
    i1 = _first_lane_where(ep == p1, lane)
    ep2 = jnp.where(lane == i1, -1.0, ep)
    p2 = ep2.max(axis=-1, keepdims=True)
    i2 = _first_lane_where(ep2 == p2, lane)
    den = p1 + p2
    gate1 = g_p * p1 / den
    gate2 = g_p * p2 / den
    oh1 = jnp.where(lane == i1, 1.0, 0.0)
    oh2 = jnp.where(lane == i2, 1.0, 0.0)
    n = lg.shape[0]
    tri = jnp.where(lax.broadcasted_iota(jnp.int32, (n, n), 0) > lax.broadcasted_iota(jnp.int32, (n, n), 1),
                    1.0, 0.0).astype(BF16)
    base = cnt_ref[...]
    tot1 = oh1.sum(axis=0, keepdims=True)
    pre1 = _dot(tri, oh1.astype(BF16)) + base
    pre2 = _dot(tri, oh2.astype(BF16)) + (base + tot1)
    rank1 = (oh1 * pre1).sum(axis=-1, keepdims=True)
    rank2 = (oh2 * pre2).sum(axis=-1, keepdims=True)
    cnt_ref[...] = base + tot1 + oh2.sum(axis=0, keepdims=True)
    cols = ((i1 - MOE_GROUPS).astype(F32), (i2 - MOE_GROUPS).astype(F32), gate1, gate2, rank1, rank2)
    route = jnp.zeros(lg.shape, F32)
    for k, col in enumerate(cols):
        route = jnp.where(lane == k, col, route)
    return route


def _pack_bf16_halves(u):
    bits = pltpu.bitcast(u.astype(F32), jnp.uint32)
    half = u.shape[1] // 2
    return bits[:, :half] | (bits[:, half:] >> 16)


def _unpack_bf16_halves(p):
    hi = pltpu.bitcast(p & jnp.uint32(0xFFFF0000), F32).astype(BF16)
    lo = pltpu.bitcast(p << 16, F32).astype(BF16)
    return hi, lo


def _post_mixer(mixer_out, x_ref, g_ref, lng_ref, lnb_ref, sc2_ref, sh2_ref, wr_ref, br_ref,
                x1_ref, u2_ref, route_ref, cnt_out_ref, cnt_ref):
    @pl.when(pl.program_id(0) == 0)
    def _():
        cnt_ref[...] = jnp.zeros(cnt_ref.shape, F32)

    for h in range(TM // SUB_ROWS):
        rows = slice(h * SUB_ROWS, (h + 1) * SUB_ROWS)
        z = DEEPNORM_ALPHA * x_ref[rows, :] + g_ref[0] * mixer_out(rows)
        x1 = _layer_norm(z, lng_ref[...], lnb_ref[...])
        x1_ref[rows, :] = x1
        u2 = (x1 * (1.0 + sc2_ref[0]) + sh2_ref[0]).astype(BF16)
        u2_ref[rows, :] = _pack_bf16_halves(u2)
        route_ref[rows, :] = _route_tile(_dot(u2, wr_ref[...]) + br_ref[...], cnt_ref)
    cnt_out_ref[...] = jnp.broadcast_to(cnt_ref[...], cnt_out_ref.shape)


def _post_mixer_specs(n_batch):
    grp = lambda i: (_mod_group(i, n_batch), 0, 0)
    row = lambda i: (i, 0)
    const = lambda i: (0, 0)
    in_specs = [pl.BlockSpec((TM, D), row),
                pl.BlockSpec((1, 1, D), grp),
                pl.BlockSpec((1, D), const),
                pl.BlockSpec((1, D), const),
                pl.BlockSpec((1, 1, D), grp),
                pl.BlockSpec((1, 1, D), grp),
                pl.BlockSpec((D, ROUTER_PAD), const),
                pl.BlockSpec((1, ROUTER_PAD), const)]
    out_specs = [pl.BlockSpec((TM, D), row),
                 pl.BlockSpec((TM, D // 2), row),
                 pl.BlockSpec((TM, ROUTER_PAD), row),
                 pl.BlockSpec((8, ROUTER_PAD), const)]
    return in_specs, out_specs


def _post_mixer_out_shape(n_rows):
    return [jax.ShapeDtypeStruct((n_rows, D), F32),
            jax.ShapeDtypeStruct((n_rows, D // 2), jnp.uint32),
            jax.ShapeDtypeStruct((n_rows, ROUTER_PAD), F32),
            jax.ShapeDtypeStruct((8, ROUTER_PAD), F32)]


def _post_mixer_scratch():
    return [pltpu.VMEM((1, ROUTER_PAD), F32)]


def _out_proj_kernel(o_ref, wo_ref, *rest):
    _post_mixer(lambda rows: _dot(o_ref[rows, :], wo_ref[...]), *rest)


def _out_proj(o, wo, x, g1, lng, lnb, sc2, sh2, wr, br):
    n_rows = x.shape[0]
    n_batch = g1.shape[0] - 1
    pm_in, pm_out = _post_mixer_specs(n_batch)
    return pl.pallas_call(
        _out_proj_kernel,
        grid=(n_rows // TM,),
        in_specs=[pl.BlockSpec((TM, D), lambda i: (i, 0)),
                  pl.BlockSpec((D, D), lambda i: (0, 0))] + pm_in,
        out_specs=pm_out,
        out_shape=_post_mixer_out_shape(n_rows),
        scratch_shapes=_post_mixer_scratch(),
        **_opts(1, flops=2 * n_rows * D * D, nbytes=12 * n_rows * D + 2 * D * D),
        name="out_proj",
    )(o, wo, x, g1, lng, lnb, sc2, sh2, wr, br)


def _na_key_row_start(j, rows):
    return jnp.clip(NA_QROWS * j - NA_KH // 2, 0, rows - NA_KROWS)


def _na_bias_table(rpb):
    rows = L // GRID_W
    n_blocks = rows // NA_QROWS
    j = jnp.array([0, 1, n_blocks - 1])
    qr = (NA_QROWS * j)[:, None] + jnp.arange(NA_QROWS)[None, :]
    kr = _na_key_row_start(j, rows)[:, None] + jnp.arange(NA_KROWS)[None, :]
    r0 = jnp.clip(qr - NA_KH // 2, 0, rows - NA_KH)
    row_ok = (kr[:, None, :] >= r0[:, :, None]) & (kr[:, None, :] < r0[:, :, None] + NA_KH)
    dr = jnp.clip(kr[:, None, :] - qr[:, :, None] + NA_KH - 1, 0, 2 * NA_KH - 2)
    cols = jnp.arange(GRID_W)
    col_start = jnp.clip(cols - NA_KW // 2, 0, GRID_W - NA_KW)
    col_ok = (cols[None, :] >= col_start[:, None]) & (cols[None, :] < col_start[:, None] + NA_KW)
    dc = jnp.clip(cols[None, :] - cols[:, None] + NA_KW - 1, 0, 2 * NA_KW - 2)
    by_row = jnp.where(col_ok, rpb[:, :, dc] * (1.0 / SCALE), NEG_INF)
    bias = jnp.where(row_ok[None, :, :, :, None, None], by_row[:, dr], NEG_INF)
    bias = jnp.transpose(bias, (0, 1, 2, 4, 3, 5))
    return bias.reshape(NA_HEADS, 3, NA_QROWS * GRID_W, NA_KROWS * GRID_W)


NA_GROUP = 2


def _na_kernel(q_ref, k_ref, v_ref, bias_ref, o_ref):
    rows = L // GRID_W
    qb = NA_QROWS * GRID_W
    kb = NA_KROWS * GRID_W
    n_blocks = rows // NA_QROWS
    kc = k_ref[L:S, :]
    vc = v_ref[L:S, :]

    def attend(blocks):
        scores = []
        for q_rows, k_rows, bias in blocks:
            q = q_ref[q_rows, :]
            s = _dot_t(q, kc)
            if k_rows is not None:
                s = jnp.concatenate([_dot_t(q, k_ref[k_rows, :]) + bias, s], axis=1)
            scores.append(s)
        probs = []
        for s in scores:
            p = jnp.exp2((s - s.max(axis=-1, keepdims=True)) * (SCALE * LOG2E))
            probs.append((p.astype(BF16), p.sum(axis=-1, keepdims=True)))
        for (q_rows, k_rows, _), (p, den) in zip(blocks, probs):
            if k_rows is None:
                o = _dot(p, vc)
            else:
                o = _dot(p[:, :kb], v_ref[k_rows, :]) + _dot(p[:, kb:], vc)
            o_ref[q_rows, :] = (o / den).astype(o_ref.dtype)

    def step(i, carry):
        blocks = []
        for u in range(NA_GROUP):
            j = i * NA_GROUP + u
            ks = pl.multiple_of(_na_key_row_start(j, rows) * GRID_W, GRID_W)
            kind = jnp.where(j == 0, 0, jnp.where(j == n_blocks - 1, 2, 1))
            blocks.append((pl.ds(pl.multiple_of(j * qb, qb), qb), pl.ds(ks, kb), bias_ref[0, kind]))
        attend(blocks)
        return carry

    lax.fori_loop(0, n_blocks // NA_GROUP, step, 0)
    attend([(slice(L, S), None, None)])


def _na_attention(qkv, bias, n_batch):
    h = NA_HEADS
    n_blocks = bias.shape[1]
    blk = lambda off: pl.BlockSpec((S, HEAD_DIM), lambda hh, b: (b, off + hh))
    return pl.pallas_call(
        _na_kernel,
        grid=(h, n_batch),
        in_specs=[blk(0), blk(h), blk(2 * h),
                  pl.BlockSpec((1, n_blocks) + bias.shape[2:], lambda hh, b: (hh, 0, 0, 0))],
        out_specs=pl.BlockSpec((S, HEAD_DIM), lambda hh, b: (b, hh)),
        out_shape=jax.ShapeDtypeStruct((n_batch * S, h * HEAD_DIM), BF16),
        **_opts(2, flops=4 * n_batch * h * S * (NA_KROWS * GRID_W + C) * HEAD_DIM,
                nbytes=8 * n_batch * S * h * HEAD_DIM,
                transcendentals=n_batch * h * S * (NA_KROWS * GRID_W + C)),
        name="na_attn",
    )(qkv, qkv, qkv, bias)


def _swa_mask_table():
    span = SWA_BLOCK + 2 * SWA_WINDOW
    qi = jnp.arange(SWA_BLOCK)[:, None]
    kj = jnp.arange(span)[None, :]
    band = jnp.abs(qi - (kj - SWA_WINDOW)) <= SWA_WINDOW
    first = band & (kj >= SWA_WINDOW)
    last = band & (kj < SWA_BLOCK + SWA_WINDOW)
    m = jnp.stack([first, band, last]).astype(F32)
    return jnp.where(m > 0, 0.0, NEG_INF).astype(F32)


def _swa_kernel(sink_ref, q_ref, k_ref, v_ref, mask_ref, o_ref, kpad_ref, vpad_ref):
    g = pl.program_id(1)
    nb = L // SWA_BLOCK
    span = SWA_BLOCK + 2 * SWA_WINDOW
    pad = SWA_WINDOW
    zeros = jnp.zeros((pad, HEAD_DIM), BF16)
    for src, dst in ((k_ref, kpad_ref), (v_ref, vpad_ref)):
        dst[0:pad, :] = zeros
        dst[pad + L:, :] = zeros
        dst[pad:pad + L, :] = src[0:L, :]
    kc = k_ref[L:S, :]
    vc = v_ref[L:S, :]

    def head(r):
        return slice(r * HEAD_DIM, (r + 1) * HEAD_DIM)

    def attend(rows, keys, values, mask):
        v = values[0] if len(values) == 1 else jnp.concatenate(values, axis=0)
        scores = []
        for r in range(SWA_REP):
            q = q_ref[rows, head(r)]
            s = [_dot_t(q, k) for k in keys]
            if mask is not None:
                s[0] = s[0] + mask
            scores.append(s[0] if len(s) == 1 else jnp.concatenate(s, axis=1))
        probs = []
        for r, s in enumerate(scores):
            sink_raw = sink_ref[g * SWA_REP + r] * (1.0 / SCALE)
            m = jnp.maximum(s.max(axis=-1, keepdims=True), sink_raw)
            p = jnp.exp2((s - m) * (SCALE * LOG2E))
            den = p.sum(axis=-1, keepdims=True) + jnp.exp2((sink_raw - m) * (SCALE * LOG2E))
            probs.append((p.astype(BF16), den))
        for r, (p, den) in enumerate(probs):
            o_ref[rows, head(r)] = (_dot(p, v) / den).astype(o_ref.dtype)

    def block(n, carry):
        start = pl.multiple_of(n * SWA_BLOCK, SWA_BLOCK)
        kind = jnp.where(n == 0, 0, jnp.where(n == nb - 1, 2, 1))
        attend(pl.ds(start, SWA_BLOCK), [kpad_ref[pl.ds(start, span), :], kc],
               [vpad_ref[pl.ds(start, span), :], vc], mask_ref[kind])
        return carry

    lax.fori_loop(0, nb, block, 0, unroll=2)
    attend(slice(L, S), [kc], [vc], None)


def _swa_attention(qk, v, sink, mask, n_batch):
    qw = SWA_REP * HEAD_DIM
    n_q_blocks = SWA_KV_HEADS
    return pl.pallas_call(
        _swa_kernel,
        grid_spec=pltpu.PrefetchScalarGridSpec(
            num_scalar_prefetch=1,
            grid=(n_batch, SWA_KV_HEADS),
            in_specs=[pl.BlockSpec((S, qw), lambda b, g, sk: (b, g)),
                      pl.BlockSpec((S, HEAD_DIM), lambda b, g, sk: (b, n_q_blocks * SWA_REP + g)),
                      pl.BlockSpec((S, HEAD_DIM), lambda b, g, sk: (b, g)),
                      pl.BlockSpec(mask.shape, lambda b, g, sk: (0, 0, 0))],
            out_specs=pl.BlockSpec((S, qw), lambda b, g, sk: (b, g)),
            scratch_shapes=[pltpu.VMEM((L + 2 * SWA_WINDOW, HEAD_DIM), BF16),
                            pltpu.VMEM((L + 2 * SWA_WINDOW, HEAD_DIM), BF16)]),
        out_shape=jax.ShapeDtypeStruct((n_batch * S, SWA_KV_HEADS * qw), BF16),
        **_opts(2, flops=4 * n_batch * SWA_KV_HEADS * SWA_REP * S * (SWA_BLOCK + 2 * SWA_WINDOW + C) * HEAD_DIM,
                nbytes=2 * n_batch * S * (2 * SWA_KV_HEADS * qw + 2 * SWA_KV_HEADS * HEAD_DIM),
                transcendentals=n_batch * SWA_KV_HEADS * SWA_REP * S * (SWA_BLOCK + 2 * SWA_WINDOW + C)),
        name="swa_attn",
    )(sink, qk, qk, v, mask)


DIFF_GROUP = 2


def _diff_kernel(lam_ref, g_ref, q_ref, k_ref, v_ref, o_ref, *, lambda_init, tq):
    lam = lam_ref[...]
    lmbda = (jnp.exp(jnp.sum(lam[0:1] * lam[1:2], axis=-1, keepdims=True))
             - jnp.exp(jnp.sum(lam[2:3] * lam[3:4], axis=-1, keepdims=True)) + lambda_init)
    gain = g_ref[...] * (1.0 - lambda_init)

    def softmax_parts(s):
        m = s.max(axis=-1, keepdims=True)
        e = jnp.exp2((s - m) * (SCALE * LOG2E))
        return e, e.sum(axis=-1, keepdims=True)

    def attend(q_blocks, keys):
        scores = []
        for q_rows in q_blocks:
            q = q_ref[q_rows, :]
            scores.append((_dot_t(q[:, :HEAD_DIM], k_ref[keys, 0:HEAD_DIM]),
                           _dot_t(q[:, HEAD_DIM:], k_ref[keys, HEAD_DIM:2 * HEAD_DIM])))
        probs = []
        for s1, s2 in scores:
            e1, d1 = softmax_parts(s1)
            e2, d2 = softmax_parts(s2)
            probs.append((e1 * (1.0 / d1) - e2 * (lmbda / d2)).astype(BF16))
        for q_rows, pd in zip(q_blocks, probs):
            o = _dot(pd, v_ref[keys, :])
            o = o * lax.rsqrt(jnp.mean(o * o, axis=-1, keepdims=True) + LN_EPS) * gain
            o_ref[q_rows, :] = o.astype(o_ref.dtype)

    def step(i, carry):
        attend([pl.ds(pl.multiple_of((i * DIFF_GROUP + u) * tq, tq), tq) for u in range(DIFF_GROUP)], slice(0, S))
        return carry

    lax.fori_loop(0, L // (tq * DIFF_GROUP), step, 0)
    attend([slice(L, S)], slice(L, S))


def _diff_attention(qk, v, lam, subln_g, lambda_init, n_batch):
    hw = 2 * HEAD_DIM
    return pl.pallas_call(
        functools.partial(_diff_kernel, lambda_init=lambda_init, tq=256),
        grid=(n_batch, DIFF_HEADS),
        in_specs=[pl.BlockSpec((4, HEAD_DIM), lambda b, h: (0, 0)),
                  pl.BlockSpec((1, hw), lambda b, h: (0, 0)),
                  pl.BlockSpec((S, hw), lambda b, h: (b, h)),
                  pl.BlockSpec((S, hw), lambda b, h: (b, DIFF_HEADS + h)),
                  pl.BlockSpec((S, hw), lambda b, h: (b, h))],
        out_specs=pl.BlockSpec((S, hw), lambda b, h: (b, h)),
        out_shape=jax.ShapeDtypeStruct((n_batch * S, DIFF_HEADS * hw), BF16),
        **_opts(2, flops=8 * n_batch * DIFF_HEADS * S * S * HEAD_DIM, nbytes=8 * n_batch * S * DIFF_HEADS * hw,
                transcendentals=2 * n_batch * DIFF_HEADS * S * S),
        name="diff_attn",
    )(lam, subln_g.reshape(1, hw), qk, qk, v)


CONV_ROWS = 128


def _conv_kernel(hc_ref, hp_ref, hn_ref, dw_ref, dwb_ref, cg_ref, cb_ref, wout_ref, bout_ref, *rest):
    win_ref, cv_ref = rest[-2], rest[-1]
    t = pl.program_id(0) % TILES_PER_BATCH
    first = (t == 0) | (t == TILES_PER_BATCH - 1)
    last = t >= TILES_PER_BATCH - 2
    win_ref[0:CONV_HALO, :] = jnp.where(first, 0.0, hp_ref[...])
    win_ref[CONV_HALO:CONV_HALO + TM, :] = hc_ref[...]
    win_ref[CONV_HALO + TM:, :] = jnp.where(last, 0.0, hn_ref[...])
    base = CONV_HALO - CONV_WIDTH // 2
    sub = 8

    def strip(c, carry):
        cs = pl.ds(pl.multiple_of(c * HEAD_DIM, HEAD_DIM), HEAD_DIM)
        for r in range(TM // CONV_ROWS):
            acc = jnp.zeros((CONV_ROWS, HEAD_DIM), F32)
            aligned = win_ref[pl.ds(r * CONV_ROWS, CONV_ROWS + 2 * CONV_HALO), cs]
            for res in range(sub):
                taps = [j for j in range(CONV_WIDTH) if (base + j) % sub == res]
                w = aligned if res == 0 else pltpu.roll(aligned, aligned.shape[0] - res, 0)
                for j in taps:
                    lo = (base + j) // sub * sub
                    acc = acc + dw_ref[j:j + 1, cs] * w[lo:lo + CONV_ROWS]
            cv_ref[r * CONV_ROWS:(r + 1) * CONV_ROWS, cs] = acc
        return carry

    lax.fori_loop(0, D // HEAD_DIM, strip, 0)

    def mixer_out(rows):
        hn = _layer_norm(cv_ref[rows, :] + dwb_ref[...], cg_ref[...], cb_ref[...])
        hn = hn * _sigmoid(hn)
        return _dot(hn.astype(BF16), wout_ref[...]) + bout_ref[...]

    _post_mixer(mixer_out, *rest[:-2])


def _conv_mixer(h, dw, dw_b, cg, cb, wout, bout, x, g1, lng, lnb, sc2, sh2, wr, br):
    n_rows = x.shape[0]
    n_batch = g1.shape[0] - 1
    n_tiles = n_rows // TM
    halo_per_tile = TM // CONV_HALO
    n_halo = n_rows // CONV_HALO
    pm_in, pm_out = _post_mixer_specs(n_batch)
    const = lambda i: (0, 0)
    vec = lambda a: a.reshape(1, D)
    dw_pad = jnp.concatenate([dw, jnp.zeros((1, D), dw.dtype)], axis=0)
    return pl.pallas_call(
        _conv_kernel,
        grid=(n_tiles,),
        in_specs=[pl.BlockSpec((TM, D), lambda i: (i, 0)),
                  pl.BlockSpec((CONV_HALO, D), lambda i: (jnp.maximum(i * halo_per_tile - 1, 0), 0)),
                  pl.BlockSpec((CONV_HALO, D), lambda i: (jnp.minimum((i + 1) * halo_per_tile, n_halo - 1), 0)),
                  pl.BlockSpec((CONV_WIDTH + 1, D), const),
                  pl.BlockSpec((1, D), const),
                  pl.BlockSpec((1, D), const),
                  pl.BlockSpec((1, D), const),
                  pl.BlockSpec((D, D), const),
                  pl.BlockSpec((1, D), const)] + pm_in,
        out_specs=pm_out,
        out_shape=_post_mixer_out_shape(n_rows),
        scratch_shapes=_post_mixer_scratch() + [pltpu.VMEM((TM + 2 * CONV_HALO, D), F32),
                                                pltpu.VMEM((TM, D), F32)],
        **_opts(1, flops=2 * n_rows * D * (D + CONV_WIDTH), nbytes=14 * n_rows * D + 2 * D * D),
        name="conv_mixer",
    )(h, h, h, dw_pad, vec(dw_b), vec(cg), vec(cb), wout, vec(bout), x, g1, lng, lnb, sc2, sh2, wr, br)


def _moe_kernel(be_ref, nu_ref, xs_ref, w13_ref, w2_ref, ys_ref, w13b_ref, w2b_ref):
    i = pl.program_id(0)

    @pl.when(i >= nu_ref[0])
    def _():
        ys_ref[...] = jnp.zeros(ys_ref.shape, ys_ref.dtype)

    @pl.when(i < nu_ref[0])
    def _():
        changed = (i == 0) | (be_ref[i] != be_ref[jnp.maximum(i - 1, 0)])

        @pl.when(changed)
        def _():
            w13b_ref[...] = w13_ref[0].astype(BF16)
            w2b_ref[...] = w2_ref[0].astype(BF16)

        for h in range(MOE_BLOCK // SUB_ROWS):
            rows = slice(h * SUB_ROWS, (h + 1) * SUB_ROWS)
            x_hi, x_lo = _unpack_bf16_halves(xs_ref[rows, :])
            a = _dot(x_hi, w13b_ref[0:D // 2, :]) + _dot(x_lo, w13b_ref[D // 2:, :])
            gate = a[:, :MOE_D_FF]
            hmid = (gate * _sigmoid(gate) * a[:, MOE_D_FF:]).astype(BF16)
            ys_ref[rows, :] = _dot(hmid, w2b_ref[...])


def _moe_experts(xs, blk_expert, n_used, w13, w2):
    n_slots = xs.shape[0]
    n_blocks = n_slots // MOE_BLOCK
    live = lambda i, nu: jnp.minimum(i, nu[0] - 1)
    return pl.pallas_call(
        _moe_kernel,
        grid_spec=pltpu.PrefetchScalarGridSpec(
            num_scalar_prefetch=2,
            grid=(n_blocks,),
            in_specs=[pl.BlockSpec((MOE_BLOCK, D // 2), lambda i, be, nu: (live(i, nu), 0)),
                      pl.BlockSpec((1, D, 2 * MOE_D_FF), lambda i, be, nu: (be[live(i, nu)], 0, 0)),
                      pl.BlockSpec((1, MOE_D_FF, D), lambda i, be, nu: (be[live(i, nu)], 0, 0))],
            out_specs=pl.BlockSpec((MOE_BLOCK, D), lambda i, be, nu: (i, 0)),
            scratch_shapes=[pltpu.VMEM((D, 2 * MOE_D_FF), BF16),
                            pltpu.VMEM((MOE_D_FF, D), BF16)]),
        out_shape=jax.ShapeDtypeStruct((n_slots, D), F32),
        **_opts(1, flops=6 * n_slots * D * MOE_D_FF, nbytes=6 * n_slots * D + 12 * MOE_EXPERTS * D * MOE_D_FF),
        name="moe_experts",
    )(blk_expert, n_used, xs, w13, w2)


def _combine_kernel(x_ref, ya_ref, yb_ref, route_ref, g_ref, lng_ref, lnb_ref, o_ref):
    route = route_ref[...]
    f = (ya_ref[...] * route[:, ROUTE_GATE:ROUTE_GATE + 1]
         + yb_ref[...] * route[:, ROUTE_GATE + 1:ROUTE_GATE + 2])
    z = DEEPNORM_ALPHA * x_ref[...] + g_ref[0] * f
    o_ref[...] = _layer_norm(z, lng_ref[...], lnb_ref[...])


def _combine(x1, ya, yb, route, g2, lng, lnb):
    n_rows = x1.shape[0]
    n_batch = g2.shape[0] - 1
    row = pl.BlockSpec((TM, D), lambda i: (i, 0))
    const = pl.BlockSpec((1, D), lambda i: (0, 0))
    return pl.pallas_call(
        _combine_kernel,
        grid=(n_rows // TM,),
        in_specs=[row, row, row,
                  pl.BlockSpec((TM, ROUTER_PAD), lambda i: (i, 0)),
                  pl.BlockSpec((1, 1, D), lambda i: (_mod_group(i, n_batch), 0, 0)),
                  const, const],
        out_specs=row,
        out_shape=jax.ShapeDtypeStruct((n_rows, D), F32),
        **_opts(1, flops=10 * n_rows * D, nbytes=16 * n_rows * D),
        name="moe_combine",
    )(x1, ya, yb, route, g2, lng, lnb)


def _dispatch_plan(route, counts):
    n = route.shape[0]
    a = n * MOE_TOP_K
    experts = jnp.arange(MOE_EXPERTS, dtype=jnp.int32)
    counts = counts[0, MOE_GROUPS:MOE_GROUPS + MOE_EXPERTS].astype(jnp.int32)
    padded = (counts + MOE_BLOCK - 1) // MOE_BLOCK * MOE_BLOCK
    pad_end = jnp.cumsum(padded)
    pad_start = pad_end - padded
    expert = route[:, ROUTE_EXPERT:ROUTE_EXPERT + MOE_TOP_K].astype(jnp.int32)
    rank = route[:, ROUTE_RANK:ROUTE_RANK + MOE_TOP_K].astype(jnp.int32)
    slot = jnp.sum(jnp.where(expert[:, :, None] == experts, pad_start, 0), axis=-1) + rank
    n_blocks = -(-a // MOE_BLOCK) + MOE_EXPERTS
    tok_of = jnp.broadcast_to(jnp.arange(n, dtype=jnp.int32)[:, None], (n, MOE_TOP_K))
    tok = (jnp.arange(n_blocks * MOE_BLOCK, dtype=jnp.int32) % n).at[slot.reshape(a)].set(
        tok_of.reshape(a), unique_indices=True, mode='promise_in_bounds')
    blk_start = jnp.arange(n_blocks, dtype=jnp.int32) * MOE_BLOCK
    blk_expert = jnp.minimum(jnp.sum((pad_end[None, :] <= blk_start[:, None]).astype(jnp.int32), axis=-1),
                             MOE_EXPERTS - 1)
    n_used = (pad_end[-1] // MOE_BLOCK).reshape(1)
    return slot, tok, blk_expert, n_used


def _take_rows(a, idx):
    return a.at[idx].get(mode='promise_in_bounds')


def _moe_layer(x1, u2, route, counts, g2, lng, lnb, w13, w2):
    slot, tok, blk_expert, n_used = _dispatch_plan(route, counts)
    ys = _moe_experts(_take_rows(u2, tok), blk_expert, n_used, w13, w2)
    return _combine(x1, _take_rows(ys, slot[:, 0]), _take_rows(ys, slot[:, 1]), route, g2, lng, lnb)


def _rope_tables():
    t = jnp.arange(L, dtype=jnp.int32)
    pos = jnp.stack([t // GRID_W, t % GRID_W], -1).astype(F32)
    n_freq = HEAD_DIM // 4
    inv_freq = ROPE_THETA ** (-jnp.arange(n_freq, dtype=F32) / n_freq)
    ang = pos[:, :, None] * inv_freq
    cos, sin = jnp.cos(ang), jnp.sin(ang)
    cos_t = jnp.stack([cos, cos], axis=2).reshape(L, HEAD_DIM)
    sin_t = jnp.stack([-sin, sin], axis=2).reshape(L, HEAD_DIM)
    cos_t = jnp.concatenate([cos_t, jnp.ones((C, HEAD_DIM), F32)], axis=0)
    sin_t = jnp.concatenate([sin_t, jnp.zeros((C, HEAD_DIM), F32)], axis=0)
    return cos_t, sin_t


def _router_params(rg_w, rg_b, re_w, re_b):
    n = MOE_GROUPS + MOE_EXPERTS
    w = jnp.concatenate([rg_w, re_w, jnp.zeros((D, ROUTER_PAD - n), F32)], axis=1).astype(BF16)
    b = jnp.concatenate([rg_b, re_b, jnp.zeros((ROUTER_PAD - n,), F32)]).reshape(1, ROUTER_PAD)
    return w, b


def _modulation_inputs(c, c_ctx):
    n = c.shape[0] + 1
    pad = -n % 8
    return jnp.concatenate([c, c_ctx[None, :], jnp.zeros((pad, D), F32)], axis=0)


def _mixer_fn(idx, mixer):
    kind = idx % 4
    if kind == 0:
        w_qkv, rpb, w_o = mixer
        w_qkv, w_o, bias = w_qkv.astype(BF16), w_o.astype(BF16), _na_bias_table(rpb)

        def run(xs, sc1, sh1, n_batch, post):
            qkv = _proj(xs, sc1, sh1, w_qkv, n_out=3 * D, tn=D, out_dtype=BF16)
            return _out_proj(_na_attention(qkv, bias, n_batch), w_o, *post)
    elif kind == 1:
        w_in, b_in, dw, dw_b, cg, cb, w_out, b_out = mixer
        w_in, w_out = w_in.astype(BF16), w_out.astype(BF16)

        def run(xs, sc1, sh1, n_batch, post):
            h = _proj(xs, sc1, sh1, w_in, n_out=D, tn=D // 2, out_dtype=F32, bias=b_in, glu=True)
            return _conv_mixer(h, dw, dw_b, cg, cb, w_out, b_out, *post)
    elif kind == 2:
        w_qkv, sink, w_o = mixer
        n_qk = (SWA_KV_HEADS * SWA_REP + SWA_KV_HEADS) * HEAD_DIM
        n_v = SWA_KV_HEADS * HEAD_DIM
        w_qkv, w_o, rope, mask = w_qkv.astype(BF16), w_o.astype(BF16), _rope_tables(), _swa_mask_table()
        w_qk, w_v = w_qkv[:, :n_qk], w_qkv[:, n_qk:]

        def run(xs, sc1, sh1, n_batch, post):
            qk = _proj(xs, sc1, sh1, w_qk, n_out=n_qk, tn=n_qk, out_dtype=BF16, rope=rope)
            v = _proj(xs, sc1, sh1, w_v, n_out=n_v, tn=n_v, out_dtype=BF16)
            return _out_proj(_swa_attention(qk, v, sink, mask, n_batch), w_o, *post)
    else:
        w_qkv, lam, subln_g, w_o = mixer
        lambda_init = 0.8 - 0.6 * math.exp(-0.3 * idx)
        w_qkv, w_o, rope = w_qkv.astype(BF16), w_o.astype(BF16), _rope_tables()
        w_qk, w_v = w_qkv[:, :2 * D], w_qkv[:, 2 * D:]

        def run(xs, sc1, sh1, n_batch, post):
            qk = _proj(xs, sc1, sh1, w_qk, n_out=2 * D, tn=D, out_dtype=BF16, rope=rope)
            v = _proj(xs, sc1, sh1, w_v, n_out=D, tn=D, out_dtype=BF16)
            return _out_proj(_diff_attention(qk, v, lam, subln_g, lambda_init, n_batch), w_o, *post)
    return run


def _hybrid_layer(idx, streams, cvec, mod_w, mod_b, mixer, ln1_g, ln1_b, moe, ln2_g, ln2_b):
    vec = lambda a: a.reshape(1, D)
    m_all = _adaln(cvec, mod_w, mod_b)
    ctx_row = sum(nb for _, _, nb in streams)
    rg_w, rg_b, re_w, re_b, w13, w2 = moe
    wr, br = _router_params(rg_w, rg_b, re_w, re_b)
    run_mixer = _mixer_fn(idx, mixer)
    out = []
    for xs, b0, nb in streams:
        m = jnp.concatenate([m_all[b0:b0 + nb], m_all[ctx_row:ctx_row + 1]], axis=0)
        sh1, sc1, g1, sh2, sc2, g2 = [m[:, None, k * D:(k + 1) * D] for k in range(6)]
        post = (xs, g1, vec(ln1_g), vec(ln1_b), sc2, sh2, wr, br)
        routed = run_mixer(xs, sc1, sh1, nb, post)
        out.append((_moe_layer(*routed, g2, vec(ln2_g), vec(ln2_b), w13, w2), b0, nb))
    return out


def kernel(x, c, ctx, c_ctx, l0_mod_w, l0_mod_b, l0_na_w_qkv, l0_na_rpb, l0_na_w_o, l0_ln1_g, l0_ln1_b, l0_router_g_w, l0_router_g_b, l0_router_e_w, l0_router_e_b, l0_moe_w13, l0_moe_w2, l0_ln2_g, l0_ln2_b, l1_mod_w, l1_mod_b, l1_cv_w_in, l1_cv_b_in, l1_cv_dw, l1_cv_dw_b, l1_cv_ln_g, l1_cv_ln_b, l1_cv_w_out, l1_cv_b_out, l1_ln1_g, l1_ln1_b, l1_router_g_w, l1_router_g_b, l1_router_e_w, l1_router_e_b, l1_moe_w13, l1_moe_w2, l1_ln2_g, l1_ln2_b, l2_mod_w, l2_mod_b, l2_sw_w_qkv, l2_sw_sink, l2_sw_w_o, l2_ln1_g, l2_ln1_b, l2_router_g_w, l2_router_g_b, l2_router_e_w, l2_router_e_b, l2_moe_w13, l2_moe_w2, l2_ln2_g, l2_ln2_b, l3_mod_w, l3_mod_b, l3_df_w_qkv, l3_df_lambda, l3_df_subln_g, l3_df_w_o, l3_ln1_g, l3_ln1_b, l3_router_g_w, l3_router_g_b, l3_router_e_w, l3_router_e_b, l3_moe_w13, l3_moe_w2, l3_ln2_g, l3_ln2_b):
    layers = (
        (l0_mod_w, l0_mod_b, (l0_na_w_qkv, l0_na_rpb, l0_na_w_o), l0_ln1_g, l0_ln1_b,
         (l0_router_g_w, l0_router_g_b, l0_router_e_w, l0_router_e_b, l0_moe_w13, l0_moe_w2), l0_ln2_g, l0_ln2_b),
        (l1_mod_w, l1_mod_b, (l1_cv_w_in, l1_cv_b_in, l1_cv_dw, l1_cv_dw_b, l1_cv_ln_g, l1_cv_ln_b, l1_cv_w_out,
                              l1_cv_b_out), l1_ln1_g, l1_ln1_b,
         (l1_router_g_w, l1_router_g_b, l1_router_e_w, l1_router_e_b, l1_moe_w13, l1_moe_w2), l1_ln2_g, l1_ln2_b),
        (l2_mod_w, l2_mod_b, (l2_sw_w_qkv, l2_sw_sink, l2_sw_w_o), l2_ln1_g, l2_ln1_b,
         (l2_router_g_w, l2_router_g_b, l2_router_e_w, l2_router_e_b, l2_moe_w13, l2_moe_w2), l2_ln2_g, l2_ln2_b),
        (l3_mod_w, l3_mod_b, (l3_df_w_qkv, l3_df_lambda, l3_df_subln_g, l3_df_w_o), l3_ln1_g, l3_ln1_b,
         (l3_router_g_w, l3_router_g_b, l3_router_e_w, l3_router_e_b, l3_moe_w13, l3_moe_w2), l3_ln2_g, l3_ln2_b),
    )
    n_batch = x.shape[0]
    assert x.shape[1:] == (L, D) and ctx.shape[1:] == (C, D)
    n_streams = N_STREAMS if n_batch % N_STREAMS == 0 else 1
    nb = n_batch // n_streams
    streams = [(jnp.concatenate([x[b0:b0 + nb], ctx[b0:b0 + nb]], axis=1).reshape(nb * S, D), b0, nb)
               for b0 in range(0, n_batch, nb)]
    cvec = _modulation_inputs(c, c_ctx)
    for idx in range(DEPTH):
        streams = _hybrid_layer(idx, streams, cvec, *layers[idx])
    return jnp.concatenate([xs.reshape(nb, S, D)[:, :L] for xs, _, _ in streams], axis=0)
```

```python
import functools
import math

import jax
import jax.numpy as jnp
from jax import lax
from jax.experimental import pallas as pl
from jax.experimental.pallas import tpu as pltpu

D = 2048
L = 2048
C = 256
S = L + C
DEPTH = 4
GRID_W = 64
HEAD_DIM = 128
ROPE_THETA = 10000.0
LN_EPS = 1e-5
NEG_INF = -1e30
DEEPNORM_ALPHA = (2.0 * DEPTH) ** 0.25
NA_HEADS = 16
NA_KH = 8
NA_KW = 16
NA_QROWS = 4
NA_KROWS = 12
CONV_WIDTH = 31
CONV_HALO = 16
SWA_KV_HEADS = 4
SWA_REP = 4
SWA_WINDOW = 128
SWA_BLOCK = 128
DIFF_HEADS = 8
MOE_GROUPS = 4
MOE_EPG = 8
MOE_EXPERTS = 32
MOE_TOP_K = 2
MOE_D_FF = 512
MOE_BLOCK = 256
ROUTER_PAD = 128

TM = 256
TILES_PER_BATCH = S // TM
LAT_TILES = L // TM
N_STREAMS = 1
VMEM_LIMIT = 52 * 1024 * 1024
SCALE = HEAD_DIM ** -0.5
LOG2E = math.log2(math.e)

F32 = jnp.float32
BF16 = jnp.bfloat16


def _opts(n_axes, *, flops, nbytes, transcendentals=0):
    return dict(
        compiler_params=pltpu.CompilerParams(dimension_semantics=("arbitrary",) * n_axes,
                                             vmem_limit_bytes=VMEM_LIMIT),
        cost_estimate=pl.CostEstimate(flops=int(flops), transcendentals=int(transcendentals),
                                      bytes_accessed=int(nbytes)))


def _dot(a, b):
    return jnp.dot(a, b, preferred_element_type=F32)


def _dot_t(a, b):
    return lax.dot_general(a, b, (((1,), (1,)), ((), ())), preferred_element_type=F32)


def _sigmoid(x):
    return 1.0 / (1.0 + jnp.exp(-x))


def _layer_norm(z, g, b):
    mu = jnp.mean(z, axis=-1, keepdims=True)
    zc = z - mu
    var = jnp.mean(zc * zc, axis=-1, keepdims=True)
    return zc * lax.rsqrt(var + LN_EPS) * g + b


def _mod_group(i, n_batch):
    return jnp.where(i % TILES_PER_BATCH == TILES_PER_BATCH - 1, n_batch, i // TILES_PER_BATCH)


def _adaln_kernel(c_ref, w_ref, b_ref, o_ref):
    c = c_ref[...]
    s = c * _sigmoid(c)
    o_ref[...] = _dot(s.astype(BF16), w_ref[...].astype(BF16)) + b_ref[...]


def _adaln(cvec, w, b):
    r = cvec.shape[0]
    n = w.shape[1]
    tn = 1024
    return pl.pallas_call(
        _adaln_kernel,
        grid=(n // tn,),
        in_specs=[pl.BlockSpec((r, D), lambda j: (0, 0)),
                  pl.BlockSpec((D, tn), lambda j: (0, j)),
                  pl.BlockSpec((1, tn), lambda j: (0, j))],
        out_specs=pl.BlockSpec((r, tn), lambda j: (0, j)),
        out_shape=jax.ShapeDtypeStruct((r, n), F32),
        **_opts(1, flops=2 * r * D * n, nbytes=4 * D * n),
        name="adaln",
    )(cvec, w, b.reshape(1, n))


def _rope_rotate(y, cos, sin):
    lane = lax.broadcasted_iota(jnp.int32, y.shape, 1)
    partner = jnp.where(lane % 64 < 32, pltpu.roll(y, 96, 1), pltpu.roll(y, 32, 1))
    return y * cos + partner * sin


def _proj_kernel(*refs, has_bias, glu, rope, tn):
    it = iter(refs)
    x_ref, sc_ref, sh_ref, w_ref = next(it), next(it), next(it), next(it)
    wg_ref = next(it) if glu else None
    b_ref = next(it) if has_bias else None
    bg_ref = next(it) if glu else None
    cos_ref = next(it) if rope else None
    sin_ref = next(it) if rope else None
    o_ref = next(it)

    u = (x_ref[...] * (1.0 + sc_ref[0]) + sh_ref[0]).astype(BF16)
    y = _dot(u, w_ref[...])
    if has_bias:
        y = y + b_ref[...]
    if glu:
        y = y * _sigmoid(_dot(u, wg_ref[...]) + bg_ref[...])
    if rope:
        cos = cos_ref[...]
        sin = sin_ref[...]
        for h in range(tn // HEAD_DIM):
            sl = slice(h * HEAD_DIM, (h + 1) * HEAD_DIM)
            o_ref[:, sl] = _rope_rotate(y[:, sl], cos, sin).astype(o_ref.dtype)
    else:
        o_ref[...] = y.astype(o_ref.dtype)


def _proj(x, sc, sh, w, *, n_out, tn, out_dtype, bias=None, glu=False, rope=None):
    n_rows = x.shape[0]
    n_batch = sc.shape[0] - 1
    n_tiles = n_rows // TM
    n_col = n_out // tn
    grp = lambda j, i: (_mod_group(i, n_batch), 0, 0)
    in_specs = [pl.BlockSpec((TM, D), lambda j, i: (i, 0)),
                pl.BlockSpec((1, 1, D), grp),
                pl.BlockSpec((1, 1, D), grp),
                pl.BlockSpec((D, tn), lambda j, i: (0, j))]
    args = [x, sc, sh, w]
    if glu:
        in_specs.append(pl.BlockSpec((D, tn), lambda j, i: (0, j + n_col)))
        args.append(w)
    if bias is not None:
        b2 = bias.reshape(1, -1)
        in_specs.append(pl.BlockSpec((1, tn), lambda j, i: (0, j)))
        args.append(b2)
        if glu:
            in_specs.append(pl.BlockSpec((1, tn), lambda j, i: (0, j + n_col)))
            args.append(b2)
    if rope is not None:
        rope_spec = pl.BlockSpec((TM, HEAD_DIM), lambda j, i: (i % TILES_PER_BATCH, 0))
        in_specs += [rope_spec, rope_spec]
        args += [rope[0], rope[1]]
    return pl.pallas_call(
        functools.partial(_proj_kernel, has_bias=bias is not None, glu=glu, rope=rope is not None, tn=tn),
        grid=(n_col, n_tiles),
        in_specs=in_specs,
        out_specs=pl.BlockSpec((TM, tn), lambda j, i: (i, j)),
        out_shape=jax.ShapeDtypeStruct((n_rows, n_out), out_dtype),
        **_opts(2, flops=2 * n_rows * D * n_out * (2 if glu else 1),
                nbytes=4 * n_rows * D * n_col + 2 * D * n_out * (2 if glu else 1) + 4 * n_rows * n_out),
        name="proj",
    )(*args)


ROUTE_EXPERT, ROUTE_GATE, ROUTE_RANK = 0, 2, 4


def _first_lane_where(cond, lane):
    return jnp.min(jnp.where(cond, lane, ROUTER_PAD), axis=-1, keepdims=True)


def _route_tile(lg, cnt_ref):
    lane = lax.broadcasted_iota(jnp.int32, lg.shape, 1)
    gmask = lane < MOE_GROUPS
    gl = jnp.where(gmask, lg, NEG_INF)
    ge = jnp.exp(gl - gl.max(axis=-1, keepdims=True))
    gp = ge / ge.sum(axis=-1, keepdims=True)
    g_p = gp.max(axis=-1, keepdims=True)
    g_idx = _first_lane_where(gmask & (gp == g_p), lane)
    lo = MOE_GROUPS + MOE_EPG * g_idx
    emask = (lane >= lo) & (lane < lo + MOE_EPG)
    el = jnp.where(emask, lg, NEG_INF)
    ee = jnp.exp(el - el.max(axis=-1, keepdims=True))
    ep = jnp.where(emask, ee / ee.sum(axis=-1, keepdims=True), -1.0)
    p1 = ep.max(axis=-1, keepdims=True)
    i1 = _first_lane_where(ep == p1, lane)
    ep2 = jnp.where(lane == i1, -1.0, ep)
    p2 = ep2.max(axis=-1, keepdims=True)
    i2 = _first_lane_where(ep2 == p2, lane)
    den = p1 + p2
    gate1 = g_p * p1 / den
    gate2 = g_p * p2 / den
    oh1 = jnp.where(lane == i1, 1.0, 0.0)
    oh2 = jnp.where(lane == i2, 1.0, 0.0)
    n = lg.shape[0]
    tri = jnp.where(lax.broadcasted_iota(jnp.int32, (n, n), 0) > lax.broadcasted_iota(jnp.int32, (n, n), 1),
                    1.0, 0.0).astype(BF16)
    base = cnt_ref[...]
    tot1 = oh1.sum(axis=0, keepdims=True)
    pre1 = _dot(tri, oh1.astype(BF16)) + base
    pre2 = _dot(tri, oh2.astype(BF16)) + (base + tot1)
    rank1 = (oh1 * pre1).sum(axis=-1, keepdims=True)
    rank2 = (oh2 * pre2).sum(axis=-1, keepdims=True)
    cnt_ref[...] = base + tot1 + oh2.sum(axis=0, keepdims=True)
    cols = ((i1 - MOE_GROUPS).astype(F32), (i2 - MOE_GROUPS).astype(F32), gate1, gate2, rank1, rank2)
    route = jnp.zeros(lg.shape, F32)
    for k, col in enumerate(cols):
        route = jnp.where(lane == k, col, route)
    return route


def _pack_bf16_halves(u):
    bits = pltpu.bitcast(u.astype(F32), jnp.uint32)
    half = u.shape[1] // 2
    return bits[:, :half] | (bits[:, half:] >> 16)


def _unpack_halves_f32(p):
    return pltpu.bitcast(p & jnp.uint32(0xFFFF0000), F32), pltpu.bitcast(p << 16, F32)


def _unpack_bf16_halves(p):
    hi, lo = _unpack_halves_f32(p)
    return hi.astype(BF16), lo.astype(BF16)


def _post_mixer(y, x_ref, g_ref, lng_ref, lnb_ref, sc2_ref, sh2_ref, wr_ref, br_ref,
                x1_ref, u2_ref, route_ref, cnt_out_ref, cnt_ref):
    @pl.when(pl.program_id(0) == 0)
    def _():
        cnt_ref[...] = jnp.zeros(cnt_ref.shape, F32)

    z = DEEPNORM_ALPHA * x_ref[...] + g_ref[0] * y
    x1 = _layer_norm(z, lng_ref[...], lnb_ref[...])
    x1_ref[...] = x1
    u2 = (x1 * (1.0 + sc2_ref[0]) + sh2_ref[0]).astype(BF16)
    u2_ref[...] = _pack_bf16_halves(u2)
    route_ref[...] = _route_tile(_dot(u2, wr_ref[...]) + br_ref[...], cnt_ref)
    cnt_out_ref[...] = jnp.broadcast_to(cnt_ref[...], cnt_out_ref.shape)


def _post_mixer_specs(n_batch):
    grp = lambda i: (_mod_group(i, n_batch), 0, 0)
    row = lambda i: (i, 0)
    const = lambda i: (0, 0)
    in_specs = [pl.BlockSpec((TM, D), row),
                pl.BlockSpec((1, 1, D), grp),
                pl.BlockSpec((1, D), const),
                pl.BlockSpec((1, D), const),
                pl.BlockSpec((1, 1, D), grp),
                pl.BlockSpec((1, 1, D), grp),
                pl.BlockSpec((D, ROUTER_PAD), const),
                pl.BlockSpec((1, ROUTER_PAD), const)]
    out_specs = [pl.BlockSpec((TM, D), row),
                 pl.BlockSpec((TM, D // 2), row),
                 pl.BlockSpec((TM, ROUTER_PAD), row),
                 pl.BlockSpec((8, ROUTER_PAD), const)]
    return in_specs, out_specs


def _post_mixer_out_shape(n_rows):
    return [jax.ShapeDtypeStruct((n_rows, D), F32),
            jax.ShapeDtypeStruct((n_rows, D // 2), jnp.uint32),
            jax.ShapeDtypeStruct((n_rows, ROUTER_PAD), F32),
            jax.ShapeDtypeStruct((8, ROUTER_PAD), F32)]


def _post_mixer_scratch():
    return [pltpu.VMEM((1, ROUTER_PAD), F32)]


def _out_proj_kernel(o_ref, wo_ref, *rest):
    _post_mixer(_dot(o_ref[...], wo_ref[...]), *rest)


def _out_proj(o, wo, x, g1, lng, lnb, sc2, sh2, wr, br):
    n_rows = x.shape[0]
    n_batch = g1.shape[0] - 1
    pm_in, pm_out = _post_mixer_specs(n_batch)
    return pl.pallas_call(
        _out_proj_kernel,
        grid=(n_rows // TM,),
        in_specs=[pl.BlockSpec((TM, D), lambda i: (i, 0)),
                  pl.BlockSpec((D, D), lambda i: (0, 0))] + pm_in,
        out_specs=pm_out,
        out_shape=_post_mixer_out_shape(n_rows),
        scratch_shapes=_post_mixer_scratch(),
        **_opts(1, flops=2 * n_rows * D * D, nbytes=12 * n_rows * D + 2 * D * D),
        name="out_proj",
    )(o, wo, x, g1, lng, lnb, sc2, sh2, wr, br)


def _na_key_row_start(j, rows):
    return jnp.clip(NA_QROWS * j - NA_KH // 2, 0, rows - NA_KROWS)


def _na_bias_table(rpb):
    rows = L // GRID_W
    n_blocks = rows // NA_QROWS
    j = jnp.array([0, 1, n_blocks - 1])
    qr = (NA_QROWS * j)[:, None] + jnp.arange(NA_QROWS)[None, :]
    kr = _na_key_row_start(j, rows)[:, None] + jnp.arange(NA_KROWS)[None, :]
    r0 = jnp.clip(qr - NA_KH // 2, 0, rows - NA_KH)
    row_ok = (kr[:, None, :] >= r0[:, :, None]) & (kr[:, None, :] < r0[:, :, None] + NA_KH)
    dr = jnp.clip(kr[:, None, :] - qr[:, :, None] + NA_KH - 1, 0, 2 * NA_KH - 2)
    cols = jnp.arange(GRID_W)
    col_start = jnp.clip(cols - NA_KW // 2, 0, GRID_W - NA_KW)
    col_ok = (cols[None, :] >= col_start[:, None]) & (cols[None, :] < col_start[:, None] + NA_KW)
    dc = jnp.clip(cols[None, :] - cols[:, None] + NA_KW - 1, 0, 2 * NA_KW - 2)
    by_row = jnp.where(col_ok, rpb[:, :, dc] * (1.0 / SCALE), NEG_INF)
    bias = jnp.where(row_ok[None, :, :, :, None, None], by_row[:, dr], NEG_INF)
    bias = jnp.transpose(bias, (0, 1, 2, 4, 3, 5))
    return bias.reshape(NA_HEADS, 3, NA_QROWS * GRID_W, NA_KROWS * GRID_W)


NA_GROUP = 2


def _na_kernel(q_ref, k_ref, v_ref, bias_ref, o_ref):
    rows = L // GRID_W
    qb = NA_QROWS * GRID_W
    kb = NA_KROWS * GRID_W
    n_blocks = rows // NA_QROWS
    kc = k_ref[L:S, :]
    vc = v_ref[L:S, :]

    def attend(blocks):
        scores = []
        for q_rows, k_rows, bias in blocks:
            q = q_ref[q_rows, :]
            s = _dot_t(q, kc)
            if k_rows is not None:
                s = jnp.concatenate([_dot_t(q, k_ref[k_rows, :]) + bias, s], axis=1)
            scores.append(s)
        probs = []
        for s in scores:
            p = jnp.exp2((s - s.max(axis=-1, keepdims=True)) * (SCALE * LOG2E))
            probs.append((p.astype(BF16), p.sum(axis=-1, keepdims=True)))
        for (q_rows, k_rows, _), (p, den) in zip(blocks, probs):
            if k_rows is None:
                o = _dot(p, vc)
            else:
                o = _dot(p[:, :kb], v_ref[k_rows, :]) + _dot(p[:, kb:], vc)
            o_ref[q_rows, :] = (o / den).astype(o_ref.dtype)

    def step(i, carry):
        blocks = []
        for u in range(NA_GROUP):
            j = i * NA_GROUP + u
            ks = pl.multiple_of(_na_key_row_start(j, rows) * GRID_W, GRID_W)
            kind = jnp.where(j == 0, 0, jnp.where(j == n_blocks - 1, 2, 1))
            blocks.append((pl.ds(pl.multiple_of(j * qb, qb), qb), pl.ds(ks, kb), bias_ref[0, kind]))
        attend(blocks)
        return carry

    lax.fori_loop(0, n_blocks // NA_GROUP, step, 0)
    attend([(slice(L, S), None, None)])


def _na_attention(qkv, bias, n_batch):
    h = NA_HEADS
    n_blocks = bias.shape[1]
    blk = lambda off: pl.BlockSpec((S, HEAD_DIM), lambda hh, b: (b, off + hh))
    return pl.pallas_call(
        _na_kernel,
        grid=(h, n_batch),
        in_specs=[blk(0), blk(h), blk(2 * h),
                  pl.BlockSpec((1, n_blocks) + bias.shape[2:], lambda hh, b: (hh, 0, 0, 0))],
        out_specs=pl.BlockSpec((S, HEAD_DIM), lambda hh, b: (b, hh)),
        out_shape=jax.ShapeDtypeStruct((n_batch * S, h * HEAD_DIM), BF16),
        **_opts(2, flops=4 * n_batch * h * S * (NA_KROWS * GRID_W + C) * HEAD_DIM,
                nbytes=8 * n_batch * S * h * HEAD_DIM,
                transcendentals=n_batch * h * S * (NA_KROWS * GRID_W + C)),
        name="na_attn",
    )(qkv, qkv, qkv, bias)


def _swa_mask_table():
    span = SWA_BLOCK + 2 * SWA_WINDOW
    qi = jnp.arange(SWA_BLOCK)[:, None]
    kj = jnp.arange(span)[None, :]
    band = jnp.abs(qi - (kj - SWA_WINDOW)) <= SWA_WINDOW
    first = band & (kj >= SWA_WINDOW)
    last = band & (kj < SWA_BLOCK + SWA_WINDOW)
    m = jnp.stack([first, band, last]).astype(F32)
    return jnp.where(m > 0, 0.0, NEG_INF).astype(F32)


def _swa_kernel(sink_ref, q_ref, k_ref, v_ref, mask_ref, o_ref, kpad_ref, vpad_ref):
    g = pl.program_id(1)
    nb = L // SWA_BLOCK
    span = SWA_BLOCK + 2 * SWA_WINDOW
    pad = SWA_WINDOW
    zeros = jnp.zeros((pad, HEAD_DIM), BF16)
    for src, dst in ((k_ref, kpad_ref), (v_ref, vpad_ref)):
        dst[0:pad, :] = zeros
        dst[pad + L:, :] = zeros
        dst[pad:pad + L, :] = src[0:L, :]
    kc = k_ref[L:S, :]
    vc = v_ref[L:S, :]

    def head(r):
        return slice(r * HEAD_DIM, (r + 1) * HEAD_DIM)

    def attend(rows, keys, values, mask):
        v = values[0] if len(values) == 1 else jnp.concatenate(values, axis=0)
        scores = []
        for r in range(SWA_REP):
            q = q_ref[rows, head(r)]
            s = [_dot_t(q, k) for k in keys]
            if mask is not None:
                s[0] = s[0] + mask
            scores.append(s[0] if len(s) == 1 else jnp.concatenate(s, axis=1))
        probs = []
        for r, s in enumerate(scores):
            sink_raw = sink_ref[g * SWA_REP + r] * (1.0 / SCALE)
            m = jnp.maximum(s.max(axis=-1, keepdims=True), sink_raw)
            p = jnp.exp2((s - m) * (SCALE * LOG2E))
            den = p.sum(axis=-1, keepdims=True) + jnp.exp2((sink_raw - m) * (SCALE * LOG2E))
            probs.append((p.astype(BF16), den))
        for r, (p, den) in enumerate(probs):
            o_ref[rows, head(r)] = (_dot(p, v) / den).astype(o_ref.dtype)

    def block(n, carry):
        start = pl.multiple_of(n * SWA_BLOCK, SWA_BLOCK)
        kind = jnp.where(n == 0, 0, jnp.where(n == nb - 1, 2, 1))
        attend(pl.ds(start, SWA_BLOCK), [kpad_ref[pl.ds(start, span), :], kc],
               [vpad_ref[pl.ds(start, span), :], vc], mask_ref[kind])
        return carry

    lax.fori_loop(0, nb, block, 0, unroll=2)
    attend(slice(L, S), [kc], [vc], None)


def _swa_attention(qk, v, sink, mask, n_batch):
    qw = SWA_REP * HEAD_DIM
    n_q_blocks = SWA_KV_HEADS
    return pl.pallas_call(
        _swa_kernel,
        grid_spec=pltpu.PrefetchScalarGridSpec(
            num_scalar_prefetch=1,
            grid=(n_batch, SWA_KV_HEADS),
            in_specs=[pl.BlockSpec((S, qw), lambda b, g, sk: (b, g)),
                      pl.BlockSpec((S, HEAD_DIM), lambda b, g, sk: (b, n_q_blocks * SWA_REP + g)),
                      pl.BlockSpec((S, HEAD_DIM), lambda b, g, sk: (b, g)),
                      pl.BlockSpec(mask.shape, lambda b, g, sk: (0, 0, 0))],
            out_specs=pl.BlockSpec((S, qw), lambda b, g, sk: (b, g)),
            scratch_shapes=[pltpu.VMEM((L + 2 * SWA_WINDOW, HEAD_DIM), BF16),
                            pltpu.VMEM((L + 2 * SWA_WINDOW, HEAD_DIM), BF16)]),
        out_shape=jax.ShapeDtypeStruct((n_batch * S, SWA_KV_HEADS * qw), BF16),
        **_opts(2, flops=4 * n_batch * SWA_KV_HEADS * SWA_REP * S * (SWA_BLOCK + 2 * SWA_WINDOW + C) * HEAD_DIM,
                nbytes=2 * n_batch * S * (2 * SWA_KV_HEADS * qw + 2 * SWA_KV_HEADS * HEAD_DIM),
                transcendentals=n_batch * SWA_KV_HEADS * SWA_REP * S * (SWA_BLOCK + 2 * SWA_WINDOW + C)),
        name="swa_attn",
    )(sink, qk, qk, v, mask)


DIFF_GROUP = 2


def _diff_kernel(lam_ref, g_ref, q_ref, k_ref, v_ref, o_ref, *, lambda_init, tq):
    lam = lam_ref[...]
    lmbda = (jnp.exp(jnp.sum(lam[0:1] * lam[1:2], axis=-1, keepdims=True))
             - jnp.exp(jnp.sum(lam[2:3] * lam[3:4], axis=-1, keepdims=True)) + lambda_init)
    gain = g_ref[...] * (1.0 - lambda_init)

    def softmax_parts(s):
        m = s.max(axis=-1, keepdims=True)
        e = jnp.exp2((s - m) * (SCALE * LOG2E))
        return e, e.sum(axis=-1, keepdims=True)

    def attend(q_blocks, keys):
        scores = []
        for q_rows in q_blocks:
            q = q_ref[q_rows, :]
            scores.append((_dot_t(q[:, :HEAD_DIM], k_ref[keys, 0:HEAD_DIM]),
                           _dot_t(q[:, HEAD_DIM:], k_ref[keys, HEAD_DIM:2 * HEAD_DIM])))
        probs = []
        for s1, s2 in scores:
            e1, d1 = softmax_parts(s1)
            e2, d2 = softmax_parts(s2)
            probs.append((e1 * (1.0 / d1) - e2 * (lmbda / d2)).astype(BF16))
        for q_rows, pd in zip(q_blocks, probs):
            o = _dot(pd, v_ref[keys, :])
            o = o * lax.rsqrt(jnp.mean(o * o, axis=-1, keepdims=True) + LN_EPS) * gain
            o_ref[q_rows, :] = o.astype(o_ref.dtype)

    def step(i, carry):
        attend([pl.ds(pl.multiple_of((i * DIFF_GROUP + u) * tq, tq), tq) for u in range(DIFF_GROUP)], slice(0, S))
        return carry

    lax.fori_loop(0, L // (tq * DIFF_GROUP), step, 0)
    attend([slice(L, S)], slice(L, S))


def _diff_attention(qk, v, lam, subln_g, lambda_init, n_batch):
    hw = 2 * HEAD_DIM
    return pl.pallas_call(
        functools.partial(_diff_kernel, lambda_init=lambda_init, tq=256),
        grid=(n_batch, DIFF_HEADS),
        in_specs=[pl.BlockSpec((4, HEAD_DIM), lambda b, h: (0, 0)),
                  pl.BlockSpec((1, hw), lambda b, h: (0, 0)),
                  pl.BlockSpec((S, hw), lambda b, h: (b, h)),
                  pl.BlockSpec((S, hw), lambda b, h: (b, DIFF_HEADS + h)),
                  pl.BlockSpec((S, hw), lambda b, h: (b, h))],
        out_specs=pl.BlockSpec((S, hw), lambda b, h: (b, h)),
        out_shape=jax.ShapeDtypeStruct((n_batch * S, DIFF_HEADS * hw), BF16),
        **_opts(2, flops=8 * n_batch * DIFF_HEADS * S * S * HEAD_DIM, nbytes=8 * n_batch * S * DIFF_HEADS * hw,
                transcendentals=2 * n_batch * DIFF_HEADS * S * S),
        name="diff_attn",
    )(lam, subln_g.reshape(1, hw), qk, qk, v)


CONV_ROWS = 128


def _conv_kernel(hc_ref, hp_ref, hn_ref, dw_ref, dwb_ref, cg_ref, cb_ref, wout_ref, bout_ref, *rest):
    win_ref, cv_ref = rest[-2:]
    t = pl.program_id(0) % TILES_PER_BATCH
    first = (t == 0) | (t == TILES_PER_BATCH - 1)
    last = t >= TILES_PER_BATCH - 2
    win_ref[0:CONV_HALO, :] = jnp.where(first, 0.0, hp_ref[...])
    win_ref[CONV_HALO:CONV_HALO + TM, :] = hc_ref[...]
    win_ref[CONV_HALO + TM:, :] = jnp.where(last, 0.0, hn_ref[...])
    base = CONV_HALO - CONV_WIDTH // 2
    sub = 8

    def strip(c, carry):
        cs = pl.ds(pl.multiple_of(c * HEAD_DIM, HEAD_DIM), HEAD_DIM)
        for r in range(TM // CONV_ROWS):
            acc = jnp.zeros((CONV_ROWS, HEAD_DIM), F32)
            aligned = win_ref[pl.ds(r * CONV_ROWS, CONV_ROWS + 2 * CONV_HALO), cs]
            for res in range(sub):
                taps = [j for j in range(CONV_WIDTH) if (base + j) % sub == res]
                w = aligned if res == 0 else pltpu.roll(aligned, aligned.shape[0] - res, 0)
                for j in taps:
                    lo = (base + j) // sub * sub
                    acc = acc + dw_ref[j:j + 1, cs] * w[lo:lo + CONV_ROWS]
            cv_ref[r * CONV_ROWS:(r + 1) * CONV_ROWS, cs] = acc
        return carry

    lax.fori_loop(0, D // HEAD_DIM, strip, 0)

    hn = _layer_norm(cv_ref[...] + dwb_ref[...], cg_ref[...], cb_ref[...])
    hn = (hn * _sigmoid(hn)).astype(BF16)
    _post_mixer(_dot(hn, wout_ref[...]) + bout_ref[...], *rest[:-2])


def _conv_mixer(h, dw, dw_b, cg, cb, wout, bout, x, g1, lng, lnb, sc2, sh2, wr, br):
    n_rows = x.shape[0]
    n_batch = g1.shape[0] - 1
    n_tiles = n_rows // TM
    halo_per_tile = TM // CONV_HALO
    n_halo = n_rows // CONV_HALO
    pm_in, pm_out = _post_mixer_specs(n_batch)
    const = lambda i: (0, 0)
    vec = lambda a: a.reshape(1, D)
    dw_pad = jnp.concatenate([dw, jnp.zeros((1, D), dw.dtype)], axis=0)
    return pl.pallas_call(
        _conv_kernel,
        grid=(n_tiles,),
        in_specs=[pl.BlockSpec((TM, D), lambda i: (i, 0)),
                  pl.BlockSpec((CONV_HALO, D), lambda i: (jnp.maximum(i * halo_per_tile - 1, 0), 0)),
                  pl.BlockSpec((CONV_HALO, D), lambda i: (jnp.minimum((i + 1) * halo_per_tile, n_halo - 1), 0)),
                  pl.BlockSpec((CONV_WIDTH + 1, D), const),
                  pl.BlockSpec((1, D), const),
                  pl.BlockSpec((1, D), const),
                  pl.BlockSpec((1, D), const),
                  pl.BlockSpec((D, D), const),
                  pl.BlockSpec((1, D), const)] + pm_in,
        out_specs=pm_out,
        out_shape=_post_mixer_out_shape(n_rows),
        scratch_shapes=_post_mixer_scratch() + [pltpu.VMEM((TM + 2 * CONV_HALO, D), F32),
                                                pltpu.VMEM((TM, D), F32)],
        **_opts(1, flops=2 * n_rows * D * (D + CONV_WIDTH), nbytes=14 * n_rows * D + 2 * D * D),
        name="conv_mixer",
    )(h, h, h, dw_pad, vec(dw_b), vec(cg), vec(cb), wout, vec(bout), x, g1, lng, lnb, sc2, sh2, wr, br)


def _moe_kernel(be_ref, nu_ref, xs_ref, w13_ref, w2_ref, ys_ref, w13b_ref, w2b_ref):
    i = pl.program_id(0)

    @pl.when(i >= nu_ref[0])
    def _():
        ys_ref[...] = jnp.zeros(ys_ref.shape, ys_ref.dtype)

    @pl.when(i < nu_ref[0])
    def _():
        changed = (i == 0) | (be_ref[i] != be_ref[jnp.maximum(i - 1, 0)])

        @pl.when(changed)
        def _():
            w13b_ref[...] = w13_ref[0].astype(BF16)
            w2b_ref[...] = w2_ref[0].astype(BF16)

        x_hi, x_lo = _unpack_bf16_halves(xs_ref[...])
        a = _dot(x_hi, w13b_ref[0:D // 2, :]) + _dot(x_lo, w13b_ref[D // 2:, :])
        gate = a[:, :MOE_D_FF]
        hmid = (gate * _sigmoid(gate) * a[:, MOE_D_FF:]).astype(BF16)
        ys_ref[...] = _pack_bf16_halves(_dot(hmid, w2b_ref[...]).astype(BF16))


def _moe_experts(xs, blk_expert, n_used, w13, w2):
    n_slots = xs.shape[0]
    n_blocks = n_slots // MOE_BLOCK
    live = lambda i, nu: jnp.minimum(i, nu[0] - 1)
    return pl.pallas_call(
        _moe_kernel,
        grid_spec=pltpu.PrefetchScalarGridSpec(
            num_scalar_prefetch=2,
            grid=(n_blocks,),
            in_specs=[pl.BlockSpec((MOE_BLOCK, D // 2), lambda i, be, nu: (live(i, nu), 0)),
                      pl.BlockSpec((1, D, 2 * MOE_D_FF), lambda i, be, nu: (be[live(i, nu)], 0, 0)),
                      pl.BlockSpec((1, MOE_D_FF, D), lambda i, be, nu: (be[live(i, nu)], 0, 0))],
            out_specs=pl.BlockSpec((MOE_BLOCK, D // 2), lambda i, be, nu: (i, 0)),
            scratch_shapes=[pltpu.VMEM((D, 2 * MOE_D_FF), BF16),
                            pltpu.VMEM((MOE_D_FF, D), BF16)]),
        out_shape=jax.ShapeDtypeStruct((n_slots, D // 2), jnp.uint32),
        **_opts(1, flops=6 * n_slots * D * MOE_D_FF, nbytes=4 * n_slots * D + 12 * MOE_EXPERTS * D * MOE_D_FF),
        name="moe_experts",
    )(blk_expert, n_used, xs, w13, w2)


def _combine_kernel(x_ref, ya_ref, yb_ref, route_ref, g_ref, lng_ref, lnb_ref, o_ref):
    route = route_ref[...]
    gate_a = route[:, ROUTE_GATE:ROUTE_GATE + 1]
    gate_b = route[:, ROUTE_GATE + 1:ROUTE_GATE + 2]
    a_hi, a_lo = _unpack_halves_f32(ya_ref[...])
    b_hi, b_lo = _unpack_halves_f32(yb_ref[...])
    f = jnp.concatenate([a_hi * gate_a + b_hi * gate_b, a_lo * gate_a + b_lo * gate_b], axis=1)
    z = DEEPNORM_ALPHA * x_ref[...] + g_ref[0] * f
    o_ref[...] = _layer_norm(z, lng_ref[...], lnb_ref[...])


def _combine(x1, ya, yb, route, g2, lng, lnb):
    n_rows = x1.shape[0]
    n_batch = g2.shape[0] - 1
    row = pl.BlockSpec((TM, D), lambda i: (i, 0))
    packed = pl.BlockSpec((TM, D // 2), lambda i: (i, 0))
    const = pl.BlockSpec((1, D), lambda i: (0, 0))
    return pl.pallas_call(
        _combine_kernel,
        grid=(n_rows // TM,),
        in_specs=[row, packed, packed,
                  pl.BlockSpec((TM, ROUTER_PAD), lambda i: (i, 0)),
                  pl.BlockSpec((1, 1, D), lambda i: (_mod_group(i, n_batch), 0, 0)),
                  const, const],
        out_specs=row,
        out_shape=jax.ShapeDtypeStruct((n_rows, D), F32),
        **_opts(1, flops=10 * n_rows * D, nbytes=16 * n_rows * D),
        name="moe_combine",
    )(x1, ya, yb, route, g2, lng, lnb)


def _dispatch_plan(route, counts):
    n = route.shape[0]
    a = n * MOE_TOP_K
    experts = jnp.arange(MOE_EXPERTS, dtype=jnp.int32)
    counts = counts[0, MOE_GROUPS:MOE_GROUPS + MOE_EXPERTS].astype(jnp.int32)
    padded = (counts + MOE_BLOCK - 1) // MOE_BLOCK * MOE_BLOCK
    pad_end = jnp.cumsum(padded)
    pad_start = pad_end - padded
    expert = route[:, ROUTE_EXPERT:ROUTE_EXPERT + MOE_TOP_K].astype(jnp.int32)
    rank = route[:, ROUTE_RANK:ROUTE_RANK + MOE_TOP_K].astype(jnp.int32)
    slot = jnp.sum(jnp.where(expert[:, :, None] == experts, pad_start, 0), axis=-1) + rank
    n_blocks = -(-a // MOE_BLOCK) + MOE_EXPERTS
    tok_of = jnp.broadcast_to(jnp.arange(n, dtype=jnp.int32)[:, None], (n, MOE_TOP_K))
    tok = (jnp.arange(n_blocks * MOE_BLOCK, dtype=jnp.int32) % n).at[slot.reshape(a)].set(
        tok_of.reshape(a), unique_indices=True, mode='promise_in_bounds')
    blk_start = jnp.arange(n_blocks, dtype=jnp.int32) * MOE_BLOCK
    blk_expert = jnp.minimum(jnp.sum((pad_end[None, :] <= blk_start[:, None]).astype(jnp.int32), axis=-1),
                             MOE_EXPERTS - 1)
    n_used = (pad_end[-1] // MOE_BLOCK).reshape(1)
    return slot, tok, blk_expert, n_used


def _take_rows(a, idx):
    return a.at[idx].get(mode='promise_in_bounds')


def _moe_layer(x1, u2, route, counts, g2, lng, lnb, w13, w2):
    slot, tok, blk_expert, n_used = _dispatch_plan(route, counts)
    ys = _moe_experts(_take_rows(u2, tok), blk_expert, n_used, w13, w2)
    return _combine(x1, _take_rows(ys, slot[:, 0]), _take_rows(ys, slot[:, 1]), route, g2, lng, lnb)


def _rope_tables():
    t = jnp.arange(L, dtype=jnp.int32)
    pos = jnp.stack([t // GRID_W, t % GRID_W], -1).astype(F32)
    n_freq = HEAD_DIM // 4
    inv_freq = ROPE_THETA ** (-jnp.arange(n_freq, dtype=F32) / n_freq)
    ang = pos[:, :, None] * inv_freq
    cos, sin = jnp.cos(ang), jnp.sin(ang)
    cos_t = jnp.stack([cos, cos], axis=2).reshape(L, HEAD_DIM)
    sin_t = jnp.stack([-sin, sin], axis=2).reshape(L, HEAD_DIM)
    cos_t = jnp.concatenate([cos_t, jnp.ones((C, HEAD_DIM), F32)], axis=0)
    sin_t = jnp.concatenate([sin_t, jnp.zeros((C, HEAD_DIM), F32)], axis=0)
    return cos_t, sin_t


def _router_params(rg_w, rg_b, re_w, re_b):
    n = MOE_GROUPS + MOE_EXPERTS
    w = jnp.concatenate([rg_w, re_w, jnp.zeros((D, ROUTER_PAD - n), F32)], axis=1).astype(BF16)
    b = jnp.concatenate([rg_b, re_b, jnp.zeros((ROUTER_PAD - n,), F32)]).reshape(1, ROUTER_PAD)
    return w, b


def _modulation_inputs(c, c_ctx):
    n = c.shape[0] + 1
    pad = -n % 8
    return jnp.concatenate([c, c_ctx[None, :], jnp.zeros((pad, D), F32)], axis=0)


def _mixer_fn(idx, mixer):
    kind = idx % 4
    if kind == 0:
        w_qkv, rpb, w_o = mixer
        w_qkv, w_o, bias = w_qkv.astype(BF16), w_o.astype(BF16), _na_bias_table(rpb)

        def run(xs, sc1, sh1, n_batch, post):
            qkv = _proj(xs, sc1, sh1, w_qkv, n_out=3 * D, tn=D, out_dtype=BF16)
            return _out_proj(_na_attention(qkv, bias, n_batch), w_o, *post)
    elif kind == 1:
        w_in, b_in, dw, dw_b, cg, cb, w_out, b_out = mixer
        w_in, w_out = w_in.astype(BF16), w_out.astype(BF16)

        def run(xs, sc1, sh1, n_batch, post):
            h = _proj(xs, sc1, sh1, w_in, n_out=D, tn=D // 2, out_dtype=F32, bias=b_in, glu=True)
            return _conv_mixer(h, dw, dw_b, cg, cb, w_out, b_out, *post)
    elif kind == 2:
        w_qkv, sink, w_o = mixer
        n_qk = (SWA_KV_HEADS * SWA_REP + SWA_KV_HEADS) * HEAD_DIM
        n_v = SWA_KV_HEADS * HEAD_DIM
        w_qkv, w_o, rope, mask = w_qkv.astype(BF16), w_o.astype(BF16), _rope_tables(), _swa_mask_table()
        w_qk, w_v = w_qkv[:, :n_qk], w_qkv[:, n_qk:]

        def run(xs, sc1, sh1, n_batch, post):
            qk = _proj(xs, sc1, sh1, w_qk, n_out=n_qk, tn=n_qk, out_dtype=BF16, rope=rope)
            v = _proj(xs, sc1, sh1, w_v, n_out=n_v, tn=n_v, out_dtype=BF16)
            return _out_proj(_swa_attention(qk, v, sink, mask, n_batch), w_o, *post)
    else:
        w_qkv, lam, subln_g, w_o = mixer
        lambda_init = 0.8 - 0.6 * math.exp(-0.3 * idx)
        w_qkv, w_o, rope = w_qkv.astype(BF16), w_o.astype(BF16), _rope_tables()
        w_qk, w_v = w_qkv[:, :2 * D], w_qkv[:, 2 * D:]

        def run(xs, sc1, sh1, n_batch, post):
            qk = _proj(xs, sc1, sh1, w_qk, n_out=2 * D, tn=D, out_dtype=BF16, rope=rope)
            v = _proj(xs, sc1, sh1, w_v, n_out=D, tn=D, out_dtype=BF16)
            return _out_proj(_diff_attention(qk, v, lam, subln_g, lambda_init, n_batch), w_o, *post)
    return run


def _hybrid_layer(idx, streams, cvec, mod_w, mod_b, mixer, ln1_g, ln1_b, moe, ln2_g, ln2_b):
    vec = lambda a: a.reshape(1, D)
    m_all = _adaln(cvec, mod_w, mod_b)
    ctx_row = sum(nb for _, _, nb in streams)
    rg_w, rg_b, re_w, re_b, w13, w2 = moe
    wr, br = _router_params(rg_w, rg_b, re_w, re_b)
    run_mixer = _mixer_fn(idx, mixer)
    out = []
    for xs, b0, nb in streams:
        m = jnp.concatenate([m_all[b0:b0 + nb], m_all[ctx_row:ctx_row + 1]], axis=0)
        sh1, sc1, g1, sh2, sc2, g2 = [m[:, None, k * D:(k + 1) * D] for k in range(6)]
        post = (xs, g1, vec(ln1_g), vec(ln1_b), sc2, sh2, wr, br)
        routed = run_mixer(xs, sc1, sh1, nb, post)
        out.append((_moe_layer(*routed, g2, vec(ln2_g), vec(ln2_b), w13, w2), b0, nb))
    return out


def kernel(x, c, ctx, c_ctx, l0_mod_w, l0_mod_b, l0_na_w_qkv, l0_na_rpb, l0_na_w_o, l0_ln1_g, l0_ln1_b, l0_router_g_w, l0_router_g_b, l0_router_e_w, l0_router_e_b, l0_moe_w13, l0_moe_w2, l0_ln2_g, l0_ln2_b, l1_mod_w, l1_mod_b, l1_cv_w_in, l1_cv_b_in, l1_cv_dw, l1_cv_dw_b, l1_cv_ln_g, l1_cv_ln_b, l1_cv_w_out, l1_cv_b_out, l1_ln1_g, l1_ln1_b, l1_router_g_w, l1_router_g_b, l1_router_e_w, l1_router_e_b, l1_moe_w13, l1_moe_w2, l1_ln2_g, l1_ln2_b, l2_mod_w, l2_mod_b, l2_sw_w_qkv, l2_sw_sink, l2_sw_w_o, l2_ln1_g, l2_ln1_b, l2_router_g_w, l2_router_g_b, l2_router_e_w, l2_router_e_b, l2_moe_w13, l2_moe_w2, l2_ln2_g, l2_ln2_b, l3_mod_w, l3_mod_b, l3_df_w_qkv, l3_df_lambda, l3_df_subln_g, l3_df_w_o, l3_ln1_g, l3_ln1_b, l3_router_g_w, l3_router_g_b, l3_router_e_w, l3_router_e_b, l3_moe_w13, l3_moe_w2, l3_ln2_g, l3_ln2_b):
    layers = (
        (l0_mod_w, l0_mod_b, (l0_na_w_qkv, l0_na_rpb, l0_na_w_o), l0_ln1_g, l0_ln1_b,
         (l0_router_g_w, l0_router_g_b, l0_router_e_w, l0_router_e_b, l0_moe_w13, l0_moe_w2), l0_ln2_g, l0_ln2_b),
        (l1_mod_w, l1_mod_b, (l1_cv_w_in, l1_cv_b_in, l1_cv_dw, l1_cv_dw_b, l1_cv_ln_g, l1_cv_ln_b, l1_cv_w_out,
                              l1_cv_b_out), l1_ln1_g, l1_ln1_b,
         (l1_router_g_w, l1_router_g_b, l1_router_e_w, l1_router_e_b, l1_moe_w13, l1_moe_w2), l1_ln2_g, l1_ln2_b),
        (l2_mod_w, l2_mod_b, (l2_sw_w_qkv, l2_sw_sink, l2_sw_w_o), l2_ln1_g, l2_ln1_b,
         (l2_router_g_w, l2_router_g_b, l2_router_e_w, l2_router_e_b, l2_moe_w13, l2_moe_w2), l2_ln2_g, l2_ln2_b),
        (l3_mod_w, l3_mod_b, (l3_df_w_qkv, l3_df_lambda, l3_df_subln_g, l3_df_w_o), l3_ln1_g, l3_ln1_b,
         (l3_router_g_w, l3_router_g_b, l3_router_e_w, l3_router_e_b, l3_moe_w13, l3_moe_w2), l3_ln2_g, l3_ln2_b),
    )
    n_batch = x.shape[0]
    assert x.shape[1:] == (L, D) and ctx.shape[1:] == (C, D)
    n_streams = N_STREAMS if n_batch % N_STREAMS == 0 else 1
    nb = n_batch // n_streams
    streams = [(jnp.concatenate([x[b0:b0 + nb], ctx[b0:b0 + nb]], axis=1).reshape(nb * S, D), b0, nb)
               for b0 in range(0, n_batch, nb)]
    cvec = _modulation_inputs(c, c_ctx)
    for idx in range(DEPTH):
        streams = _hybrid_layer(idx, streams, cvec, *layers[idx])
    return jnp.concatenate([xs.reshape(nb, S, D)[:, :L] for xs, _, _ in streams], axis=0)
```

```python
import functools
import math

import jax
import jax.numpy as jnp
from jax import lax
from jax.experimental import pallas as pl
from jax.experimental.pallas import tpu as pltpu

D = 2048
L = 2048
C = 256
S = L + C
DEPTH = 4
GRID_W = 64
HEAD_DIM = 128
ROPE_THETA = 10000.0
LN_EPS = 1e-5
NEG_INF = -1e30
DEEPNORM_ALPHA = (2.0 * DEPTH) ** 0.25
NA_HEADS = 16
NA_KH = 8
NA_KW = 16
NA_QROWS = 4
NA_KROWS = 12
CONV_WIDTH = 31
CONV_HALO = 16
SWA_KV_HEADS = 4
SWA_REP = 4
SWA_WINDOW = 128
SWA_BLOCK = 128
DIFF_HEADS = 8
MOE_GROUPS = 4
MOE_EPG = 8
MOE_EXPERTS = 32
MOE_TOP_K = 2
MOE_D_FF = 512
MOE_BLOCK = 256
ROUTER_PAD = 128

TM = 256
TILES_PER_BATCH = S // TM
LAT_TILES = L // TM
N_STREAMS = 1
VMEM_LIMIT = 52 * 1024 * 1024
SCALE = HEAD_DIM ** -0.5
LOG2E = math.log2(math.e)

F32 = jnp.float32
BF16 = jnp.bfloat16


def _opts(n_axes, *, flops, nbytes, transcendentals=0):
    return dict(
        compiler_params=pltpu.CompilerParams(dimension_semantics=("arbitrary",) * n_axes,
                                             vmem_limit_bytes=VMEM_LIMIT),
        cost_estimate=pl.CostEstimate(flops=int(flops), transcendentals=int(transcendentals),
                                      bytes_accessed=int(nbytes)))


def _dot(a, b):
    return jnp.dot(a, b, preferred_element_type=F32)


def _dot_t(a, b):
    return lax.dot_general(a, b, (((1,), (1,)), ((), ())), preferred_element_type=F32)


def _sigmoid(x):
    return 1.0 / (1.0 + jnp.exp(-x))


def _layer_norm(z, g, b):
    mu = jnp.mean(z, axis=-1, keepdims=True)
    zc = z - mu
    var = jnp.mean(zc * zc, axis=-1, keepdims=True)
    return zc * lax.rsqrt(var + LN_EPS) * g + b


def _mod_group(i, n_batch):
    return jnp.where(i % TILES_PER_BATCH == TILES_PER_BATCH - 1, n_batch, i // TILES_PER_BATCH)


def _adaln_kernel(c_ref, w_ref, b_ref, o_ref):
    c = c_ref[...]
    s = c * _sigmoid(c)
    o_ref[...] = _dot(s.astype(BF16), w_ref[...].astype(BF16)) + b_ref[...]


def _adaln(cvec, w, b):
    r = cvec.shape[0]
    n = w.shape[1]
    tn = 1024
    return pl.pallas_call(
        _adaln_kernel,
        grid=(n // tn,),
        in_specs=[pl.BlockSpec((r, D), lambda j: (0, 0)),
                  pl.BlockSpec((D, tn), lambda j: (0, j)),
                  pl.BlockSpec((1, tn), lambda j: (0, j))],
        out_specs=pl.BlockSpec((r, tn), lambda j: (0, j)),
        out_shape=jax.ShapeDtypeStruct((r, n), F32),
        **_opts(1, flops=2 * r * D * n, nbytes=4 * D * n),
        name="adaln",
    )(cvec, w, b.reshape(1, n))


def _rope_rotate(y, cos, sin):
    lane = lax.broadcasted_iota(jnp.int32, y.shape, 1)
    partner = jnp.where(lane % 64 < 32, pltpu.roll(y, 96, 1), pltpu.roll(y, 32, 1))
    return y * cos + partner * sin


def _proj_kernel(*refs, has_bias, glu, rope, tn):
    it = iter(refs)
    x_ref, sc_ref, sh_ref, w_ref = next(it), next(it), next(it), next(it)
    wg_ref = next(it) if glu else None
    b_ref = next(it) if has_bias else None
    bg_ref = next(it) if glu else None
    cos_ref = next(it) if rope else None
    sin_ref = next(it) if rope else None
    o_ref = next(it)
    wb_ref = next(it)
    wgb_ref = next(it) if glu else None

    @pl.when(pl.program_id(1) == 0)
    def _():
        wb_ref[...] = w_ref[...].astype(BF16)
        if glu:
            wgb_ref[...] = wg_ref[...].astype(BF16)

    u = (x_ref[...] * (1.0 + sc_ref[0]) + sh_ref[0]).astype(BF16)
    y = _dot(u, wb_ref[...])
    if has_bias:
        y = y + b_ref[...]
    if glu:
        y = y * _sigmoid(_dot(u, wgb_ref[...]) + bg_ref[...])
    if rope:
        cos = cos_ref[...]
        sin = sin_ref[...]
        for h in range(tn // HEAD_DIM):
            sl = slice(h * HEAD_DIM, (h + 1) * HEAD_DIM)
            o_ref[:, sl] = _rope_rotate(y[:, sl], cos, sin).astype(o_ref.dtype)
    else:
        o_ref[...] = y.astype(o_ref.dtype)


def _proj(x, sc, sh, w, *, n_out, tn, out_dtype, col_off=0, bias=None, glu=False, rope=None):
    n_rows = x.shape[0]
    n_batch = sc.shape[0] - 1
    n_tiles = n_rows // TM
    n_col = n_out // tn
    grp = lambda j, i: (_mod_group(i, n_batch), 0, 0)
    w_spec = lambda off: pl.BlockSpec((D, tn), lambda j, i: (0, j + off), pipeline_mode=pl.Buffered(1))
    in_specs = [pl.BlockSpec((TM, D), lambda j, i: (i, 0)),
                pl.BlockSpec((1, 1, D), grp),
                pl.BlockSpec((1, 1, D), grp),
                w_spec(col_off)]
    args = [x, sc, sh, w]
    scratch = [pltpu.VMEM((D, tn), BF16)]
    if glu:
        in_specs.append(w_spec(col_off + n_col))
        args.append(w)
        scratch.append(pltpu.VMEM((D, tn), BF16))
    if bias is not None:
        b2 = bias.reshape(1, -1)
        in_specs.append(pl.BlockSpec((1, tn), lambda j, i: (0, j + col_off)))
        args.append(b2)
        if glu:
            in_specs.append(pl.BlockSpec((1, tn), lambda j, i: (0, j + col_off + n_col)))
            args.append(b2)
    if rope is not None:
        rope_spec = pl.BlockSpec((TM, HEAD_DIM), lambda j, i: (i % TILES_PER_BATCH, 0))
        in_specs += [rope_spec, rope_spec]
        args += [rope[0], rope[1]]
    return pl.pallas_call(
        functools.partial(_proj_kernel, has_bias=bias is not None, glu=glu, rope=rope is not None, tn=tn),
        grid=(n_col, n_tiles),
        in_specs=in_specs,
        out_specs=pl.BlockSpec((TM, tn), lambda j, i: (i, j)),
        out_shape=jax.ShapeDtypeStruct((n_rows, n_out), out_dtype),
        scratch_shapes=scratch,
        **_opts(2, flops=2 * n_rows * D * n_out * (2 if glu else 1),
                nbytes=4 * n_rows * D * n_col + 4 * D * n_out * (2 if glu else 1) + 4 * n_rows * n_out),
        name="proj",
    )(*args)


ROUTE_EXPERT, ROUTE_GATE, ROUTE_RANK = 0, 2, 4


def _first_lane_where(cond, lane):
    return jnp.min(jnp.where(cond, lane, ROUTER_PAD), axis=-1, keepdims=True)


def _route_tile(lg, cnt_ref, counted):
    lane = lax.broadcasted_iota(jnp.int32, lg.shape, 1)
    gmask = lane < MOE_GROUPS
    gl = jnp.where(gmask, lg, NEG_INF)
    ge = jnp.exp(gl - gl.max(axis=-1, keepdims=True))
    gp = ge / ge.sum(axis=-1, keepdims=True)
    g_p = gp.max(axis=-1, keepdims=True)
    g_idx = _first_lane_where(gmask & (gp == g_p), lane)
    lo = MOE_GROUPS + MOE_EPG * g_idx
    emask = (lane >= lo) & (lane < lo + MOE_EPG)
    el = jnp.where(emask, lg, NEG_INF)
    ee = jnp.exp(el - el.max(axis=-1, keepdims=True))
    ep = jnp.where(emask, ee / ee.sum(axis=-1, keepdims=True), -1.0)
    p1 = ep.max(axis=-1, keepdims=True)
    i1 = _first_lane_where(ep == p1, lane)
    ep2 = jnp.where(lane == i1, -1.0, ep)
    p2 = ep2.max(axis=-1, keepdims=True)
    i2 = _first_lane_where(ep2 == p2, lane)
    den = p1 + p2
    gate1 = g_p * p1 / den
    gate2 = g_p * p2 / den
    oh1 = jnp.where(lane == i1, 1.0, 0.0)
    oh2 = jnp.where(lane == i2, 1.0, 0.0)
    n = lg.shape[0]
    tri = jnp.where(lax.broadcasted_iota(jnp.int32, (n, n), 0) > lax.broadcasted_iota(jnp.int32, (n, n), 1),
                    1.0, 0.0).astype(BF16)
    base = cnt_ref[...]
    tot1 = oh1.sum(axis=0, keepdims=True)
    pre1 = _dot(tri, oh1.astype(BF16)) + base
    pre2 = _dot(tri, oh2.astype(BF16)) + (base + tot1)
    rank1 = (oh1 * pre1).sum(axis=-1, keepdims=True)
    rank2 = (oh2 * pre2).sum(axis=-1, keepdims=True)
    cnt_ref[...] = base + jnp.where(counted, tot1 + oh2.sum(axis=0, keepdims=True), 0.0)
    cols = ((i1 - MOE_GROUPS).astype(F32), (i2 - MOE_GROUPS).astype(F32), gate1, gate2, rank1, rank2)
    route = jnp.zeros(lg.shape, F32)
    for k, col in enumerate(cols):
        route = jnp.where(lane == k, col, route)
    return route


def _pack_bf16_halves(u):
    bits = pltpu.bitcast(u.astype(F32), jnp.uint32)
    half = u.shape[1] // 2
    return bits[:, :half] | (bits[:, half:] >> 16)


def _unpack_halves_f32(p):
    return pltpu.bitcast(p & jnp.uint32(0xFFFF0000), F32), pltpu.bitcast(p << 16, F32)


def _unpack_bf16_halves(p):
    hi, lo = _unpack_halves_f32(p)
    return hi.astype(BF16), lo.astype(BF16)


def _post_mixer(y_rows, x_ref, g_ref, lng_ref, lnb_ref, sc2_ref, sh2_ref, wr_ref, br_ref,
                x1_ref, u2_ref, route_ref, cnt_out_ref, cnt_ref, counted=True, n_chunks=1, before_chunk=None):
    logits = []
    for c in range(n_chunks):
        if before_chunk is not None:
            before_chunk(c)
        rows = slice(c * TM // n_chunks, (c + 1) * TM // n_chunks)
        z = DEEPNORM_ALPHA * x_ref[rows, :] + g_ref[0] * y_rows(rows)
        x1 = _layer_norm(z, lng_ref[...], lnb_ref[...])
        x1_ref[rows, :] = x1
        u2 = (x1 * (1.0 + sc2_ref[0]) + sh2_ref[0]).astype(BF16)
        u2_ref[rows, :] = _pack_bf16_halves(u2)
        logits.append(_dot(u2, wr_ref[...]) + br_ref[...])
    logits = logits[0] if n_chunks == 1 else jnp.concatenate(logits, axis=0)
    route_ref[...] = _route_tile(logits, cnt_ref, counted)
    cnt_out_ref[...] = jnp.broadcast_to(cnt_ref[...], cnt_out_ref.shape)


def _post_mixer_specs(n_batch, lag=0):
    tile = lambda i: jnp.maximum(i - lag, 0)
    grp = lambda i: (_mod_group(tile(i), n_batch), 0, 0)
    row = lambda i: (tile(i), 0)
    const = lambda i: (0, 0)
    in_specs = [pl.BlockSpec((TM, D), row),
                pl.BlockSpec((1, 1, D), grp),
                pl.BlockSpec((1, D), const),
                pl.BlockSpec((1, D), const),
                pl.BlockSpec((1, 1, D), grp),
                pl.BlockSpec((1, 1, D), grp),
                pl.BlockSpec((D, ROUTER_PAD), const),
                pl.BlockSpec((1, ROUTER_PAD), const)]
    out_specs = [pl.BlockSpec((TM, D), row),
                 pl.BlockSpec((TM, D // 2), row),
                 pl.BlockSpec((TM, ROUTER_PAD), row),
                 pl.BlockSpec((8, ROUTER_PAD), const)]
    return in_specs, out_specs


def _post_mixer_out_shape(n_rows):
    return [jax.ShapeDtypeStruct((n_rows, D), F32),
            jax.ShapeDtypeStruct((n_rows, D // 2), jnp.uint32),
            jax.ShapeDtypeStruct((n_rows, ROUTER_PAD), F32),
            jax.ShapeDtypeStruct((8, ROUTER_PAD), F32)]


def _post_mixer_scratch():
    return [pltpu.VMEM((1, ROUTER_PAD), F32)]


OUT_PROJ_CHUNKS = 4


def _out_proj_kernel(o_ref, wo_ref, *rest):
    cnt_ref, y_even_ref, y_odd_ref = rest[-3:]
    i = pl.program_id(0)

    @pl.when(i == 0)
    def _():
        cnt_ref[...] = jnp.zeros(cnt_ref.shape, F32)
        y_odd_ref[...] = jnp.zeros(y_odd_ref.shape, F32)

    def step(cur_ref, prev_ref):
        def project(c):
            cols = slice(c * D // OUT_PROJ_CHUNKS, (c + 1) * D // OUT_PROJ_CHUNKS)
            cur_ref[:, cols] = _dot(o_ref[...], wo_ref[:, cols])

        _post_mixer(lambda rows: prev_ref[rows, :], *rest[:-2], counted=i > 0,
                    n_chunks=OUT_PROJ_CHUNKS, before_chunk=project)

    @pl.when(i % 2 == 0)
    def _():
        step(y_even_ref, y_odd_ref)

    @pl.when(i % 2 == 1)
    def _():
        step(y_odd_ref, y_even_ref)


def _out_proj(o, wo, x, g1, lng, lnb, sc2, sh2, wr, br):
    n_rows = x.shape[0]
    n_batch = g1.shape[0] - 1
    n_tiles = n_rows // TM
    pm_in, pm_out = _post_mixer_specs(n_batch, lag=1)
    return pl.pallas_call(
        _out_proj_kernel,
        grid=(n_tiles + 1,),
        in_specs=[pl.BlockSpec((TM, D), lambda i: (jnp.minimum(i, n_tiles - 1), 0)),
                  pl.BlockSpec((D, D), lambda i: (0, 0))] + pm_in,
        out_specs=pm_out,
        out_shape=_post_mixer_out_shape(n_rows),
        scratch_shapes=_post_mixer_scratch() + [pltpu.VMEM((TM, D), F32), pltpu.VMEM((TM, D), F32)],
        **_opts(1, flops=2 * n_rows * D * D, nbytes=12 * n_rows * D + 2 * D * D),
        name="out_proj",
    )(o, wo, x, g1, lng, lnb, sc2, sh2, wr, br)


def _na_key_row_start(j, rows):
    return jnp.clip(NA_QROWS * j - NA_KH // 2, 0, rows - NA_KROWS)


def _na_bias_table(rpb):
    rows = L // GRID_W
    n_blocks = rows // NA_QROWS
    j = jnp.array([0, 1, n_blocks - 1])
    qr = (NA_QROWS * j)[:, None] + jnp.arange(NA_QROWS)[None, :]
    kr = _na_key_row_start(j, rows)[:, None] + jnp.arange(NA_KROWS)[None, :]
    r0 = jnp.clip(qr - NA_KH // 2, 0, rows - NA_KH)
    row_ok = (kr[:, None, :] >= r0[:, :, None]) & (kr[:, None, :] < r0[:, :, None] + NA_KH)
    dr = jnp.clip(kr[:, None, :] - qr[:, :, None] + NA_KH - 1, 0, 2 * NA_KH - 2)
    cols = jnp.arange(GRID_W)
    col_start = jnp.clip(cols - NA_KW // 2, 0, GRID_W - NA_KW)
    col_ok = (cols[None, :] >= col_start[:, None]) & (cols[None, :] < col_start[:, None] + NA_KW)
    dc = jnp.clip(cols[None, :] - cols[:, None] + NA_KW - 1, 0, 2 * NA_KW - 2)
    by_row = jnp.where(col_ok, rpb[:, :, dc] * (1.0 / SCALE), NEG_INF)
    bias = jnp.where(row_ok[None, :, :, :, None, None], by_row[:, dr], NEG_INF)
    bias = jnp.transpose(bias, (0, 1, 2, 4, 3, 5))
    return bias.reshape(NA_HEADS, 3, NA_QROWS * GRID_W, NA_KROWS * GRID_W)


NA_GROUP = 2


def _na_kernel(q_ref, k_ref, v_ref, bias_ref, o_ref):
    rows = L // GRID_W
    qb = NA_QROWS * GRID_W
    kb = NA_KROWS * GRID_W
    n_blocks = rows // NA_QROWS
    kc = k_ref[L:S, :]
    vc = v_ref[L:S, :]

    def attend(blocks):
        scores = []
        for q_rows, k_rows, bias in blocks:
            q = q_ref[q_rows, :]
            s = _dot_t(q, kc)
            if k_rows is not None:
                s = jnp.concatenate([_dot_t(q, k_ref[k_rows, :]) + bias, s], axis=1)
            scores.append(s)
        probs = []
        for s in scores:
            p = jnp.exp2((s - s.max(axis=-1, keepdims=True)) * (SCALE * LOG2E))
            probs.append((p.astype(BF16), p.sum(axis=-1, keepdims=True)))
        for (q_rows, k_rows, _), (p, den) in zip(blocks, probs):
            if k_rows is None:
                o = _dot(p, vc)
            else:
                o = _dot(p[:, :kb], v_ref[k_rows, :]) + _dot(p[:, kb:], vc)
            o_ref[q_rows, :] = (o / den).astype(o_ref.dtype)

    def step(i, carry):
        blocks = []
        for u in range(NA_GROUP):
            j = i * NA_GROUP + u
            ks = pl.multiple_of(_na_key_row_start(j, rows) * GRID_W, GRID_W)
            kind = jnp.where(j == 0, 0, jnp.where(j == n_blocks - 1, 2, 1))
            blocks.append((pl.ds(pl.multiple_of(j * qb, qb), qb), pl.ds(ks, kb), bias_ref[0, kind]))
        attend(blocks)
        return carry

    lax.fori_loop(0, n_blocks // NA_GROUP, step, 0)
    attend([(slice(L, S), None, None)])


def _na_attention(qkv, bias, n_batch):
    h = NA_HEADS
    n_blocks = bias.shape[1]
    blk = lambda off: pl.BlockSpec((S, HEAD_DIM), lambda hh, b: (b, off + hh))
    return pl.pallas_call(
        _na_kernel,
        grid=(h, n_batch),
        in_specs=[blk(0), blk(h), blk(2 * h),
                  pl.BlockSpec((1, n_blocks) + bias.shape[2:], lambda hh, b: (hh, 0, 0, 0))],
        out_specs=pl.BlockSpec((S, HEAD_DIM), lambda hh, b: (b, hh)),
        out_shape=jax.ShapeDtypeStruct((n_batch * S, h * HEAD_DIM), BF16),
        **_opts(2, flops=4 * n_batch * h * S * (NA_KROWS * GRID_W + C) * HEAD_DIM,
                nbytes=8 * n_batch * S * h * HEAD_DIM,
                transcendentals=n_batch * h * S * (NA_KROWS * GRID_W + C)),
        name="na_attn",
    )(qkv, qkv, qkv, bias)


def _swa_mask_table():
    span = SWA_BLOCK + 2 * SWA_WINDOW
    qi = jnp.arange(SWA_BLOCK)[:, None]
    kj = jnp.arange(span)[None, :]
    band = jnp.abs(qi - (kj - SWA_WINDOW)) <= SWA_WINDOW
    first = band & (kj >= SWA_WINDOW)
    last = band & (kj < SWA_BLOCK + SWA_WINDOW)
    m = jnp.stack([first, band, last]).astype(F32)
    return jnp.where(m > 0, 0.0, NEG_INF).astype(F32)


def _swa_kernel(sink_ref, q_ref, k_ref, v_ref, mask_ref, o_ref, kpad_ref, vpad_ref):
    g = pl.program_id(1)
    nb = L // SWA_BLOCK
    span = SWA_BLOCK + 2 * SWA_WINDOW
    pad = SWA_WINDOW
    zeros = jnp.zeros((pad, HEAD_DIM), BF16)
    for src, dst in ((k_ref, kpad_ref), (v_ref, vpad_ref)):
        dst[0:pad, :] = zeros
        dst[pad + L:, :] = zeros
        dst[pad:pad + L, :] = src[0:L, :]
    kc = k_ref[L:S, :]
    vc = v_ref[L:S, :]

    def head(r):
        return slice(r * HEAD_DIM, (r + 1) * HEAD_DIM)

    def attend(rows, keys, values, mask):
        v = values[0] if len(values) == 1 else jnp.concatenate(values, axis=0)
        scores = []
        for r in range(SWA_REP):
            q = q_ref[rows, head(r)]
            s = [_dot_t(q, k) for k in keys]
            if mask is not None:
                s[0] = s[0] + mask
            scores.append(s[0] if len(s) == 1 else jnp.concatenate(s, axis=1))
        probs = []
        for r, s in enumerate(scores):
            sink_raw = sink_ref[g * SWA_REP + r] * (1.0 / SCALE)
            m = jnp.maximum(s.max(axis=-1, keepdims=True), sink_raw)
            p = jnp.exp2((s - m) * (SCALE * LOG2E))
            den = p.sum(axis=-1, keepdims=True) + jnp.exp2((sink_raw - m) * (SCALE * LOG2E))
            probs.append((p.astype(BF16), den))
        for r, (p, den) in enumerate(probs):
            o_ref[rows, head(r)] = (_dot(p, v) / den).astype(o_ref.dtype)

    def block(n, carry):
        start = pl.multiple_of(n * SWA_BLOCK, SWA_BLOCK)
        kind = jnp.where(n == 0, 0, jnp.where(n == nb - 1, 2, 1))
        attend(pl.ds(start, SWA_BLOCK), [kpad_ref[pl.ds(start, span), :], kc],
               [vpad_ref[pl.ds(start, span), :], vc], mask_ref[kind])
        return carry

    lax.fori_loop(0, nb, block, 0, unroll=2)
    attend(slice(L, S), [kc], [vc], None)


def _swa_attention(qk, v, sink, mask, n_batch):
    qw = SWA_REP * HEAD_DIM
    n_q_blocks = SWA_KV_HEADS
    return pl.pallas_call(
        _swa_kernel,
        grid_spec=pltpu.PrefetchScalarGridSpec(
            num_scalar_prefetch=1,
            grid=(n_batch, SWA_KV_HEADS),
            in_specs=[pl.BlockSpec((S, qw), lambda b, g, sk: (b, g)),
                      pl.BlockSpec((S, HEAD_DIM), lambda b, g, sk: (b, n_q_blocks * SWA_REP + g)),
                      pl.BlockSpec((S, HEAD_DIM), lambda b, g, sk: (b, g)),
                      pl.BlockSpec(mask.shape, lambda b, g, sk: (0, 0, 0))],
            out_specs=pl.BlockSpec((S, qw), lambda b, g, sk: (b, g)),
            scratch_shapes=[pltpu.VMEM((L + 2 * SWA_WINDOW, HEAD_DIM), BF16),
                            pltpu.VMEM((L + 2 * SWA_WINDOW, HEAD_DIM), BF16)]),
        out_shape=jax.ShapeDtypeStruct((n_batch * S, SWA_KV_HEADS * qw), BF16),
        **_opts(2, flops=4 * n_batch * SWA_KV_HEADS * SWA_REP * S * (SWA_BLOCK + 2 * SWA_WINDOW + C) * HEAD_DIM,
                nbytes=2 * n_batch * S * (2 * SWA_KV_HEADS * qw + 2 * SWA_KV_HEADS * HEAD_DIM),
                transcendentals=n_batch * SWA_KV_HEADS * SWA_REP * S * (SWA_BLOCK + 2 * SWA_WINDOW + C)),
        name="swa_attn",
    )(sink, qk, qk, v, mask)


DIFF_GROUP = 2


def _diff_kernel(lam_ref, g_ref, q_ref, k_ref, v_ref, o_ref, *, lambda_init, tq):
    lam = lam_ref[...]
    lmbda = (jnp.exp(jnp.sum(lam[0:1] * lam[1:2], axis=-1, keepdims=True))
             - jnp.exp(jnp.sum(lam[2:3] * lam[3:4], axis=-1, keepdims=True)) + lambda_init)
    gain = g_ref[...] * (1.0 - lambda_init)

    def softmax_parts(s):
        m = s.max(axis=-1, keepdims=True)
        e = jnp.exp2((s - m) * (SCALE * LOG2E))
        return e, e.sum(axis=-1, keepdims=True)

    def attend(q_blocks, keys):
        scores = []
        for q_rows in q_blocks:
            q = q_ref[q_rows, :]
            scores.append((_dot_t(q[:, :HEAD_DIM], k_ref[keys, 0:HEAD_DIM]),
                           _dot_t(q[:, HEAD_DIM:], k_ref[keys, HEAD_DIM:2 * HEAD_DIM])))
        probs = []
        for s1, s2 in scores:
            e1, d1 = softmax_parts(s1)
            e2, d2 = softmax_parts(s2)
            probs.append((e1 * (1.0 / d1) - e2 * (lmbda / d2)).astype(BF16))
        for q_rows, pd in zip(q_blocks, probs):
            o = _dot(pd, v_ref[keys, :])
            o = o * lax.rsqrt(jnp.mean(o * o, axis=-1, keepdims=True) + LN_EPS) * gain
            o_ref[q_rows, :] = o.astype(o_ref.dtype)

    def step(i, carry):
        attend([pl.ds(pl.multiple_of((i * DIFF_GROUP + u) * tq, tq), tq) for u in range(DIFF_GROUP)], slice(0, S))
        return carry

    lax.fori_loop(0, L // (tq * DIFF_GROUP), step, 0)
    attend([slice(L, S)], slice(L, S))


def _diff_attention(qk, v, lam, subln_g, lambda_init, n_batch):
    hw = 2 * HEAD_DIM
    return pl.pallas_call(
        functools.partial(_diff_kernel, lambda_init=lambda_init, tq=256),
        grid=(n_batch, DIFF_HEADS),
        in_specs=[pl.BlockSpec((4, HEAD_DIM), lambda b, h: (0, 0)),
                  pl.BlockSpec((1, hw), lambda b, h: (0, 0)),
                  pl.BlockSpec((S, hw), lambda b, h: (b, h)),
                  pl.BlockSpec((S, hw), lambda b, h: (b, DIFF_HEADS + h)),
                  pl.BlockSpec((S, hw), lambda b, h: (b, h))],
        out_specs=pl.BlockSpec((S, hw), lambda b, h: (b, h)),
        out_shape=jax.ShapeDtypeStruct((n_batch * S, DIFF_HEADS * hw), BF16),
        **_opts(2, flops=8 * n_batch * DIFF_HEADS * S * S * HEAD_DIM, nbytes=8 * n_batch * S * DIFF_HEADS * hw,
                transcendentals=2 * n_batch * DIFF_HEADS * S * S),
        name="diff_attn",
    )(lam, subln_g.reshape(1, hw), qk, qk, v)


CONV_ROWS = 128


def _conv_kernel(hc_ref, hp_ref, hn_ref, dw_ref, dwb_ref, cg_ref, cb_ref, wout_ref, bout_ref, *rest):
    cnt_ref, win_ref, cv_ref = rest[-3:]

    @pl.when(pl.program_id(0) == 0)
    def _():
        cnt_ref[...] = jnp.zeros(cnt_ref.shape, F32)

    t = pl.program_id(0) % TILES_PER_BATCH
    first = (t == 0) | (t == TILES_PER_BATCH - 1)
    last = t >= TILES_PER_BATCH - 2
    win_ref[0:CONV_HALO, :] = jnp.where(first, 0.0, hp_ref[...])
    win_ref[CONV_HALO:CONV_HALO + TM, :] = hc_ref[...]
    win_ref[CONV_HALO + TM:, :] = jnp.where(last, 0.0, hn_ref[...])
    base = CONV_HALO - CONV_WIDTH // 2
    sub = 8

    def strip(c, carry):
        cs = pl.ds(pl.multiple_of(c * HEAD_DIM, HEAD_DIM), HEAD_DIM)
        for r in range(TM // CONV_ROWS):
            acc = jnp.zeros((CONV_ROWS, HEAD_DIM), F32)
            aligned = win_ref[pl.ds(r * CONV_ROWS, CONV_ROWS + 2 * CONV_HALO), cs]
            for res in range(sub):
                taps = [j for j in range(CONV_WIDTH) if (base + j) % sub == res]
                w = aligned if res == 0 else pltpu.roll(aligned, aligned.shape[0] - res, 0)
                for j in taps:
                    lo = (base + j) // sub * sub
                    acc = acc + dw_ref[j:j + 1, cs] * w[lo:lo + CONV_ROWS]
            cv_ref[r * CONV_ROWS:(r + 1) * CONV_ROWS, cs] = acc
        return carry

    lax.fori_loop(0, D // HEAD_DIM, strip, 0)

    hn = _layer_norm(cv_ref[...] + dwb_ref[...], cg_ref[...], cb_ref[...])
    hn = (hn * _sigmoid(hn)).astype(BF16)
    y = _dot(hn, wout_ref[...]) + bout_ref[...]
    _post_mixer(lambda rows: y[rows, :], *rest[:-2])


def _conv_mixer(h, dw, dw_b, cg, cb, wout, bout, x, g1, lng, lnb, sc2, sh2, wr, br):
    n_rows = x.shape[0]
    n_batch = g1.shape[0] - 1
    n_tiles = n_rows // TM
    halo_per_tile = TM // CONV_HALO
    n_halo = n_rows // CONV_HALO
    pm_in, pm_out = _post_mixer_specs(n_batch)
    const = lambda i: (0, 0)
    vec = lambda a: a.reshape(1, D)
    dw_pad = jnp.concatenate([dw, jnp.zeros((1, D), dw.dtype)], axis=0)
    return pl.pallas_call(
        _conv_kernel,
        grid=(n_tiles,),
        in_specs=[pl.BlockSpec((TM, D), lambda i: (i, 0)),
                  pl.BlockSpec((CONV_HALO, D), lambda i: (jnp.maximum(i * halo_per_tile - 1, 0), 0)),
                  pl.BlockSpec((CONV_HALO, D), lambda i: (jnp.minimum((i + 1) * halo_per_tile, n_halo - 1), 0)),
                  pl.BlockSpec((CONV_WIDTH + 1, D), const),
                  pl.BlockSpec((1, D), const),
                  pl.BlockSpec((1, D), const),
                  pl.BlockSpec((1, D), const),
                  pl.BlockSpec((D, D), const),
                  pl.BlockSpec((1, D), const)] + pm_in,
        out_specs=pm_out,
        out_shape=_post_mixer_out_shape(n_rows),
        scratch_shapes=_post_mixer_scratch() + [pltpu.VMEM((TM + 2 * CONV_HALO, D), F32),
                                                pltpu.VMEM((TM, D), F32)],
        **_opts(1, flops=2 * n_rows * D * (D + CONV_WIDTH), nbytes=14 * n_rows * D + 2 * D * D),
        name="conv_mixer",
    )(h, h, h, dw_pad, vec(dw_b), vec(cg), vec(cb), wout, vec(bout), x, g1, lng, lnb, sc2, sh2, wr, br)


def _moe_kernel(be_ref, nu_ref, xs_ref, w13_ref, w2_ref, ys_ref, w13b_ref, w2b_ref):
    i = pl.program_id(0)

    @pl.when(i >= nu_ref[0])
    def _():
        ys_ref[...] = jnp.zeros(ys_ref.shape, ys_ref.dtype)

    @pl.when(i < nu_ref[0])
    def _():
        changed = (i == 0) | (be_ref[i] != be_ref[jnp.maximum(i - 1, 0)])

        @pl.when(changed)
        def _():
            w13b_ref[...] = w13_ref[0].astype(BF16)
            w2b_ref[...] = w2_ref[0].astype(BF16)

        x_hi, x_lo = _unpack_bf16_halves(xs_ref[...])
        a = _dot(x_hi, w13b_ref[0:D // 2, :]) + _dot(x_lo, w13b_ref[D // 2:, :])
        gate = a[:, :MOE_D_FF]
        hmid = (gate * _sigmoid(gate) * a[:, MOE_D_FF:]).astype(BF16)
        ys_ref[...] = _pack_bf16_halves(_dot(hmid, w2b_ref[...]).astype(BF16))


def _moe_experts(xs, blk_expert, n_used, w13, w2):
    n_slots = xs.shape[0]
    n_blocks = n_slots // MOE_BLOCK
    live = lambda i, nu: jnp.minimum(i, nu[0] - 1)
    return pl.pallas_call(
        _moe_kernel,
        grid_spec=pltpu.PrefetchScalarGridSpec(
            num_scalar_prefetch=2,
            grid=(n_blocks,),
            in_specs=[pl.BlockSpec((MOE_BLOCK, D // 2), lambda i, be, nu: (live(i, nu), 0)),
                      pl.BlockSpec((1, D, 2 * MOE_D_FF), lambda i, be, nu: (be[live(i, nu)], 0, 0)),
                      pl.BlockSpec((1, MOE_D_FF, D), lambda i, be, nu: (be[live(i, nu)], 0, 0))],
            out_specs=pl.BlockSpec((MOE_BLOCK, D // 2), lambda i, be, nu: (i, 0)),
            scratch_shapes=[pltpu.VMEM((D, 2 * MOE_D_FF), BF16),
                            pltpu.VMEM((MOE_D_FF, D), BF16)]),
        out_shape=jax.ShapeDtypeStruct((n_slots, D // 2), jnp.uint32),
        **_opts(1, flops=6 * n_slots * D * MOE_D_FF, nbytes=4 * n_slots * D + 12 * MOE_EXPERTS * D * MOE_D_FF),
        name="moe_experts",
    )(blk_expert, n_used, xs, w13, w2)


def _combine_kernel(x_ref, ya_ref, yb_ref, route_ref, g_ref, lng_ref, lnb_ref, o_ref):
    route = route_ref[...]
    gate_a = route[:, ROUTE_GATE:ROUTE_GATE + 1]
    gate_b = route[:, ROUTE_GATE + 1:ROUTE_GATE + 2]
    a_hi, a_lo = _unpack_halves_f32(ya_ref[...])
    b_hi, b_lo = _unpack_halves_f32(yb_ref[...])
    f = jnp.concatenate([a_hi * gate_a + b_hi * gate_b, a_lo * gate_a + b_lo * gate_b], axis=1)
    z = DEEPNORM_ALPHA * x_ref[...] + g_ref[0] * f
    o_ref[...] = _layer_norm(z, lng_ref[...], lnb_ref[...])


def _combine_latent_kernel(*refs):
    @pl.when(pl.program_id(0) % TILES_PER_BATCH < LAT_TILES)
    def _():
        _combine_kernel(*refs)


def _combine(x1, ya, yb, route, g2, lng, lnb, latent_only):
    n_rows = x1.shape[0]
    n_batch = g2.shape[0] - 1
    row = pl.BlockSpec((TM, D), lambda i: (i, 0))
    packed = pl.BlockSpec((TM, D // 2), lambda i: (i, 0))
    const = pl.BlockSpec((1, D), lambda i: (0, 0))
    if latent_only:
        out_rows = n_batch * L
        out_spec = pl.BlockSpec((TM, D), lambda i: (
            i // TILES_PER_BATCH * LAT_TILES + jnp.minimum(i % TILES_PER_BATCH, LAT_TILES - 1), 0))
    else:
        out_rows, out_spec = n_rows, row
    return pl.pallas_call(
        _combine_latent_kernel if latent_only else _combine_kernel,
        grid=(n_rows // TM,),
        in_specs=[row, packed, packed,
                  pl.BlockSpec((TM, ROUTER_PAD), lambda i: (i, 0)),
                  pl.BlockSpec((1, 1, D), lambda i: (_mod_group(i, n_batch), 0, 0)),
                  const, const],
        out_specs=out_spec,
        out_shape=jax.ShapeDtypeStruct((out_rows, D), F32),
        **_opts(1, flops=10 * n_rows * D, nbytes=16 * n_rows * D),
        name="moe_combine",
    )(x1, ya, yb, route, g2, lng, lnb)


def _dispatch_plan(route, counts):
    n = route.shape[0]
    a = n * MOE_TOP_K
    experts = jnp.arange(MOE_EXPERTS, dtype=jnp.int32)
    counts = counts[0, MOE_GROUPS:MOE_GROUPS + MOE_EXPERTS].astype(jnp.int32)
    padded = (counts + MOE_BLOCK - 1) // MOE_BLOCK * MOE_BLOCK
    pad_end = jnp.cumsum(padded)
    pad_start = pad_end - padded
    expert = route[:, ROUTE_EXPERT:ROUTE_EXPERT + MOE_TOP_K].astype(jnp.int32)
    rank = route[:, ROUTE_RANK:ROUTE_RANK + MOE_TOP_K].astype(jnp.int32)
    slot = jnp.sum(jnp.where(expert[:, :, None] == experts, pad_start, 0), axis=-1) + rank
    n_blocks = -(-a // MOE_BLOCK) + MOE_EXPERTS
    tok_of = jnp.broadcast_to(jnp.arange(n, dtype=jnp.int32)[:, None], (n, MOE_TOP_K))
    tok = (jnp.arange(n_blocks * MOE_BLOCK, dtype=jnp.int32) % n).at[slot.reshape(a)].set(
        tok_of.reshape(a), unique_indices=True, mode='promise_in_bounds')
    blk_start = jnp.arange(n_blocks, dtype=jnp.int32) * MOE_BLOCK
    blk_expert = jnp.minimum(jnp.sum((pad_end[None, :] <= blk_start[:, None]).astype(jnp.int32), axis=-1),
                             MOE_EXPERTS - 1)
    n_used = (pad_end[-1] // MOE_BLOCK).reshape(1)
    return slot, tok, blk_expert, n_used


def _take_rows(a, idx):
    return a.at[idx].get(mode='promise_in_bounds')


def _moe_layer(x1, u2, route, counts, g2, lng, lnb, w13, w2, latent_only):
    slot, tok, blk_expert, n_used = _dispatch_plan(route, counts)
    ys = _moe_experts(_take_rows(u2, tok), blk_expert, n_used, w13, w2)
    return _combine(x1, _take_rows(ys, slot[:, 0]), _take_rows(ys, slot[:, 1]), route, g2, lng, lnb, latent_only)


def _rope_tables():
    t = jnp.arange(L, dtype=jnp.int32)
    pos = jnp.stack([t // GRID_W, t % GRID_W], -1).astype(F32)
    n_freq = HEAD_DIM // 4
    inv_freq = ROPE_THETA ** (-jnp.arange(n_freq, dtype=F32) / n_freq)
    ang = pos[:, :, None] * inv_freq
    cos, sin = jnp.cos(ang), jnp.sin(ang)
    cos_t = jnp.stack([cos, cos], axis=2).reshape(L, HEAD_DIM)
    sin_t = jnp.stack([-sin, sin], axis=2).reshape(L, HEAD_DIM)
    cos_t = jnp.concatenate([cos_t, jnp.ones((C, HEAD_DIM), F32)], axis=0)
    sin_t = jnp.concatenate([sin_t, jnp.zeros((C, HEAD_DIM), F32)], axis=0)
    return cos_t, sin_t


def _router_params(rg_w, rg_b, re_w, re_b):
    n = MOE_GROUPS + MOE_EXPERTS
    w = jnp.concatenate([rg_w, re_w, jnp.zeros((D, ROUTER_PAD - n), F32)], axis=1).astype(BF16)
    b = jnp.concatenate([rg_b, re_b, jnp.zeros((ROUTER_PAD - n,), F32)]).reshape(1, ROUTER_PAD)
    return w, b


def _modulation_inputs(c, c_ctx):
    n = c.shape[0] + 1
    pad = -n % 8
    return jnp.concatenate([c, c_ctx[None, :], jnp.zeros((pad, D), F32)], axis=0)


def _mixer_fn(idx, mixer):
    kind = idx % 4
    if kind == 0:
        w_qkv, rpb, w_o = mixer
        w_o, bias = w_o.astype(BF16), _na_bias_table(rpb)

        def run(xs, sc1, sh1, n_batch, post):
            qkv = _proj(xs, sc1, sh1, w_qkv, n_out=3 * D, tn=D, out_dtype=BF16)
            return _out_proj(_na_attention(qkv, bias, n_batch), w_o, *post)
    elif kind == 1:
        w_in, b_in, dw, dw_b, cg, cb, w_out, b_out = mixer
        w_out = w_out.astype(BF16)

        def run(xs, sc1, sh1, n_batch, post):
            h = _proj(xs, sc1, sh1, w_in, n_out=D, tn=D // 2, out_dtype=F32, bias=b_in, glu=True)
            return _conv_mixer(h, dw, dw_b, cg, cb, w_out, b_out, *post)
    elif kind == 2:
        w_qkv, sink, w_o = mixer
        n_qk = (SWA_KV_HEADS * SWA_REP + SWA_KV_HEADS) * HEAD_DIM
        n_v = SWA_KV_HEADS * HEAD_DIM
        w_o, rope, mask = w_o.astype(BF16), _rope_tables(), _swa_mask_table()

        def run(xs, sc1, sh1, n_batch, post):
            qk = _proj(xs, sc1, sh1, w_qkv, n_out=n_qk, tn=n_qk, out_dtype=BF16, rope=rope)
            v = _proj(xs, sc1, sh1, w_qkv, n_out=n_v, tn=n_v, col_off=n_qk // n_v, out_dtype=BF16)
            return _out_proj(_swa_attention(qk, v, sink, mask, n_batch), w_o, *post)
    else:
        w_qkv, lam, subln_g, w_o = mixer
        lambda_init = 0.8 - 0.6 * math.exp(-0.3 * idx)
        w_o, rope = w_o.astype(BF16), _rope_tables()

        def run(xs, sc1, sh1, n_batch, post):
            qk = _proj(xs, sc1, sh1, w_qkv, n_out=2 * D, tn=D, out_dtype=BF16, rope=rope)
            v = _proj(xs, sc1, sh1, w_qkv, n_out=D, tn=D, col_off=2, out_dtype=BF16)
            return _out_proj(_diff_attention(qk, v, lam, subln_g, lambda_init, n_batch), w_o, *post)
    return run


def _hybrid_layer(idx, streams, cvec, mod_w, mod_b, mixer, ln1_g, ln1_b, moe, ln2_g, ln2_b, last=False):
    vec = lambda a: a.reshape(1, D)
    m_all = _adaln(cvec, mod_w, mod_b)
    ctx_row = sum(nb for _, _, nb in streams)
    rg_w, rg_b, re_w, re_b, w13, w2 = moe
    wr, br = _router_params(rg_w, rg_b, re_w, re_b)
    run_mixer = _mixer_fn(idx, mixer)
    out = []
    for xs, b0, nb in streams:
        m = jnp.concatenate([m_all[b0:b0 + nb], m_all[ctx_row:ctx_row + 1]], axis=0)
        sh1, sc1, g1, sh2, sc2, g2 = [m[:, None, k * D:(k + 1) * D] for k in range(6)]
        post = (xs, g1, vec(ln1_g), vec(ln1_b), sc2, sh2, wr, br)
        routed = run_mixer(xs, sc1, sh1, nb, post)
        out.append((_moe_layer(*routed, g2, vec(ln2_g), vec(ln2_b), w13, w2, latent_only=last), b0, nb))
    return out


def kernel(x, c, ctx, c_ctx, l0_mod_w, l0_mod_b, l0_na_w_qkv, l0_na_rpb, l0_na_w_o, l0_ln1_g, l0_ln1_b, l0_router_g_w, l0_router_g_b, l0_router_e_w, l0_router_e_b, l0_moe_w13, l0_moe_w2, l0_ln2_g, l0_ln2_b, l1_mod_w, l1_mod_b, l1_cv_w_in, l1_cv_b_in, l1_cv_dw, l1_cv_dw_b, l1_cv_ln_g, l1_cv_ln_b, l1_cv_w_out, l1_cv_b_out, l1_ln1_g, l1_ln1_b, l1_router_g_w, l1_router_g_b, l1_router_e_w, l1_router_e_b, l1_moe_w13, l1_moe_w2, l1_ln2_g, l1_ln2_b, l2_mod_w, l2_mod_b, l2_sw_w_qkv, l2_sw_sink, l2_sw_w_o, l2_ln1_g, l2_ln1_b, l2_router_g_w, l2_router_g_b, l2_router_e_w, l2_router_e_b, l2_moe_w13, l2_moe_w2, l2_ln2_g, l2_ln2_b, l3_mod_w, l3_mod_b, l3_df_w_qkv, l3_df_lambda, l3_df_subln_g, l3_df_w_o, l3_ln1_g, l3_ln1_b, l3_router_g_w, l3_router_g_b, l3_router_e_w, l3_router_e_b, l3_moe_w13, l3_moe_w2, l3_ln2_g, l3_ln2_b):
    layers = (
        (l0_mod_w, l0_mod_b, (l0_na_w_qkv, l0_na_rpb, l0_na_w_o), l0_ln1_g, l0_ln1_b,
         (l0_router_g_w, l0_router_g_b, l0_router_e_w, l0_router_e_b, l0_moe_w13, l0_moe_w2), l0_ln2_g, l0_ln2_b),
        (l1_mod_w, l1_mod_b, (l1_cv_w_in, l1_cv_b_in, l1_cv_dw, l1_cv_dw_b, l1_cv_ln_g, l1_cv_ln_b, l1_cv_w_out,
                              l1_cv_b_out), l1_ln1_g, l1_ln1_b,
         (l1_router_g_w, l1_router_g_b, l1_router_e_w, l1_router_e_b, l1_moe_w13, l1_moe_w2), l1_ln2_g, l1_ln2_b),
        (l2_mod_w, l2_mod_b, (l2_sw_w_qkv, l2_sw_sink, l2_sw_w_o), l2_ln1_g, l2_ln1_b,
         (l2_router_g_w, l2_router_g_b, l2_router_e_w, l2_router_e_b, l2_moe_w13, l2_moe_w2), l2_ln2_g, l2_ln2_b),
        (l3_mod_w, l3_mod_b, (l3_df_w_qkv, l3_df_lambda, l3_df_subln_g, l3_df_w_o), l3_ln1_g, l3_ln1_b,
         (l3_router_g_w, l3_router_g_b, l3_router_e_w, l3_router_e_b, l3_moe_w13, l3_moe_w2), l3_ln2_g, l3_ln2_b),
    )
    n_batch = x.shape[0]
    assert x.shape[1:] == (L, D) and ctx.shape[1:] == (C, D)
    n_streams = N_STREAMS if n_batch % N_STREAMS == 0 else 1
    nb = n_batch // n_streams
    streams = [(jnp.concatenate([x[b0:b0 + nb], ctx[b0:b0 + nb]], axis=1).reshape(nb * S, D), b0, nb)
               for b0 in range(0, n_batch, nb)]
    cvec = _modulation_inputs(c, c_ctx)
    for idx in range(DEPTH):
        streams = _hybrid_layer(idx, streams, cvec, *layers[idx], last=idx == DEPTH - 1)
    out = [xs.reshape(nb, L, D) for xs, _, _ in streams]
    return out[0] if len(out) == 1 else jnp.concatenate(out, axis=0)
```

```python
import functools
import math

import jax
import jax.numpy as jnp
from jax import lax
from jax.experimental import pallas as pl
from jax.experimental.pallas import tpu as pltpu

D = 2048
L = 2048
C = 256
S = L + C
DEPTH = 4
GRID_W = 64
HEAD_DIM = 128
ROPE_THETA = 10000.0
LN_EPS = 1e-5
NEG_INF = -1e30
DEEPNORM_ALPHA = (2.0 * DEPTH) ** 0.25
NA_HEADS = 16
NA_KH = 8
NA_KW = 16
NA_QROWS = 4
NA_KROWS = 12
CONV_WIDTH = 31
CONV_HALO = 16
SWA_KV_HEADS = 4
SWA_REP = 4
SWA_WINDOW = 128
SWA_BLOCK = 128
DIFF_HEADS = 8
MOE_GROUPS = 4
MOE_EPG = 8
MOE_EXPERTS = 32
MOE_TOP_K = 2
MOE_D_FF = 512
MOE_BLOCK = 256
ROUTER_PAD = 128

TM = 256
TILES_PER_BATCH = S // TM
LAT_TILES = L // TM
N_STREAMS = 1
VMEM_LIMIT = 52 * 1024 * 1024
SCALE = HEAD_DIM ** -0.5
LOG2E = math.log2(math.e)

F32 = jnp.float32
BF16 = jnp.bfloat16


def _opts(n_axes, *, flops, nbytes, transcendentals=0):
    return dict(
        compiler_params=pltpu.CompilerParams(dimension_semantics=("arbitrary",) * n_axes,
                                             vmem_limit_bytes=VMEM_LIMIT),
        cost_estimate=pl.CostEstimate(flops=int(flops), transcendentals=int(transcendentals),
                                      bytes_accessed=int(nbytes)))


def _dot(a, b):
    return jnp.dot(a, b, preferred_element_type=F32)


def _dot_t(a, b):
    return lax.dot_general(a, b, (((1,), (1,)), ((), ())), preferred_element_type=F32)


def _sigmoid(x):
    return 1.0 / (1.0 + jnp.exp(-x))


def _layer_norm(z, g, b):
    mu = jnp.mean(z, axis=-1, keepdims=True)
    zc = z - mu
    var = jnp.mean(zc * zc, axis=-1, keepdims=True)
    return zc * lax.rsqrt(var + LN_EPS) * g + b


def _mod_group(i, n_batch):
    return jnp.where(i % TILES_PER_BATCH == TILES_PER_BATCH - 1, n_batch, i // TILES_PER_BATCH)


def _adaln_kernel(c_ref, w_ref, b_ref, o_ref):
    c = c_ref[...]
    s = c * _sigmoid(c)
    o_ref[...] = _dot(s.astype(BF16), w_ref[...].astype(BF16)) + b_ref[...]


def _adaln(cvec, w, b):
    r = cvec.shape[0]
    n = w.shape[1]
    tn = 1024
    return pl.pallas_call(
        _adaln_kernel,
        grid=(n // tn,),
        in_specs=[pl.BlockSpec((r, D), lambda j: (0, 0)),
                  pl.BlockSpec((D, tn), lambda j: (0, j)),
                  pl.BlockSpec((1, tn), lambda j: (0, j))],
        out_specs=pl.BlockSpec((r, tn), lambda j: (0, j)),
        out_shape=jax.ShapeDtypeStruct((r, n), F32),
        **_opts(1, flops=2 * r * D * n, nbytes=4 * D * n),
        name="adaln",
    )(cvec, w, b.reshape(1, n))


def _rope_rotate(y, cos, sin):
    lane = lax.broadcasted_iota(jnp.int32, y.shape, 1)
    partner = jnp.where(lane % 64 < 32, pltpu.roll(y, 96, 1), pltpu.roll(y, 32, 1))
    return y * cos + partner * sin


def _proj_kernel(*refs, has_bias, glu, rope, tn):
    it = iter(refs)
    x_ref, sc_ref, sh_ref, w_ref = next(it), next(it), next(it), next(it)
    wg_ref = next(it) if glu else None
    b_ref = next(it) if has_bias else None
    bg_ref = next(it) if glu else None
    cos_ref = next(it) if rope else None
    sin_ref = next(it) if rope else None
    o_ref = next(it)
    wb_ref = next(it)
    wgb_ref = next(it) if glu else None

    @pl.when(pl.program_id(1) == 0)
    def _():
        wb_ref[...] = w_ref[...].astype(BF16)
        if glu:
            wgb_ref[...] = wg_ref[...].astype(BF16)

    u = (x_ref[...] * (1.0 + sc_ref[0]) + sh_ref[0]).astype(BF16)
    y = _dot(u, wb_ref[...])
    if has_bias:
        y = y + b_ref[...]
    if glu:
        y = y * _sigmoid(_dot(u, wgb_ref[...]) + bg_ref[...])
    if rope:
        cos = cos_ref[...]
        sin = sin_ref[...]
        for h in range(tn // HEAD_DIM):
            sl = slice(h * HEAD_DIM, (h + 1) * HEAD_DIM)
            o_ref[:, sl] = _rope_rotate(y[:, sl], cos, sin).astype(o_ref.dtype)
    else:
        o_ref[...] = y.astype(o_ref.dtype)


def _proj(x, sc, sh, w, *, n_out, tn, out_dtype, col_off=0, bias=None, glu=False, rope=None):
    n_rows = x.shape[0]
    n_batch = sc.shape[0] - 1
    n_tiles = n_rows // TM
    n_col = n_out // tn
    grp = lambda j, i: (_mod_group(i, n_batch), 0, 0)
    w_spec = lambda off: pl.BlockSpec((D, tn), lambda j, i: (0, j + off), pipeline_mode=pl.Buffered(1))
    in_specs = [pl.BlockSpec((TM, D), lambda j, i: (i, 0)),
                pl.BlockSpec((1, 1, D), grp),
                pl.BlockSpec((1, 1, D), grp),
                w_spec(col_off)]
    args = [x, sc, sh, w]
    scratch = [pltpu.VMEM((D, tn), BF16)]
    if glu:
        in_specs.append(w_spec(col_off + n_col))
        args.append(w)
        scratch.append(pltpu.VMEM((D, tn), BF16))
    if bias is not None:
        b2 = bias.reshape(1, -1)
        in_specs.append(pl.BlockSpec((1, tn), lambda j, i: (0, j + col_off)))
        args.append(b2)
        if glu:
            in_specs.append(pl.BlockSpec((1, tn), lambda j, i: (0, j + col_off + n_col)))
            args.append(b2)
    if rope is not None:
        rope_spec = pl.BlockSpec((TM, HEAD_DIM), lambda j, i: (i % TILES_PER_BATCH, 0))
        in_specs += [rope_spec, rope_spec]
        args += [rope[0], rope[1]]
    return pl.pallas_call(
        functools.partial(_proj_kernel, has_bias=bias is not None, glu=glu, rope=rope is not None, tn=tn),
        grid=(n_col, n_tiles),
        in_specs=in_specs,
        out_specs=pl.BlockSpec((TM, tn), lambda j, i: (i, j)),
        out_shape=jax.ShapeDtypeStruct((n_rows, n_out), out_dtype),
        scratch_shapes=scratch,
        **_opts(2, flops=2 * n_rows * D * n_out * (2 if glu else 1),
                nbytes=4 * n_rows * D * n_col + 4 * D * n_out * (2 if glu else 1) + 4 * n_rows * n_out),
        name="proj",
    )(*args)


ROUTE_EXPERT, ROUTE_GATE, ROUTE_RANK = 0, 2, 4


def _first_lane_where(cond, lane):
    return jnp.min(jnp.where(cond, lane, ROUTER_PAD), axis=-1, keepdims=True)


def _route_tile(lg, cnt_ref, counted):
    lane = lax.broadcasted_iota(jnp.int32, lg.shape, 1)
    gmask = lane < MOE_GROUPS
    gl = jnp.where(gmask, lg, NEG_INF)
    ge = jnp.exp(gl - gl.max(axis=-1, keepdims=True))
    gp = ge / ge.sum(axis=-1, keepdims=True)
    g_p = gp.max(axis=-1, keepdims=True)
    g_idx = _first_lane_where(gmask & (gp == g_p), lane)
    lo = MOE_GROUPS + MOE_EPG * g_idx
    emask = (lane >= lo) & (lane < lo + MOE_EPG)
    el = jnp.where(emask, lg, NEG_INF)
    ee = jnp.exp(el - el.max(axis=-1, keepdims=True))
    ep = jnp.where(emask, ee / ee.sum(axis=-1, keepdims=True), -1.0)
    p1 = ep.max(axis=-1, keepdims=True)
    i1 = _first_lane_where(ep == p1, lane)
    ep2 = jnp.where(lane == i1, -1.0, ep)
    p2 = ep2.max(axis=-1, keepdims=True)
    i2 = _first_lane_where(ep2 == p2, lane)
    den = p1 + p2
    gate1 = g_p * p1 / den
    gate2 = g_p * p2 / den
    oh1 = jnp.where(lane == i1, 1.0, 0.0)
    oh2 = jnp.where(lane == i2, 1.0, 0.0)
    n = lg.shape[0]
    tri = jnp.where(lax.broadcasted_iota(jnp.int32, (n, n), 0) > lax.broadcasted_iota(jnp.int32, (n, n), 1),
                    1.0, 0.0).astype(BF16)
    base = cnt_ref[...]
    tot1 = oh1.sum(axis=0, keepdims=True)
    pre1 = _dot(tri, oh1.astype(BF16)) + base
    pre2 = _dot(tri, oh2.astype(BF16)) + (base + tot1)
    rank1 = (oh1 * pre1).sum(axis=-1, keepdims=True)
    rank2 = (oh2 * pre2).sum(axis=-1, keepdims=True)
    cnt_ref[...] = base + jnp.where(counted, tot1 + oh2.sum(axis=0, keepdims=True), 0.0)
    cols = ((i1 - MOE_GROUPS).astype(F32), (i2 - MOE_GROUPS).astype(F32), gate1, gate2, rank1, rank2)
    route = jnp.zeros(lg.shape, F32)
    for k, col in enumerate(cols):
        route = jnp.where(lane == k, col, route)
    return route


def _pack_bf16_halves(u):
    bits = pltpu.bitcast(u.astype(F32), jnp.uint32)
    half = u.shape[1] // 2
    return bits[:, :half] | (bits[:, half:] >> 16)


def _unpack_halves_f32(p):
    return pltpu.bitcast(p & jnp.uint32(0xFFFF0000), F32), pltpu.bitcast(p << 16, F32)


def _unpack_bf16_halves(p):
    hi, lo = _unpack_halves_f32(p)
    return hi.astype(BF16), lo.astype(BF16)


def _post_mixer(y_rows, x_ref, g_ref, lng_ref, lnb_ref, sc2_ref, sh2_ref, wr_ref, br_ref,
                x1_ref, u2_ref, route_ref, cnt_out_ref, cnt_ref, counted=True, n_chunks=1, before_chunk=None):
    logits = []
    for c in range(n_chunks):
        if before_chunk is not None:
            before_chunk(c)
        rows = slice(c * TM // n_chunks, (c + 1) * TM // n_chunks)
        z = DEEPNORM_ALPHA * x_ref[rows, :] + g_ref[0] * y_rows(rows)
        x1 = _layer_norm(z, lng_ref[...], lnb_ref[...])
        x1_ref[rows, :] = x1
        u2 = (x1 * (1.0 + sc2_ref[0]) + sh2_ref[0]).astype(BF16)
        u2_ref[rows, :] = _pack_bf16_halves(u2)
        logits.append(_dot(u2, wr_ref[...]) + br_ref[...])
    logits = logits[0] if n_chunks == 1 else jnp.concatenate(logits, axis=0)
    route_ref[...] = _route_tile(logits, cnt_ref, counted)
    cnt_out_ref[...] = jnp.broadcast_to(cnt_ref[...], cnt_out_ref.shape)


def _post_mixer_specs(n_batch, lag=0):
    tile = lambda i: jnp.maximum(i - lag, 0)
    grp = lambda i: (_mod_group(tile(i), n_batch), 0, 0)
    row = lambda i: (tile(i), 0)
    const = lambda i: (0, 0)
    in_specs = [pl.BlockSpec((TM, D), row),
                pl.BlockSpec((1, 1, D), grp),
                pl.BlockSpec((1, D), const),
                pl.BlockSpec((1, D), const),
                pl.BlockSpec((1, 1, D), grp),
                pl.BlockSpec((1, 1, D), grp),
                pl.BlockSpec((D, ROUTER_PAD), const),
                pl.BlockSpec((1, ROUTER_PAD), const)]
    out_specs = [pl.BlockSpec((TM, D), row),
                 pl.BlockSpec((TM, D // 2), row),
                 pl.BlockSpec((TM, ROUTER_PAD), row),
                 pl.BlockSpec((8, ROUTER_PAD), const)]
    return in_specs, out_specs


def _post_mixer_out_shape(n_rows):
    return [jax.ShapeDtypeStruct((n_rows, D), F32),
            jax.ShapeDtypeStruct((n_rows, D // 2), jnp.uint32),
            jax.ShapeDtypeStruct((n_rows, ROUTER_PAD), F32),
            jax.ShapeDtypeStruct((8, ROUTER_PAD), F32)]


def _post_mixer_scratch():
    return [pltpu.VMEM((1, ROUTER_PAD), F32)]


OUT_PROJ_CHUNKS = 4


def _out_proj_kernel(o_ref, wo_ref, *rest):
    cnt_ref, y_even_ref, y_odd_ref = rest[-3:]
    i = pl.program_id(0)

    @pl.when(i == 0)
    def _():
        cnt_ref[...] = jnp.zeros(cnt_ref.shape, F32)
        y_odd_ref[...] = jnp.zeros(y_odd_ref.shape, F32)

    def step(cur_ref, prev_ref):
        def project(c):
            cols = slice(c * D // OUT_PROJ_CHUNKS, (c + 1) * D // OUT_PROJ_CHUNKS)
            cur_ref[:, cols] = _dot(o_ref[...], wo_ref[:, cols])

        _post_mixer(lambda rows: prev_ref[rows, :], *rest[:-2], counted=i > 0,
                    n_chunks=OUT_PROJ_CHUNKS, before_chunk=project)

    @pl.when(i % 2 == 0)
    def _():
        step(y_even_ref, y_odd_ref)

    @pl.when(i % 2 == 1)
    def _():
        step(y_odd_ref, y_even_ref)


def _out_proj(o, wo, x, g1, lng, lnb, sc2, sh2, wr, br):
    n_rows = x.shape[0]
    n_batch = g1.shape[0] - 1
    n_tiles = n_rows // TM
    pm_in, pm_out = _post_mixer_specs(n_batch, lag=1)
    return pl.pallas_call(
        _out_proj_kernel,
        grid=(n_tiles + 1,),
        in_specs=[pl.BlockSpec((TM, D), lambda i: (jnp.minimum(i, n_tiles - 1), 0)),
                  pl.BlockSpec((D, D), lambda i: (0, 0))] + pm_in,
        out_specs=pm_out,
        out_shape=_post_mixer_out_shape(n_rows),
        scratch_shapes=_post_mixer_scratch() + [pltpu.VMEM((TM, D), F32), pltpu.VMEM((TM, D), F32)],
        **_opts(1, flops=2 * n_rows * D * D, nbytes=12 * n_rows * D + 2 * D * D),
        name="out_proj",
    )(o, wo, x, g1, lng, lnb, sc2, sh2, wr, br)


def _na_key_row_start(j, rows):
    return jnp.clip(NA_QROWS * j - NA_KH // 2, 0, rows - NA_KROWS)


def _na_bias_table(rpb):
    rows = L // GRID_W
    n_blocks = rows // NA_QROWS
    j = jnp.array([0, 1, n_blocks - 1])
    qr = (NA_QROWS * j)[:, None] + jnp.arange(NA_QROWS)[None, :]
    kr = _na_key_row_start(j, rows)[:, None] + jnp.arange(NA_KROWS)[None, :]
    r0 = jnp.clip(qr - NA_KH // 2, 0, rows - NA_KH)
    row_ok = (kr[:, None, :] >= r0[:, :, None]) & (kr[:, None, :] < r0[:, :, None] + NA_KH)
    dr = jnp.clip(kr[:, None, :] - qr[:, :, None] + NA_KH - 1, 0, 2 * NA_KH - 2)
    cols = jnp.arange(GRID_W)
    col_start = jnp.clip(cols - NA_KW // 2, 0, GRID_W - NA_KW)
    col_ok = (cols[None, :] >= col_start[:, None]) & (cols[None, :] < col_start[:, None] + NA_KW)
    dc = jnp.clip(cols[None, :] - cols[:, None] + NA_KW - 1, 0, 2 * NA_KW - 2)
    by_row = jnp.where(col_ok, rpb[:, :, dc] * (1.0 / SCALE), NEG_INF)
    bias = jnp.where(row_ok[None, :, :, :, None, None], by_row[:, dr], NEG_INF)
    bias = jnp.transpose(bias, (0, 1, 2, 4, 3, 5))
    return bias.reshape(NA_HEADS, 3, NA_QROWS * GRID_W, NA_KROWS * GRID_W)


NA_GROUP = 8


def _na_kernel(q_ref, k_ref, v_ref, bias_ref, o_ref):
    rows = L // GRID_W
    qb = NA_QROWS * GRID_W
    kb = NA_KROWS * GRID_W
    n_blocks = rows // NA_QROWS
    kc = k_ref[L:S, :]
    vc = v_ref[L:S, :]

    def attend(blocks):
        scores = []
        for q_rows, k_rows, bias in blocks:
            q = q_ref[q_rows, :]
            s = _dot_t(q, kc)
            if k_rows is not None:
                s = jnp.concatenate([_dot_t(q, k_ref[k_rows, :]) + bias, s], axis=1)
            scores.append(s)
        probs = []
        for s in scores:
            p = jnp.exp2((s - s.max(axis=-1, keepdims=True)) * (SCALE * LOG2E))
            probs.append((p.astype(BF16), p.sum(axis=-1, keepdims=True)))
        for (q_rows, k_rows, _), (p, den) in zip(blocks, probs):
            if k_rows is None:
                o = _dot(p, vc)
            else:
                o = _dot(p[:, :kb], v_ref[k_rows, :]) + _dot(p[:, kb:], vc)
            o_ref[q_rows, :] = (o / den).astype(o_ref.dtype)

    def step(i, carry):
        blocks = []
        for u in range(NA_GROUP):
            j = i * NA_GROUP + u
            ks = pl.multiple_of(_na_key_row_start(j, rows) * GRID_W, GRID_W)
            kind = jnp.where(j == 0, 0, jnp.where(j == n_blocks - 1, 2, 1))
            blocks.append((pl.ds(pl.multiple_of(j * qb, qb), qb), pl.ds(ks, kb), bias_ref[0, kind]))
        attend(blocks)
        return carry

    lax.fori_loop(0, n_blocks // NA_GROUP, step, 0)
    attend([(slice(L, S), None, None)])


def _na_attention(qkv, bias, n_batch):
    h = NA_HEADS
    n_blocks = bias.shape[1]
    blk = lambda off: pl.BlockSpec((S, HEAD_DIM), lambda hh, b: (b, off + hh))
    return pl.pallas_call(
        _na_kernel,
        grid=(h, n_batch),
        in_specs=[blk(0), blk(h), blk(2 * h),
                  pl.BlockSpec((1, n_blocks) + bias.shape[2:], lambda hh, b: (hh, 0, 0, 0))],
        out_specs=pl.BlockSpec((S, HEAD_DIM), lambda hh, b: (b, hh)),
        out_shape=jax.ShapeDtypeStruct((n_batch * S, h * HEAD_DIM), BF16),
        **_opts(2, flops=4 * n_batch * h * S * (NA_KROWS * GRID_W + C) * HEAD_DIM,
                nbytes=8 * n_batch * S * h * HEAD_DIM,
                transcendentals=n_batch * h * S * (NA_KROWS * GRID_W + C)),
        name="na_attn",
    )(qkv, qkv, qkv, bias)


def _swa_mask_table():
    span = SWA_BLOCK + 2 * SWA_WINDOW
    qi = jnp.arange(SWA_BLOCK)[:, None]
    kj = jnp.arange(span)[None, :]
    band = jnp.abs(qi - (kj - SWA_WINDOW)) <= SWA_WINDOW
    first = band & (kj >= SWA_WINDOW)
    last = band & (kj < SWA_BLOCK + SWA_WINDOW)
    m = jnp.stack([first, band, last]).astype(F32)
    return jnp.where(m > 0, 0.0, NEG_INF).astype(F32)


SWA_GROUP = 4


def _swa_kernel(sink_ref, q_ref, k_ref, v_ref, mask_ref, o_ref, kpad_ref, vpad_ref):
    g = pl.program_id(1)
    nb = L // SWA_BLOCK
    span = SWA_BLOCK + 2 * SWA_WINDOW
    pad = SWA_WINDOW
    zeros = jnp.zeros((pad, HEAD_DIM), BF16)
    for src, dst in ((k_ref, kpad_ref), (v_ref, vpad_ref)):
        dst[0:pad, :] = zeros
        dst[pad + L:, :] = zeros
        dst[pad:pad + L, :] = src[0:L, :]
    kc = k_ref[L:S, :]
    vc = v_ref[L:S, :]

    def head(r):
        return slice(r * HEAD_DIM, (r + 1) * HEAD_DIM)

    def attend(blocks):
        scores = []
        for rows, keys, _, mask in blocks:
            for r in range(SWA_REP):
                q = q_ref[rows, head(r)]
                s = [_dot_t(q, k) for k in keys]
                if mask is not None:
                    s[0] = s[0] + mask
                scores.append(s[0] if len(s) == 1 else jnp.concatenate(s, axis=1))
        probs = []
        for i, s in enumerate(scores):
            sink_raw = sink_ref[g * SWA_REP + i % SWA_REP] * (1.0 / SCALE)
            m = jnp.maximum(s.max(axis=-1, keepdims=True), sink_raw)
            p = jnp.exp2((s - m) * (SCALE * LOG2E))
            den = p.sum(axis=-1, keepdims=True) + jnp.exp2((sink_raw - m) * (SCALE * LOG2E))
            probs.append((p.astype(BF16), den))
        for b, (rows, _, values, _) in enumerate(blocks):
            v = values[0] if len(values) == 1 else jnp.concatenate(values, axis=0)
            for r in range(SWA_REP):
                p, den = probs[b * SWA_REP + r]
                o_ref[rows, head(r)] = (_dot(p, v) / den).astype(o_ref.dtype)

    def step(i, carry):
        blocks = []
        for u in range(SWA_GROUP):
            n = i * SWA_GROUP + u
            start = pl.multiple_of(n * SWA_BLOCK, SWA_BLOCK)
            kind = jnp.where(n == 0, 0, jnp.where(n == nb - 1, 2, 1))
            blocks.append((pl.ds(start, SWA_BLOCK), [kpad_ref[pl.ds(start, span), :], kc],
                           [vpad_ref[pl.ds(start, span), :], vc], mask_ref[kind]))
        attend(blocks)
        return carry

    lax.fori_loop(0, nb // SWA_GROUP, step, 0)
    attend([(slice(L, S), [kc], [vc], None)])


def _swa_attention(qk, v, sink, mask, n_batch):
    qw = SWA_REP * HEAD_DIM
    n_q_blocks = SWA_KV_HEADS
    return pl.pallas_call(
        _swa_kernel,
        grid_spec=pltpu.PrefetchScalarGridSpec(
            num_scalar_prefetch=1,
            grid=(n_batch, SWA_KV_HEADS),
            in_specs=[pl.BlockSpec((S, qw), lambda b, g, sk: (b, g)),
                      pl.BlockSpec((S, HEAD_DIM), lambda b, g, sk: (b, n_q_blocks * SWA_REP + g)),
                      pl.BlockSpec((S, HEAD_DIM), lambda b, g, sk: (b, g)),
                      pl.BlockSpec(mask.shape, lambda b, g, sk: (0, 0, 0))],
            out_specs=pl.BlockSpec((S, qw), lambda b, g, sk: (b, g)),
            scratch_shapes=[pltpu.VMEM((L + 2 * SWA_WINDOW, HEAD_DIM), BF16),
                            pltpu.VMEM((L + 2 * SWA_WINDOW, HEAD_DIM), BF16)]),
        out_shape=jax.ShapeDtypeStruct((n_batch * S, SWA_KV_HEADS * qw), BF16),
        **_opts(2, flops=4 * n_batch * SWA_KV_HEADS * SWA_REP * S * (SWA_BLOCK + 2 * SWA_WINDOW + C) * HEAD_DIM,
                nbytes=2 * n_batch * S * (2 * SWA_KV_HEADS * qw + 2 * SWA_KV_HEADS * HEAD_DIM),
                transcendentals=n_batch * SWA_KV_HEADS * SWA_REP * S * (SWA_BLOCK + 2 * SWA_WINDOW + C)),
        name="swa_attn",
    )(sink, qk, qk, v, mask)


DIFF_GROUP = 4


def _diff_kernel(lam_ref, g_ref, q_ref, k_ref, v_ref, o_ref, *, lambda_init, tq):
    lam = lam_ref[...]
    lmbda = (jnp.exp(jnp.sum(lam[0:1] * lam[1:2], axis=-1, keepdims=True))
             - jnp.exp(jnp.sum(lam[2:3] * lam[3:4], axis=-1, keepdims=True)) + lambda_init)
    gain = g_ref[...] * (1.0 - lambda_init)

    def softmax_parts(s):
        m = s.max(axis=-1, keepdims=True)
        e = jnp.exp2((s - m) * (SCALE * LOG2E))
        return e, e.sum(axis=-1, keepdims=True)

    def attend(q_blocks, keys):
        scores = []
        for q_rows in q_blocks:
            q = q_ref[q_rows, :]
            scores.append((_dot_t(q[:, :HEAD_DIM], k_ref[keys, 0:HEAD_DIM]),
                           _dot_t(q[:, HEAD_DIM:], k_ref[keys, HEAD_DIM:2 * HEAD_DIM])))
        probs = []
        for s1, s2 in scores:
            e1, d1 = softmax_parts(s1)
            e2, d2 = softmax_parts(s2)
            probs.append((e1 * (1.0 / d1) - e2 * (lmbda / d2)).astype(BF16))
        for q_rows, pd in zip(q_blocks, probs):
            o = _dot(pd, v_ref[keys, :])
            o = o * lax.rsqrt(jnp.mean(o * o, axis=-1, keepdims=True) + LN_EPS) * gain
            o_ref[q_rows, :] = o.astype(o_ref.dtype)

    def step(i, carry):
        attend([pl.ds(pl.multiple_of((i * DIFF_GROUP + u) * tq, tq), tq) for u in range(DIFF_GROUP)], slice(0, S))
        return carry

    lax.fori_loop(0, L // (tq * DIFF_GROUP), step, 0)
    attend([slice(L, S)], slice(L, S))


def _diff_attention(qk, v, lam, subln_g, lambda_init, n_batch):
    hw = 2 * HEAD_DIM
    return pl.pallas_call(
        functools.partial(_diff_kernel, lambda_init=lambda_init, tq=256),
        grid=(n_batch, DIFF_HEADS),
        in_specs=[pl.BlockSpec((4, HEAD_DIM), lambda b, h: (0, 0)),
                  pl.BlockSpec((1, hw), lambda b, h: (0, 0)),
                  pl.BlockSpec((S, hw), lambda b, h: (b, h)),
                  pl.BlockSpec((S, hw), lambda b, h: (b, DIFF_HEADS + h)),
                  pl.BlockSpec((S, hw), lambda b, h: (b, h))],
        out_specs=pl.BlockSpec((S, hw), lambda b, h: (b, h)),
        out_shape=jax.ShapeDtypeStruct((n_batch * S, DIFF_HEADS * hw), BF16),
        **_opts(2, flops=8 * n_batch * DIFF_HEADS * S * S * HEAD_DIM, nbytes=8 * n_batch * S * DIFF_HEADS * hw,
                transcendentals=2 * n_batch * DIFF_HEADS * S * S),
        name="diff_attn",
    )(lam, subln_g.reshape(1, hw), qk, qk, v)


CONV_ROWS = 128


def _conv_kernel(hc_ref, hp_ref, hn_ref, dw_ref, dwb_ref, cg_ref, cb_ref, wout_ref, bout_ref, *rest):
    cnt_ref, win_ref, cv_ref = rest[-3:]

    @pl.when(pl.program_id(0) == 0)
    def _():
        cnt_ref[...] = jnp.zeros(cnt_ref.shape, F32)

    t = pl.program_id(0) % TILES_PER_BATCH
    first = (t == 0) | (t == TILES_PER_BATCH - 1)
    last = t >= TILES_PER_BATCH - 2
    win_ref[0:CONV_HALO, :] = jnp.where(first, 0.0, hp_ref[...])
    win_ref[CONV_HALO:CONV_HALO + TM, :] = hc_ref[...]
    win_ref[CONV_HALO + TM:, :] = jnp.where(last, 0.0, hn_ref[...])
    base = CONV_HALO - CONV_WIDTH // 2
    sub = 8

    def strip(c, carry):
        cs = pl.ds(pl.multiple_of(c * HEAD_DIM, HEAD_DIM), HEAD_DIM)
        for r in range(TM // CONV_ROWS):
            acc = jnp.zeros((CONV_ROWS, HEAD_DIM), F32)
            aligned = win_ref[pl.ds(r * CONV_ROWS, CONV_ROWS + 2 * CONV_HALO), cs]
            for res in range(sub):
                taps = [j for j in range(CONV_WIDTH) if (base + j) % sub == res]
                w = aligned if res == 0 else pltpu.roll(aligned, aligned.shape[0] - res, 0)
                for j in taps:
                    lo = (base + j) // sub * sub
                    acc = acc + dw_ref[j:j + 1, cs] * w[lo:lo + CONV_ROWS]
            cv_ref[r * CONV_ROWS:(r + 1) * CONV_ROWS, cs] = acc
        return carry

    lax.fori_loop(0, D // HEAD_DIM, strip, 0)

    hn = _layer_norm(cv_ref[...] + dwb_ref[...], cg_ref[...], cb_ref[...])
    hn = (hn * _sigmoid(hn)).astype(BF16)
    y = _dot(hn, wout_ref[...]) + bout_ref[...]
    _post_mixer(lambda rows: y[rows, :], *rest[:-2])


def _conv_mixer(h, dw, dw_b, cg, cb, wout, bout, x, g1, lng, lnb, sc2, sh2, wr, br):
    n_rows = x.shape[0]
    n_batch = g1.shape[0] - 1
    n_tiles = n_rows // TM
    halo_per_tile = TM // CONV_HALO
    n_halo = n_rows // CONV_HALO
    pm_in, pm_out = _post_mixer_specs(n_batch)
    const = lambda i: (0, 0)
    vec = lambda a: a.reshape(1, D)
    dw_pad = jnp.concatenate([dw, jnp.zeros((1, D), dw.dtype)], axis=0)
    return pl.pallas_call(
        _conv_kernel,
        grid=(n_tiles,),
        in_specs=[pl.BlockSpec((TM, D), lambda i: (i, 0)),
                  pl.BlockSpec((CONV_HALO, D), lambda i: (jnp.maximum(i * halo_per_tile - 1, 0), 0)),
                  pl.BlockSpec((CONV_HALO, D), lambda i: (jnp.minimum((i + 1) * halo_per_tile, n_halo - 1), 0)),
                  pl.BlockSpec((CONV_WIDTH + 1, D), const),
                  pl.BlockSpec((1, D), const),
                  pl.BlockSpec((1, D), const),
                  pl.BlockSpec((1, D), const),
                  pl.BlockSpec((D, D), const),
                  pl.BlockSpec((1, D), const)] + pm_in,
        out_specs=pm_out,
        out_shape=_post_mixer_out_shape(n_rows),
        scratch_shapes=_post_mixer_scratch() + [pltpu.VMEM((TM + 2 * CONV_HALO, D), F32),
                                                pltpu.VMEM((TM, D), F32)],
        **_opts(1, flops=2 * n_rows * D * (D + CONV_WIDTH), nbytes=14 * n_rows * D + 2 * D * D),
        name="conv_mixer",
    )(h, h, h, dw_pad, vec(dw_b), vec(cg), vec(cb), wout, vec(bout), x, g1, lng, lnb, sc2, sh2, wr, br)


def _moe_kernel(be_ref, nu_ref, xs_ref, w13_ref, w2_ref, ys_ref, w13b_ref, w2b_ref):
    i = pl.program_id(0)

    @pl.when(i >= nu_ref[0])
    def _():
        ys_ref[...] = jnp.zeros(ys_ref.shape, ys_ref.dtype)

    @pl.when(i < nu_ref[0])
    def _():
        changed = (i == 0) | (be_ref[i] != be_ref[jnp.maximum(i - 1, 0)])

        @pl.when(changed)
        def _():
            w13b_ref[...] = w13_ref[0].astype(BF16)
            w2b_ref[...] = w2_ref[0].astype(BF16)

        x_hi, x_lo = _unpack_bf16_halves(xs_ref[...])
        a = _dot(x_hi, w13b_ref[0:D // 2, :]) + _dot(x_lo, w13b_ref[D // 2:, :])
        gate = a[:, :MOE_D_FF]
        hmid = (gate * _sigmoid(gate) * a[:, MOE_D_FF:]).astype(BF16)
        ys_ref[...] = _pack_bf16_halves(_dot(hmid, w2b_ref[...]).astype(BF16))


def _moe_experts(xs, blk_expert, n_used, w13, w2):
    n_slots = xs.shape[0]
    n_blocks = n_slots // MOE_BLOCK
    live = lambda i, nu: jnp.minimum(i, nu[0] - 1)
    return pl.pallas_call(
        _moe_kernel,
        grid_spec=pltpu.PrefetchScalarGridSpec(
            num_scalar_prefetch=2,
            grid=(n_blocks,),
            in_specs=[pl.BlockSpec((MOE_BLOCK, D // 2), lambda i, be, nu: (live(i, nu), 0)),
                      pl.BlockSpec((1, D, 2 * MOE_D_FF), lambda i, be, nu: (be[live(i, nu)], 0, 0)),
                      pl.BlockSpec((1, MOE_D_FF, D), lambda i, be, nu: (be[live(i, nu)], 0, 0))],
            out_specs=pl.BlockSpec((MOE_BLOCK, D // 2), lambda i, be, nu: (i, 0)),
            scratch_shapes=[pltpu.VMEM((D, 2 * MOE_D_FF), BF16),
                            pltpu.VMEM((MOE_D_FF, D), BF16)]),
        out_shape=jax.ShapeDtypeStruct((n_slots, D // 2), jnp.uint32),
        **_opts(1, flops=6 * n_slots * D * MOE_D_FF, nbytes=4 * n_slots * D + 12 * MOE_EXPERTS * D * MOE_D_FF),
        name="moe_experts",
    )(blk_expert, n_used, xs, w13, w2)


def _combine_kernel(x_ref, ya_ref, yb_ref, route_ref, g_ref, lng_ref, lnb_ref, o_ref):
    route = route_ref[...]
    gate_a = route[:, ROUTE_GATE:ROUTE_GATE + 1]
    gate_b = route[:, ROUTE_GATE + 1:ROUTE_GATE + 2]
    a_hi, a_lo = _unpack_halves_f32(ya_ref[...])
    b_hi, b_lo = _unpack_halves_f32(yb_ref[...])
    f = jnp.concatenate([a_hi * gate_a + b_hi * gate_b, a_lo * gate_a + b_lo * gate_b], axis=1)
    z = DEEPNORM_ALPHA * x_ref[...] + g_ref[0] * f
    o_ref[...] = _layer_norm(z, lng_ref[...], lnb_ref[...])


def _combine_latent_kernel(*refs):
    @pl.when(pl.program_id(0) % TILES_PER_BATCH < LAT_TILES)
    def _():
        _combine_kernel(*refs)


def _combine(x1, ya, yb, route, g2, lng, lnb, latent_only):
    n_rows = x1.shape[0]
    n_batch = g2.shape[0] - 1
    row = pl.BlockSpec((TM, D), lambda i: (i, 0))
    packed = pl.BlockSpec((TM, D // 2), lambda i: (i, 0))
    const = pl.BlockSpec((1, D), lambda i: (0, 0))
    if latent_only:
        out_rows = n_batch * L
        out_spec = pl.BlockSpec((TM, D), lambda i: (
            i // TILES_PER_BATCH * LAT_TILES + jnp.minimum(i % TILES_PER_BATCH, LAT_TILES - 1), 0))
    else:
        out_rows, out_spec = n_rows, row
    return pl.pallas_call(
        _combine_latent_kernel if latent_only else _combine_kernel,
        grid=(n_rows // TM,),
        in_specs=[row, packed, packed,
                  pl.BlockSpec((TM, ROUTER_PAD), lambda i: (i, 0)),
                  pl.BlockSpec((1, 1, D), lambda i: (_mod_group(i, n_batch), 0, 0)),
                  const, const],
        out_specs=out_spec,
        out_shape=jax.ShapeDtypeStruct((out_rows, D), F32),
        **_opts(1, flops=10 * n_rows * D, nbytes=16 * n_rows * D),
        name="moe_combine",
    )(x1, ya, yb, route, g2, lng, lnb)


def _dispatch_plan(route, counts):
    n = route.shape[0]
    a = n * MOE_TOP_K
    experts = jnp.arange(MOE_EXPERTS, dtype=jnp.int32)
    counts = counts[0, MOE_GROUPS:MOE_GROUPS + MOE_EXPERTS].astype(jnp.int32)
    padded = (counts + MOE_BLOCK - 1) // MOE_BLOCK * MOE_BLOCK
    pad_end = jnp.cumsum(padded)
    pad_start = pad_end - padded
    expert = route[:, ROUTE_EXPERT:ROUTE_EXPERT + MOE_TOP_K].astype(jnp.int32)
    rank = route[:, ROUTE_RANK:ROUTE_RANK + MOE_TOP_K].astype(jnp.int32)
    slot = jnp.sum(jnp.where(expert[:, :, None] == experts, pad_start, 0), axis=-1) + rank
    n_blocks = -(-a // MOE_BLOCK) + MOE_EXPERTS
    tok_of = jnp.broadcast_to(jnp.arange(n, dtype=jnp.int32)[:, None], (n, MOE_TOP_K))
    tok = (jnp.arange(n_blocks * MOE_BLOCK, dtype=jnp.int32) % n).at[slot.reshape(a)].set(
        tok_of.reshape(a), unique_indices=True, mode='promise_in_bounds')
    blk_start = jnp.arange(n_blocks, dtype=jnp.int32) * MOE_BLOCK
    blk_expert = jnp.minimum(jnp.sum((pad_end[None, :] <= blk_start[:, None]).astype(jnp.int32), axis=-1),
                             MOE_EXPERTS - 1)
    n_used = (pad_end[-1] // MOE_BLOCK).reshape(1)
    return slot, tok, blk_expert, n_used


def _take_rows(a, idx):
    return a.at[idx].get(mode='promise_in_bounds')


def _moe_layer(x1, u2, route, counts, g2, lng, lnb, w13, w2, latent_only):
    slot, tok, blk_expert, n_used = _dispatch_plan(route, counts)
    ys = _moe_experts(_take_rows(u2, tok), blk_expert, n_used, w13, w2)
    return _combine(x1, _take_rows(ys, slot[:, 0]), _take_rows(ys, slot[:, 1]), route, g2, lng, lnb, latent_only)


def _rope_tables():
    t = jnp.arange(L, dtype=jnp.int32)
    pos = jnp.stack([t // GRID_W, t % GRID_W], -1).astype(F32)
    n_freq = HEAD_DIM // 4
    inv_freq = ROPE_THETA ** (-jnp.arange(n_freq, dtype=F32) / n_freq)
    ang = pos[:, :, None] * inv_freq
    cos, sin = jnp.cos(ang), jnp.sin(ang)
    cos_t = jnp.stack([cos, cos], axis=2).reshape(L, HEAD_DIM)
    sin_t = jnp.stack([-sin, sin], axis=2).reshape(L, HEAD_DIM)
    cos_t = jnp.concatenate([cos_t, jnp.ones((C, HEAD_DIM), F32)], axis=0)
    sin_t = jnp.concatenate([sin_t, jnp.zeros((C, HEAD_DIM), F32)], axis=0)
    return cos_t, sin_t


def _router_params(rg_w, rg_b, re_w, re_b):
    n = MOE_GROUPS + MOE_EXPERTS
    w = jnp.concatenate([rg_w, re_w, jnp.zeros((D, ROUTER_PAD - n), F32)], axis=1).astype(BF16)
    b = jnp.concatenate([rg_b, re_b, jnp.zeros((ROUTER_PAD - n,), F32)]).reshape(1, ROUTER_PAD)
    return w, b


def _modulation_inputs(c, c_ctx):
    n = c.shape[0] + 1
    pad = -n % 8
    return jnp.concatenate([c, c_ctx[None, :], jnp.zeros((pad, D), F32)], axis=0)


def _mixer_fn(idx, mixer):
    kind = idx % 4
    if kind == 0:
        w_qkv, rpb, w_o = mixer
        w_o, bias = w_o.astype(BF16), _na_bias_table(rpb)

        def run(xs, sc1, sh1, n_batch, post):
            qkv = _proj(xs, sc1, sh1, w_qkv, n_out=3 * D, tn=D, out_dtype=BF16)
            return _out_proj(_na_attention(qkv, bias, n_batch), w_o, *post)
    elif kind == 1:
        w_in, b_in, dw, dw_b, cg, cb, w_out, b_out = mixer
        w_out = w_out.astype(BF16)

        def run(xs, sc1, sh1, n_batch, post):
            h = _proj(xs, sc1, sh1, w_in, n_out=D, tn=D // 2, out_dtype=F32, bias=b_in, glu=True)
            return _conv_mixer(h, dw, dw_b, cg, cb, w_out, b_out, *post)
    elif kind == 2:
        w_qkv, sink, w_o = mixer
        n_qk = (SWA_KV_HEADS * SWA_REP + SWA_KV_HEADS) * HEAD_DIM
        n_v = SWA_KV_HEADS * HEAD_DIM
        w_o, rope, mask = w_o.astype(BF16), _rope_tables(), _swa_mask_table()

        def run(xs, sc1, sh1, n_batch, post):
            qk = _proj(xs, sc1, sh1, w_qkv, n_out=n_qk, tn=n_qk, out_dtype=BF16, rope=rope)
            v = _proj(xs, sc1, sh1, w_qkv, n_out=n_v, tn=n_v, col_off=n_qk // n_v, out_dtype=BF16)
            return _out_proj(_swa_attention(qk, v, sink, mask, n_batch), w_o, *post)
    else:
        w_qkv, lam, subln_g, w_o = mixer
        lambda_init = 0.8 - 0.6 * math.exp(-0.3 * idx)
        w_o, rope = w_o.astype(BF16), _rope_tables()

        def run(xs, sc1, sh1, n_batch, post):
            qk = _proj(xs, sc1, sh1, w_qkv, n_out=2 * D, tn=D, out_dtype=BF16, rope=rope)
            v = _proj(xs, sc1, sh1, w_qkv, n_out=D, tn=D, col_off=2, out_dtype=BF16)
            return _out_proj(_diff_attention(qk, v, lam, subln_g, lambda_init, n_batch), w_o, *post)
    return run


def _hybrid_layer(idx, streams, cvec, mod_w, mod_b, mixer, ln1_g, ln1_b, moe, ln2_g, ln2_b, last=False):
    vec = lambda a: a.reshape(1, D)
    m_all = _adaln(cvec, mod_w, mod_b)
    ctx_row = sum(nb for _, _, nb in streams)
    rg_w, rg_b, re_w, re_b, w13, w2 = moe
    wr, br = _router_params(rg_w, rg_b, re_w, re_b)
    run_mixer = _mixer_fn(idx, mixer)
    out = []
    for xs, b0, nb in streams:
        m = jnp.concatenate([m_all[b0:b0 + nb], m_all[ctx_row:ctx_row + 1]], axis=0)
        sh1, sc1, g1, sh2, sc2, g2 = [m[:, None, k * D:(k + 1) * D] for k in range(6)]
        post = (xs, g1, vec(ln1_g), vec(ln1_b), sc2, sh2, wr, br)
        routed = run_mixer(xs, sc1, sh1, nb, post)
        out.append((_moe_layer(*routed, g2, vec(ln2_g), vec(ln2_b), w13, w2, latent_only=last), b0, nb))
    return out


def kernel(x, c, ctx, c_ctx, l0_mod_w, l0_mod_b, l0_na_w_qkv, l0_na_rpb, l0_na_w_o, l0_ln1_g, l0_ln1_b, l0_router_g_w, l0_router_g_b, l0_router_e_w, l0_router_e_b, l0_moe_w13, l0_moe_w2, l0_ln2_g, l0_ln2_b, l1_mod_w, l1_mod_b, l1_cv_w_in, l1_cv_b_in, l1_cv_dw, l1_cv_dw_b, l1_cv_ln_g, l1_cv_ln_b, l1_cv_w_out, l1_cv_b_out, l1_ln1_g, l1_ln1_b, l1_router_g_w, l1_router_g_b, l1_router_e_w, l1_router_e_b, l1_moe_w13, l1_moe_w2, l1_ln2_g, l1_ln2_b, l2_mod_w, l2_mod_b, l2_sw_w_qkv, l2_sw_sink, l2_sw_w_o, l2_ln1_g, l2_ln1_b, l2_router_g_w, l2_router_g_b, l2_router_e_w, l2_router_e_b, l2_moe_w13, l2_moe_w2, l2_ln2_g, l2_ln2_b, l3_mod_w, l3_mod_b, l3_df_w_qkv, l3_df_lambda, l3_df_subln_g, l3_df_w_o, l3_ln1_g, l3_ln1_b, l3_router_g_w, l3_router_g_b, l3_router_e_w, l3_router_e_b, l3_moe_w13, l3_moe_w2, l3_ln2_g, l3_ln2_b):
    layers = (
        (l0_mod_w, l0_mod_b, (l0_na_w_qkv, l0_na_rpb, l0_na_w_o), l0_ln1_g, l0_ln1_b,
         (l0_router_g_w, l0_router_g_b, l0_router_e_w, l0_router_e_b, l0_moe_w13, l0_moe_w2), l0_ln2_g, l0_ln2_b),
        (l1_mod_w, l1_mod_b, (l1_cv_w_in, l1_cv_b_in, l1_cv_dw, l1_cv_dw_b, l1_cv_ln_g, l1_cv_ln_b, l1_cv_w_out,
                              l1_cv_b_out), l1_ln1_g, l1_ln1_b,
         (l1_router_g_w, l1_router_g_b, l1_router_e_w, l1_router_e_b, l1_moe_w13, l1_moe_w2), l1_ln2_g, l1_ln2_b),
        (l2_mod_w, l2_mod_b, (l2_sw_w_qkv, l2_sw_sink, l2_sw_w_o), l2_ln1_g, l2_ln1_b,
         (l2_router_g_w, l2_router_g_b, l2_router_e_w, l2_router_e_b, l2_moe_w13, l2_moe_w2), l2_ln2_g, l2_ln2_b),
        (l3_mod_w, l3_mod_b, (l3_df_w_qkv, l3_df_lambda, l3_df_subln_g, l3_df_w_o), l3_ln1_g, l3_ln1_b,
         (l3_router_g_w, l3_router_g_b, l3_router_e_w, l3_router_e_b, l3_moe_w13, l3_moe_w2), l3_ln2_g, l3_ln2_b),
    )
    n_batch = x.shape[0]
    assert x.shape[1:] == (L, D) and ctx.shape[1:] == (C, D)
    n_streams = N_STREAMS if n_batch % N_STREAMS == 0 else 1
    nb = n_batch // n_streams
    streams = [(jnp.concatenate([x[b0:b0 + nb], ctx[b0:b0 + nb]], axis=1).reshape(nb * S, D), b0, nb)
               for b0 in range(0, n_batch, nb)]
    cvec = _modulation_inputs(c, c_ctx)
    for idx in range(DEPTH):
        streams = _hybrid_layer(idx, streams, cvec, *layers[idx], last=idx == DEPTH - 1)
    out = [xs.reshape(nb, L, D) for xs, _, _ in streams]
    return out[0] if len(out) == 1 else jnp.concatenate(out, axis=0)
```

```python
import functools
import math

import jax
import jax.numpy as jnp
from jax import lax
from jax.experimental import pallas as pl
from jax.experimental.pallas import tpu as pltpu

D = 2048
L = 2048
C = 256
S = L + C
DEPTH = 4
GRID_W = 64
HEAD_DIM = 128
ROPE_THETA = 10000.0
LN_EPS = 1e-5
NEG_INF = -1e30
DEEPNORM_ALPHA = (2.0 * DEPTH) ** 0.25
NA_HEADS = 16
NA_KH = 8
NA_KW = 16
NA_QROWS = 4
NA_KROWS = 12
CONV_WIDTH = 31
CONV_HALO = 16
SWA_KV_HEADS = 4
SWA_REP = 4
SWA_WINDOW = 128
SWA_BLOCK = 128
DIFF_HEADS = 8
MOE_GROUPS = 4
MOE_EPG = 8
MOE_EXPERTS = 32
MOE_TOP_K = 2
MOE_D_FF = 512
MOE_BLOCK = 512
ROUTER_PAD = 128

TM = 256
TILES_PER_BATCH = S // TM
LAT_TILES = L // TM
N_STREAMS = 1
VMEM_LIMIT = 52 * 1024 * 1024
SCALE = HEAD_DIM ** -0.5
LOG2E = math.log2(math.e)

F32 = jnp.float32
BF16 = jnp.bfloat16


def _opts(n_axes, *, flops, nbytes, transcendentals=0):
    return dict(
        compiler_params=pltpu.CompilerParams(dimension_semantics=("arbitrary",) * n_axes,
                                             vmem_limit_bytes=VMEM_LIMIT),
        cost_estimate=pl.CostEstimate(flops=int(flops), transcendentals=int(transcendentals),
                                      bytes_accessed=int(nbytes)))


def _dot(a, b):
    return jnp.dot(a, b, preferred_element_type=F32)


def _dot_t(a, b):
    return lax.dot_general(a, b, (((1,), (1,)), ((), ())), preferred_element_type=F32)


def _sigmoid(x):
    return 1.0 / (1.0 + jnp.exp(-x))


def _layer_norm(z, g, b):
    mu = jnp.mean(z, axis=-1, keepdims=True)
    zc = z - mu
    var = jnp.mean(zc * zc, axis=-1, keepdims=True)
    return zc * lax.rsqrt(var + LN_EPS) * g + b


def _mod_group(i, n_batch):
    return jnp.where(i % TILES_PER_BATCH == TILES_PER_BATCH - 1, n_batch, i // TILES_PER_BATCH)


def _adaln_kernel(c_ref, w_ref, b_ref, o_ref):
    c = c_ref[...]
    s = c * _sigmoid(c)
    o_ref[...] = _dot(s.astype(BF16), w_ref[...].astype(BF16)) + b_ref[...]


def _adaln(cvec, w, b):
    r = cvec.shape[0]
    n = w.shape[1]
    tn = 1024
    return pl.pallas_call(
        _adaln_kernel,
        grid=(n // tn,),
        in_specs=[pl.BlockSpec((r, D), lambda j: (0, 0)),
                  pl.BlockSpec((D, tn), lambda j: (0, j)),
                  pl.BlockSpec((1, tn), lambda j: (0, j))],
        out_specs=pl.BlockSpec((r, tn), lambda j: (0, j)),
        out_shape=jax.ShapeDtypeStruct((r, n), F32),
        **_opts(1, flops=2 * r * D * n, nbytes=4 * D * n),
        name="adaln",
    )(cvec, w, b.reshape(1, n))


def _rope_rotate(y, cos, sin):
    lane = lax.broadcasted_iota(jnp.int32, y.shape, 1)
    partner = jnp.where(lane % 64 < 32, pltpu.roll(y, 96, 1), pltpu.roll(y, 32, 1))
    return y * cos + partner * sin


def _proj_kernel(*refs, has_bias, glu, rope, tn):
    it = iter(refs)
    x_ref, sc_ref, sh_ref, w_ref = next(it), next(it), next(it), next(it)
    wg_ref = next(it) if glu else None
    b_ref = next(it) if has_bias else None
    bg_ref = next(it) if glu else None
    cos_ref = next(it) if rope else None
    sin_ref = next(it) if rope else None
    o_ref = next(it)
    wb_ref = next(it)
    wgb_ref = next(it) if glu else None

    @pl.when(pl.program_id(1) == 0)
    def _():
        wb_ref[...] = w_ref[...].astype(BF16)
        if glu:
            wgb_ref[...] = wg_ref[...].astype(BF16)

    u = (x_ref[...] * (1.0 + sc_ref[0]) + sh_ref[0]).astype(BF16)
    y = _dot(u, wb_ref[...])
    if has_bias:
        y = y + b_ref[...]
    if glu:
        y = y * _sigmoid(_dot(u, wgb_ref[...]) + bg_ref[...])
    if rope:
        cos = cos_ref[...]
        sin = sin_ref[...]
        for h in range(tn // HEAD_DIM):
            sl = slice(h * HEAD_DIM, (h + 1) * HEAD_DIM)
            o_ref[:, sl] = _rope_rotate(y[:, sl], cos, sin).astype(o_ref.dtype)
    else:
        o_ref[...] = y.astype(o_ref.dtype)


def _proj(x, sc, sh, w, *, n_out, tn, out_dtype, col_off=0, bias=None, glu=False, rope=None):
    n_rows = x.shape[0]
    n_batch = sc.shape[0] - 1
    n_tiles = n_rows // TM
    n_col = n_out // tn
    grp = lambda j, i: (_mod_group(i, n_batch), 0, 0)
    w_spec = lambda off: pl.BlockSpec((D, tn), lambda j, i: (0, j + off), pipeline_mode=pl.Buffered(1))
    in_specs = [pl.BlockSpec((TM, D), lambda j, i: (i, 0)),
                pl.BlockSpec((1, 1, D), grp),
                pl.BlockSpec((1, 1, D), grp),
                w_spec(col_off)]
    args = [x, sc, sh, w]
    scratch = [pltpu.VMEM((D, tn), BF16)]
    if glu:
        in_specs.append(w_spec(col_off + n_col))
        args.append(w)
        scratch.append(pltpu.VMEM((D, tn), BF16))
    if bias is not None:
        b2 = bias.reshape(1, -1)
        in_specs.append(pl.BlockSpec((1, tn), lambda j, i: (0, j + col_off)))
        args.append(b2)
        if glu:
            in_specs.append(pl.BlockSpec((1, tn), lambda j, i: (0, j + col_off + n_col)))
            args.append(b2)
    if rope is not None:
        rope_spec = pl.BlockSpec((TM, HEAD_DIM), lambda j, i: (i % TILES_PER_BATCH, 0))
        in_specs += [rope_spec, rope_spec]
        args += [rope[0], rope[1]]
    return pl.pallas_call(
        functools.partial(_proj_kernel, has_bias=bias is not None, glu=glu, rope=rope is not None, tn=tn),
        grid=(n_col, n_tiles),
        in_specs=in_specs,
        out_specs=pl.BlockSpec((TM, tn), lambda j, i: (i, j)),
        out_shape=jax.ShapeDtypeStruct((n_rows, n_out), out_dtype),
        scratch_shapes=scratch,
        **_opts(2, flops=2 * n_rows * D * n_out * (2 if glu else 1),
                nbytes=4 * n_rows * D * n_col + 4 * D * n_out * (2 if glu else 1) + 4 * n_rows * n_out),
        name="proj",
    )(*args)


ROUTE_EXPERT, ROUTE_GATE, ROUTE_RANK = 0, 2, 4


def _first_lane_where(cond, lane):
    return jnp.min(jnp.where(cond, lane, ROUTER_PAD), axis=-1, keepdims=True)


def _route_tile(lg, cnt_ref, counted):
    lane = lax.broadcasted_iota(jnp.int32, lg.shape, 1)
    gmask = lane < MOE_GROUPS
    gl = jnp.where(gmask, lg, NEG_INF)
    ge = jnp.exp(gl - gl.max(axis=-1, keepdims=True))
    gp = ge / ge.sum(axis=-1, keepdims=True)
    g_p = gp.max(axis=-1, keepdims=True)
    g_idx = _first_lane_where(gmask & (gp == g_p), lane)
    lo = MOE_GROUPS + MOE_EPG * g_idx
    emask = (lane >= lo) & (lane < lo + MOE_EPG)
    el = jnp.where(emask, lg, NEG_INF)
    ee = jnp.exp(el - el.max(axis=-1, keepdims=True))
    ep = jnp.where(emask, ee / ee.sum(axis=-1, keepdims=True), -1.0)
    p1 = ep.max(axis=-1, keepdims=True)
    i1 = _first_lane_where(ep == p1, lane)
    ep2 = jnp.where(lane == i1, -1.0, ep)
    p2 = ep2.max(axis=-1, keepdims=True)
    i2 = _first_lane_where(ep2 == p2, lane)
    den = p1 + p2
    gate1 = g_p * p1 / den
    gate2 = g_p * p2 / den
    oh1 = jnp.where(lane == i1, 1.0, 0.0)
    oh2 = jnp.where(lane == i2, 1.0, 0.0)
    n = lg.shape[0]
    tri = jnp.where(lax.broadcasted_iota(jnp.int32, (n, n), 0) > lax.broadcasted_iota(jnp.int32, (n, n), 1),
                    1.0, 0.0).astype(BF16)
    base = cnt_ref[...]
    tot1 = oh1.sum(axis=0, keepdims=True)
    pre1 = _dot(tri, oh1.astype(BF16)) + base
    pre2 = _dot(tri, oh2.astype(BF16)) + (base + tot1)
    rank1 = (oh1 * pre1).sum(axis=-1, keepdims=True)
    rank2 = (oh2 * pre2).sum(axis=-1, keepdims=True)
    cnt_ref[...] = base + jnp.where(counted, tot1 + oh2.sum(axis=0, keepdims=True), 0.0)
    cols = ((i1 - MOE_GROUPS).astype(F32), (i2 - MOE_GROUPS).astype(F32), gate1, gate2, rank1, rank2)
    route = jnp.zeros(lg.shape, F32)
    for k, col in enumerate(cols):
        route = jnp.where(lane == k, col, route)
    return route


def _pack_bf16_halves(u):
    bits = pltpu.bitcast(u.astype(F32), jnp.uint32)
    half = u.shape[1] // 2
    return bits[:, :half] | (bits[:, half:] >> 16)


def _unpack_halves_f32(p):
    return pltpu.bitcast(p & jnp.uint32(0xFFFF0000), F32), pltpu.bitcast(p << 16, F32)


def _unpack_bf16_halves(p):
    hi, lo = _unpack_halves_f32(p)
    return hi.astype(BF16), lo.astype(BF16)


def _post_mixer(y_rows, x_ref, g_ref, lng_ref, lnb_ref, sc2_ref, sh2_ref, wr_ref, br_ref,
                x1_ref, u2_ref, route_ref, cnt_out_ref, cnt_ref, counted=True, n_chunks=1, before_chunk=None):
    logits = []
    for c in range(n_chunks):
        if before_chunk is not None:
            before_chunk(c)
        rows = slice(c * TM // n_chunks, (c + 1) * TM // n_chunks)
        z = DEEPNORM_ALPHA * x_ref[rows, :] + g_ref[0] * y_rows(rows)
        x1 = _layer_norm(z, lng_ref[...], lnb_ref[...])
        x1_ref[rows, :] = x1
        u2 = (x1 * (1.0 + sc2_ref[0]) + sh2_ref[0]).astype(BF16)
        u2_ref[rows, :] = _pack_bf16_halves(u2)
        logits.append(_dot(u2, wr_ref[...]) + br_ref[...])
    logits = logits[0] if n_chunks == 1 else jnp.concatenate(logits, axis=0)
    route_ref[...] = _route_tile(logits, cnt_ref, counted)
    cnt_out_ref[...] = jnp.broadcast_to(cnt_ref[...], cnt_out_ref.shape)


def _post_mixer_specs(n_batch, lag=0):
    tile = lambda i: jnp.maximum(i - lag, 0)
    grp = lambda i: (_mod_group(tile(i), n_batch), 0, 0)
    row = lambda i: (tile(i), 0)
    const = lambda i: (0, 0)
    in_specs = [pl.BlockSpec((TM, D), row),
                pl.BlockSpec((1, 1, D), grp),
                pl.BlockSpec((1, D), const),
                pl.BlockSpec((1, D), const),
                pl.BlockSpec((1, 1, D), grp),
                pl.BlockSpec((1, 1, D), grp),
                pl.BlockSpec((D, ROUTER_PAD), const),
                pl.BlockSpec((1, ROUTER_PAD), const)]
    out_specs = [pl.BlockSpec((TM, D), row),
                 pl.BlockSpec((TM, D // 2), row),
                 pl.BlockSpec((TM, ROUTER_PAD), row),
                 pl.BlockSpec((8, ROUTER_PAD), const)]
    return in_specs, out_specs


def _post_mixer_out_shape(n_rows):
    return [jax.ShapeDtypeStruct((n_rows, D), F32),
            jax.ShapeDtypeStruct((n_rows, D // 2), jnp.uint32),
            jax.ShapeDtypeStruct((n_rows, ROUTER_PAD), F32),
            jax.ShapeDtypeStruct((8, ROUTER_PAD), F32)]


def _post_mixer_scratch():
    return [pltpu.VMEM((1, ROUTER_PAD), F32)]


OUT_PROJ_CHUNKS = 4


def _out_proj_kernel(o_ref, wo_ref, *rest):
    cnt_ref, y_even_ref, y_odd_ref = rest[-3:]
    i = pl.program_id(0)

    @pl.when(i == 0)
    def _():
        cnt_ref[...] = jnp.zeros(cnt_ref.shape, F32)
        y_odd_ref[...] = jnp.zeros(y_odd_ref.shape, F32)

    def step(cur_ref, prev_ref):
        def project(c):
            cols = slice(c * D // OUT_PROJ_CHUNKS, (c + 1) * D // OUT_PROJ_CHUNKS)
            cur_ref[:, cols] = _dot(o_ref[...], wo_ref[:, cols])

        _post_mixer(lambda rows: prev_ref[rows, :], *rest[:-2], counted=i > 0,
                    n_chunks=OUT_PROJ_CHUNKS, before_chunk=project)

    @pl.when(i % 2 == 0)
    def _():
        step(y_even_ref, y_odd_ref)

    @pl.when(i % 2 == 1)
    def _():
        step(y_odd_ref, y_even_ref)


def _out_proj(o, wo, x, g1, lng, lnb, sc2, sh2, wr, br):
    n_rows = x.shape[0]
    n_batch = g1.shape[0] - 1
    n_tiles = n_rows // TM
    pm_in, pm_out = _post_mixer_specs(n_batch, lag=1)
    return pl.pallas_call(
        _out_proj_kernel,
        grid=(n_tiles + 1,),
        in_specs=[pl.BlockSpec((TM, D), lambda i: (jnp.minimum(i, n_tiles - 1), 0)),
                  pl.BlockSpec((D, D), lambda i: (0, 0))] + pm_in,
        out_specs=pm_out,
        out_shape=_post_mixer_out_shape(n_rows),
        scratch_shapes=_post_mixer_scratch() + [pltpu.VMEM((TM, D), F32), pltpu.VMEM((TM, D), F32)],
        **_opts(1, flops=2 * n_rows * D * D, nbytes=12 * n_rows * D + 2 * D * D),
        name="out_proj",
    )(o, wo, x, g1, lng, lnb, sc2, sh2, wr, br)


def _na_key_row_start(j, rows):
    return jnp.clip(NA_QROWS * j - NA_KH // 2, 0, rows - NA_KROWS)


def _na_bias_table(rpb):
    rows = L // GRID_W
    n_blocks = rows // NA_QROWS
    j = jnp.array([0, 1, n_blocks - 1])
    qr = (NA_QROWS * j)[:, None] + jnp.arange(NA_QROWS)[None, :]
    kr = _na_key_row_start(j, rows)[:, None] + jnp.arange(NA_KROWS)[None, :]
    r0 = jnp.clip(qr - NA_KH // 2, 0, rows - NA_KH)
    row_ok = (kr[:, None, :] >= r0[:, :, None]) & (kr[:, None, :] < r0[:, :, None] + NA_KH)
    dr = jnp.clip(kr[:, None, :] - qr[:, :, None] + NA_KH - 1, 0, 2 * NA_KH - 2)
    cols = jnp.arange(GRID_W)
    col_start = jnp.clip(cols - NA_KW // 2, 0, GRID_W - NA_KW)
    col_ok = (cols[None, :] >= col_start[:, None]) & (cols[None, :] < col_start[:, None] + NA_KW)
    dc = jnp.clip(cols[None, :] - cols[:, None] + NA_KW - 1, 0, 2 * NA_KW - 2)
    by_row = jnp.where(col_ok, rpb[:, :, dc] * (1.0 / SCALE), NEG_INF)
    bias = jnp.where(row_ok[None, :, :, :, None, None], by_row[:, dr], NEG_INF)
    bias = jnp.transpose(bias, (0, 1, 2, 4, 3, 5))
    return bias.reshape(NA_HEADS, 3, NA_QROWS * GRID_W, NA_KROWS * GRID_W)


NA_GROUP = 8


def _na_kernel(q_ref, k_ref, v_ref, bias_ref, o_ref):
    rows = L // GRID_W
    qb = NA_QROWS * GRID_W
    kb = NA_KROWS * GRID_W
    n_blocks = rows // NA_QROWS
    kc = k_ref[L:S, :]
    vc = v_ref[L:S, :]

    def attend(blocks):
        scores = []
        for q_rows, k_rows, bias in blocks:
            q = q_ref[q_rows, :]
            s = _dot_t(q, kc)
            if k_rows is not None:
                s = jnp.concatenate([_dot_t(q, k_ref[k_rows, :]) + bias, s], axis=1)
            scores.append(s)
        probs = []
        for s in scores:
            p = jnp.exp2((s - s.max(axis=-1, keepdims=True)) * (SCALE * LOG2E))
            probs.append((p.astype(BF16), p.sum(axis=-1, keepdims=True)))
        for (q_rows, k_rows, _), (p, den) in zip(blocks, probs):
            if k_rows is None:
                o = _dot(p, vc)
            else:
                o = _dot(p[:, :kb], v_ref[k_rows, :]) + _dot(p[:, kb:], vc)
            o_ref[q_rows, :] = (o / den).astype(o_ref.dtype)

    def step(i, carry):
        blocks = []
        for u in range(NA_GROUP):
            j = i * NA_GROUP + u
            ks = pl.multiple_of(_na_key_row_start(j, rows) * GRID_W, GRID_W)
            kind = jnp.where(j == 0, 0, jnp.where(j == n_blocks - 1, 2, 1))
            blocks.append((pl.ds(pl.multiple_of(j * qb, qb), qb), pl.ds(ks, kb), bias_ref[0, kind]))
        attend(blocks)
        return carry

    lax.fori_loop(0, n_blocks // NA_GROUP, step, 0)
    attend([(slice(L, S), None, None)])


def _na_attention(qkv, bias, n_batch):
    h = NA_HEADS
    n_blocks = bias.shape[1]
    blk = lambda off: pl.BlockSpec((S, HEAD_DIM), lambda hh, b: (b, off + hh))
    return pl.pallas_call(
        _na_kernel,
        grid=(h, n_batch),
        in_specs=[blk(0), blk(h), blk(2 * h),
                  pl.BlockSpec((1, n_blocks) + bias.shape[2:], lambda hh, b: (hh, 0, 0, 0))],
        out_specs=pl.BlockSpec((S, HEAD_DIM), lambda hh, b: (b, hh)),
        out_shape=jax.ShapeDtypeStruct((n_batch * S, h * HEAD_DIM), BF16),
        **_opts(2, flops=4 * n_batch * h * S * (NA_KROWS * GRID_W + C) * HEAD_DIM,
                nbytes=8 * n_batch * S * h * HEAD_DIM,
                transcendentals=n_batch * h * S * (NA_KROWS * GRID_W + C)),
        name="na_attn",
    )(qkv, qkv, qkv, bias)


def _swa_mask_table():
    span = SWA_BLOCK + 2 * SWA_WINDOW
    qi = jnp.arange(SWA_BLOCK)[:, None]
    kj = jnp.arange(span)[None, :]
    band = jnp.abs(qi - (kj - SWA_WINDOW)) <= SWA_WINDOW
    first = band & (kj >= SWA_WINDOW)
    last = band & (kj < SWA_BLOCK + SWA_WINDOW)
    m = jnp.stack([first, band, last]).astype(F32)
    return jnp.where(m > 0, 0.0, NEG_INF).astype(F32)


SWA_GROUP = 4


def _swa_kernel(sink_ref, q_ref, k_ref, v_ref, mask_ref, o_ref, kpad_ref, vpad_ref):
    g = pl.program_id(1)
    nb = L // SWA_BLOCK
    span = SWA_BLOCK + 2 * SWA_WINDOW
    pad = SWA_WINDOW
    zeros = jnp.zeros((pad, HEAD_DIM), BF16)
    for src, dst in ((k_ref, kpad_ref), (v_ref, vpad_ref)):
        dst[0:pad, :] = zeros
        dst[pad + L:, :] = zeros
        dst[pad:pad + L, :] = src[0:L, :]
    kc = k_ref[L:S, :]
    vc = v_ref[L:S, :]

    def head(r):
        return slice(r * HEAD_DIM, (r + 1) * HEAD_DIM)

    def attend(blocks):
        scores = []
        for rows, keys, _, mask in blocks:
            for r in range(SWA_REP):
                q = q_ref[rows, head(r)]
                s = [_dot_t(q, k) for k in keys]
                if mask is not None:
                    s[0] = s[0] + mask
                scores.append(s[0] if len(s) == 1 else jnp.concatenate(s, axis=1))
        probs = []
        for i, s in enumerate(scores):
            sink_raw = sink_ref[g * SWA_REP + i % SWA_REP] * (1.0 / SCALE)
            m = jnp.maximum(s.max(axis=-1, keepdims=True), sink_raw)
            p = jnp.exp2((s - m) * (SCALE * LOG2E))
            den = p.sum(axis=-1, keepdims=True) + jnp.exp2((sink_raw - m) * (SCALE * LOG2E))
            probs.append((p.astype(BF16), den))
        for b, (rows, _, values, _) in enumerate(blocks):
            v = values[0] if len(values) == 1 else jnp.concatenate(values, axis=0)
            for r in range(SWA_REP):
                p, den = probs[b * SWA_REP + r]
                o_ref[rows, head(r)] = (_dot(p, v) / den).astype(o_ref.dtype)

    def step(i, carry):
        blocks = []
        for u in range(SWA_GROUP):
            n = i * SWA_GROUP + u
            start = pl.multiple_of(n * SWA_BLOCK, SWA_BLOCK)
            kind = jnp.where(n == 0, 0, jnp.where(n == nb - 1, 2, 1))
            blocks.append((pl.ds(start, SWA_BLOCK), [kpad_ref[pl.ds(start, span), :], kc],
                           [vpad_ref[pl.ds(start, span), :], vc], mask_ref[kind]))
        attend(blocks)
        return carry

    lax.fori_loop(0, nb // SWA_GROUP, step, 0)
    attend([(slice(L, S), [kc], [vc], None)])


def _swa_attention(qk, v, sink, mask, n_batch):
    qw = SWA_REP * HEAD_DIM
    n_q_blocks = SWA_KV_HEADS
    return pl.pallas_call(
        _swa_kernel,
        grid_spec=pltpu.PrefetchScalarGridSpec(
            num_scalar_prefetch=1,
            grid=(n_batch, SWA_KV_HEADS),
            in_specs=[pl.BlockSpec((S, qw), lambda b, g, sk: (b, g)),
                      pl.BlockSpec((S, HEAD_DIM), lambda b, g, sk: (b, n_q_blocks * SWA_REP + g)),
                      pl.BlockSpec((S, HEAD_DIM), lambda b, g, sk: (b, g)),
                      pl.BlockSpec(mask.shape, lambda b, g, sk: (0, 0, 0))],
            out_specs=pl.BlockSpec((S, qw), lambda b, g, sk: (b, g)),
            scratch_shapes=[pltpu.VMEM((L + 2 * SWA_WINDOW, HEAD_DIM), BF16),
                            pltpu.VMEM((L + 2 * SWA_WINDOW, HEAD_DIM), BF16)]),
        out_shape=jax.ShapeDtypeStruct((n_batch * S, SWA_KV_HEADS * qw), BF16),
        **_opts(2, flops=4 * n_batch * SWA_KV_HEADS * SWA_REP * S * (SWA_BLOCK + 2 * SWA_WINDOW + C) * HEAD_DIM,
                nbytes=2 * n_batch * S * (2 * SWA_KV_HEADS * qw + 2 * SWA_KV_HEADS * HEAD_DIM),
                transcendentals=n_batch * SWA_KV_HEADS * SWA_REP * S * (SWA_BLOCK + 2 * SWA_WINDOW + C)),
        name="swa_attn",
    )(sink, qk, qk, v, mask)


DIFF_GROUP = 4


def _diff_kernel(lam_ref, g_ref, q_ref, k_ref, v_ref, o_ref, *, lambda_init, tq):
    lam = lam_ref[...]
    lmbda = (jnp.exp(jnp.sum(lam[0:1] * lam[1:2], axis=-1, keepdims=True))
             - jnp.exp(jnp.sum(lam[2:3] * lam[3:4], axis=-1, keepdims=True)) + lambda_init)
    gain = g_ref[...] * (1.0 - lambda_init)

    def softmax_parts(s):
        m = s.max(axis=-1, keepdims=True)
        e = jnp.exp2((s - m) * (SCALE * LOG2E))
        return e, e.sum(axis=-1, keepdims=True)

    def attend(q_blocks, keys):
        scores = []
        for q_rows in q_blocks:
            q = q_ref[q_rows, :]
            scores.append((_dot_t(q[:, :HEAD_DIM], k_ref[keys, 0:HEAD_DIM]),
                           _dot_t(q[:, HEAD_DIM:], k_ref[keys, HEAD_DIM:2 * HEAD_DIM])))
        probs = []
        for s1, s2 in scores:
            e1, d1 = softmax_parts(s1)
            e2, d2 = softmax_parts(s2)
            probs.append((e1 * (1.0 / d1) - e2 * (lmbda / d2)).astype(BF16))
        for q_rows, pd in zip(q_blocks, probs):
            o = _dot(pd, v_ref[keys, :])
            o = o * lax.rsqrt(jnp.mean(o * o, axis=-1, keepdims=True) + LN_EPS) * gain
            o_ref[q_rows, :] = o.astype(o_ref.dtype)

    def step(i, carry):
        attend([pl.ds(pl.multiple_of((i * DIFF_GROUP + u) * tq, tq), tq) for u in range(DIFF_GROUP)], slice(0, S))
        return carry

    lax.fori_loop(0, L // (tq * DIFF_GROUP), step, 0)
    attend([slice(L, S)], slice(L, S))


def _diff_attention(qk, v, lam, subln_g, lambda_init, n_batch):
    hw = 2 * HEAD_DIM
    return pl.pallas_call(
        functools.partial(_diff_kernel, lambda_init=lambda_init, tq=256),
        grid=(n_batch, DIFF_HEADS),
        in_specs=[pl.BlockSpec((4, HEAD_DIM), lambda b, h: (0, 0)),
                  pl.BlockSpec((1, hw), lambda b, h: (0, 0)),
                  pl.BlockSpec((S, hw), lambda b, h: (b, h)),
                  pl.BlockSpec((S, hw), lambda b, h: (b, DIFF_HEADS + h)),
                  pl.BlockSpec((S, hw), lambda b, h: (b, h))],
        out_specs=pl.BlockSpec((S, hw), lambda b, h: (b, h)),
        out_shape=jax.ShapeDtypeStruct((n_batch * S, DIFF_HEADS * hw), BF16),
        **_opts(2, flops=8 * n_batch * DIFF_HEADS * S * S * HEAD_DIM, nbytes=8 * n_batch * S * DIFF_HEADS * hw,
                transcendentals=2 * n_batch * DIFF_HEADS * S * S),
        name="diff_attn",
    )(lam, subln_g.reshape(1, hw), qk, qk, v)


CONV_ROWS = 128


def _conv_kernel(hc_ref, hp_ref, hn_ref, dw_ref, dwb_ref, cg_ref, cb_ref, wout_ref, bout_ref, *rest):
    cnt_ref, win_ref, cv_ref = rest[-3:]

    @pl.when(pl.program_id(0) == 0)
    def _():
        cnt_ref[...] = jnp.zeros(cnt_ref.shape, F32)

    t = pl.program_id(0) % TILES_PER_BATCH
    first = (t == 0) | (t == TILES_PER_BATCH - 1)
    last = t >= TILES_PER_BATCH - 2
    win_ref[0:CONV_HALO, :] = jnp.where(first, 0.0, hp_ref[...])
    win_ref[CONV_HALO:CONV_HALO + TM, :] = hc_ref[...]
    win_ref[CONV_HALO + TM:, :] = jnp.where(last, 0.0, hn_ref[...])
    base = CONV_HALO - CONV_WIDTH // 2
    sub = 8

    def strip(c, carry):
        cs = pl.ds(pl.multiple_of(c * HEAD_DIM, HEAD_DIM), HEAD_DIM)
        for r in range(TM // CONV_ROWS):
            acc = jnp.zeros((CONV_ROWS, HEAD_DIM), F32)
            aligned = win_ref[pl.ds(r * CONV_ROWS, CONV_ROWS + 2 * CONV_HALO), cs]
            for res in range(sub):
                taps = [j for j in range(CONV_WIDTH) if (base + j) % sub == res]
                w = aligned if res == 0 else pltpu.roll(aligned, aligned.shape[0] - res, 0)
                for j in taps:
                    lo = (base + j) // sub * sub
                    acc = acc + dw_ref[j:j + 1, cs] * w[lo:lo + CONV_ROWS]
            cv_ref[r * CONV_ROWS:(r + 1) * CONV_ROWS, cs] = acc
        return carry

    lax.fori_loop(0, D // HEAD_DIM, strip, 0)

    hn = _layer_norm(cv_ref[...] + dwb_ref[...], cg_ref[...], cb_ref[...])
    hn = (hn * _sigmoid(hn)).astype(BF16)
    y = _dot(hn, wout_ref[...]) + bout_ref[...]
    _post_mixer(lambda rows: y[rows, :], *rest[:-2])


def _conv_mixer(h, dw, dw_b, cg, cb, wout, bout, x, g1, lng, lnb, sc2, sh2, wr, br):
    n_rows = x.shape[0]
    n_batch = g1.shape[0] - 1
    n_tiles = n_rows // TM
    halo_per_tile = TM // CONV_HALO
    n_halo = n_rows // CONV_HALO
    pm_in, pm_out = _post_mixer_specs(n_batch)
    const = lambda i: (0, 0)
    vec = lambda a: a.reshape(1, D)
    dw_pad = jnp.concatenate([dw, jnp.zeros((1, D), dw.dtype)], axis=0)
    return pl.pallas_call(
        _conv_kernel,
        grid=(n_tiles,),
        in_specs=[pl.BlockSpec((TM, D), lambda i: (i, 0)),
                  pl.BlockSpec((CONV_HALO, D), lambda i: (jnp.maximum(i * halo_per_tile - 1, 0), 0)),
                  pl.BlockSpec((CONV_HALO, D), lambda i: (jnp.minimum((i + 1) * halo_per_tile, n_halo - 1), 0)),
                  pl.BlockSpec((CONV_WIDTH + 1, D), const),
                  pl.BlockSpec((1, D), const),
                  pl.BlockSpec((1, D), const),
                  pl.BlockSpec((1, D), const),
                  pl.BlockSpec((D, D), const),
                  pl.BlockSpec((1, D), const)] + pm_in,
        out_specs=pm_out,
        out_shape=_post_mixer_out_shape(n_rows),
        scratch_shapes=_post_mixer_scratch() + [pltpu.VMEM((TM + 2 * CONV_HALO, D), F32),
                                                pltpu.VMEM((TM, D), F32)],
        **_opts(1, flops=2 * n_rows * D * (D + CONV_WIDTH), nbytes=14 * n_rows * D + 2 * D * D),
        name="conv_mixer",
    )(h, h, h, dw_pad, vec(dw_b), vec(cg), vec(cb), wout, vec(bout), x, g1, lng, lnb, sc2, sh2, wr, br)


def _moe_kernel(be_ref, nu_ref, xs_ref, w13_ref, w2_ref, ys_ref, w13b_ref, w2b_ref):
    i = pl.program_id(0)

    @pl.when(i >= nu_ref[0])
    def _():
        ys_ref[...] = jnp.zeros(ys_ref.shape, ys_ref.dtype)

    @pl.when(i < nu_ref[0])
    def _():
        changed = (i == 0) | (be_ref[i] != be_ref[jnp.maximum(i - 1, 0)])

        @pl.when(changed)
        def _():
            w13b_ref[...] = w13_ref[0].astype(BF16)
            w2b_ref[...] = w2_ref[0].astype(BF16)

        x_hi, x_lo = _unpack_bf16_halves(xs_ref[...])
        a = _dot(x_hi, w13b_ref[0:D // 2, :]) + _dot(x_lo, w13b_ref[D // 2:, :])
        gate = a[:, :MOE_D_FF]
        hmid = (gate * _sigmoid(gate) * a[:, MOE_D_FF:]).astype(BF16)
        ys_ref[...] = _pack_bf16_halves(_dot(hmid, w2b_ref[...]).astype(BF16))


def _moe_experts(xs, blk_expert, n_used, w13, w2):
    n_slots = xs.shape[0]
    n_blocks = n_slots // MOE_BLOCK
    live = lambda i, nu: jnp.minimum(i, nu[0] - 1)
    return pl.pallas_call(
        _moe_kernel,
        grid_spec=pltpu.PrefetchScalarGridSpec(
            num_scalar_prefetch=2,
            grid=(n_blocks,),
            in_specs=[pl.BlockSpec((MOE_BLOCK, D // 2), lambda i, be, nu: (live(i, nu), 0)),
                      pl.BlockSpec((1, D, 2 * MOE_D_FF), lambda i, be, nu: (be[live(i, nu)], 0, 0)),
                      pl.BlockSpec((1, MOE_D_FF, D), lambda i, be, nu: (be[live(i, nu)], 0, 0))],
            out_specs=pl.BlockSpec((MOE_BLOCK, D // 2), lambda i, be, nu: (i, 0)),
            scratch_shapes=[pltpu.VMEM((D, 2 * MOE_D_FF), BF16),
                            pltpu.VMEM((MOE_D_FF, D), BF16)]),
        out_shape=jax.ShapeDtypeStruct((n_slots, D // 2), jnp.uint32),
        **_opts(1, flops=6 * n_slots * D * MOE_D_FF, nbytes=4 * n_slots * D + 12 * MOE_EXPERTS * D * MOE_D_FF),
        name="moe_experts",
    )(blk_expert, n_used, xs, w13, w2)


def _combine_kernel(x_ref, ya_ref, yb_ref, route_ref, g_ref, lng_ref, lnb_ref, o_ref):
    route = route_ref[...]
    gate_a = route[:, ROUTE_GATE:ROUTE_GATE + 1]
    gate_b = route[:, ROUTE_GATE + 1:ROUTE_GATE + 2]
    a_hi, a_lo = _unpack_halves_f32(ya_ref[...])
    b_hi, b_lo = _unpack_halves_f32(yb_ref[...])
    f = jnp.concatenate([a_hi * gate_a + b_hi * gate_b, a_lo * gate_a + b_lo * gate_b], axis=1)
    z = DEEPNORM_ALPHA * x_ref[...] + g_ref[0] * f
    o_ref[...] = _layer_norm(z, lng_ref[...], lnb_ref[...])


def _combine_latent_kernel(*refs):
    @pl.when(pl.program_id(0) % TILES_PER_BATCH < LAT_TILES)
    def _():
        _combine_kernel(*refs)


def _combine(x1, ya, yb, route, g2, lng, lnb, latent_only):
    n_rows = x1.shape[0]
    n_batch = g2.shape[0] - 1
    row = pl.BlockSpec((TM, D), lambda i: (i, 0))
    packed = pl.BlockSpec((TM, D // 2), lambda i: (i, 0))
    const = pl.BlockSpec((1, D), lambda i: (0, 0))
    if latent_only:
        out_rows = n_batch * L
        out_spec = pl.BlockSpec((TM, D), lambda i: (
            i // TILES_PER_BATCH * LAT_TILES + jnp.minimum(i % TILES_PER_BATCH, LAT_TILES - 1), 0))
    else:
        out_rows, out_spec = n_rows, row
    return pl.pallas_call(
        _combine_latent_kernel if latent_only else _combine_kernel,
        grid=(n_rows // TM,),
        in_specs=[row, packed, packed,
                  pl.BlockSpec((TM, ROUTER_PAD), lambda i: (i, 0)),
                  pl.BlockSpec((1, 1, D), lambda i: (_mod_group(i, n_batch), 0, 0)),
                  const, const],
        out_specs=out_spec,
        out_shape=jax.ShapeDtypeStruct((out_rows, D), F32),
        **_opts(1, flops=10 * n_rows * D, nbytes=16 * n_rows * D),
        name="moe_combine",
    )(x1, ya, yb, route, g2, lng, lnb)


def _dispatch_plan(route, counts):
    n = route.shape[0]
    a = n * MOE_TOP_K
    experts = jnp.arange(MOE_EXPERTS, dtype=jnp.int32)
    counts = counts[0, MOE_GROUPS:MOE_GROUPS + MOE_EXPERTS].astype(jnp.int32)
    padded = (counts + MOE_BLOCK - 1) // MOE_BLOCK * MOE_BLOCK
    pad_end = jnp.cumsum(padded)
    pad_start = pad_end - padded
    expert = route[:, ROUTE_EXPERT:ROUTE_EXPERT + MOE_TOP_K].astype(jnp.int32)
    rank = route[:, ROUTE_RANK:ROUTE_RANK + MOE_TOP_K].astype(jnp.int32)
    slot = jnp.sum(jnp.where(expert[:, :, None] == experts, pad_start, 0), axis=-1) + rank
    n_blocks = -(-a // MOE_BLOCK) + MOE_EXPERTS
    tok_of = jnp.broadcast_to(jnp.arange(n, dtype=jnp.int32)[:, None], (n, MOE_TOP_K))
    tok = (jnp.arange(n_blocks * MOE_BLOCK, dtype=jnp.int32) % n).at[slot.reshape(a)].set(
        tok_of.reshape(a), unique_indices=True, mode='promise_in_bounds')
    blk_start = jnp.arange(n_blocks, dtype=jnp.int32) * MOE_BLOCK
    blk_expert = jnp.minimum(jnp.sum((pad_end[None, :] <= blk_start[:, None]).astype(jnp.int32), axis=-1),
                             MOE_EXPERTS - 1)
    n_used = (pad_end[-1] // MOE_BLOCK).reshape(1)
    return slot, tok, blk_expert, n_used


def _take_rows(a, idx):
    return a.at[idx].get(mode='promise_in_bounds')


def _moe_layer(x1, u2, route, counts, g2, lng, lnb, w13, w2, latent_only):
    slot, tok, blk_expert, n_used = _dispatch_plan(route, counts)
    ys = _moe_experts(_take_rows(u2, tok), blk_expert, n_used, w13, w2)
    return _combine(x1, _take_rows(ys, slot[:, 0]), _take_rows(ys, slot[:, 1]), route, g2, lng, lnb, latent_only)


def _rope_tables():
    t = jnp.arange(L, dtype=jnp.int32)
    pos = jnp.stack([t // GRID_W, t % GRID_W], -1).astype(F32)
    n_freq = HEAD_DIM // 4
    inv_freq = ROPE_THETA ** (-jnp.arange(n_freq, dtype=F32) / n_freq)
    ang = pos[:, :, None] * inv_freq
    cos, sin = jnp.cos(ang), jnp.sin(ang)
    cos_t = jnp.stack([cos, cos], axis=2).reshape(L, HEAD_DIM)
    sin_t = jnp.stack([-sin, sin], axis=2).reshape(L, HEAD_DIM)
    cos_t = jnp.concatenate([cos_t, jnp.ones((C, HEAD_DIM), F32)], axis=0)
    sin_t = jnp.concatenate([sin_t, jnp.zeros((C, HEAD_DIM), F32)], axis=0)
    return cos_t, sin_t


def _router_params(rg_w, rg_b, re_w, re_b):
    n = MOE_GROUPS + MOE_EXPERTS
    w = jnp.concatenate([rg_w, re_w, jnp.zeros((D, ROUTER_PAD - n), F32)], axis=1).astype(BF16)
    b = jnp.concatenate([rg_b, re_b, jnp.zeros((ROUTER_PAD - n,), F32)]).reshape(1, ROUTER_PAD)
    return w, b


def _modulation_inputs(c, c_ctx):
    n = c.shape[0] + 1
    pad = -n % 8
    return jnp.concatenate([c, c_ctx[None, :], jnp.zeros((pad, D), F32)], axis=0)


def _mixer_fn(idx, mixer):
    kind = idx % 4
    if kind == 0:
        w_qkv, rpb, w_o = mixer
        w_o, bias = w_o.astype(BF16), _na_bias_table(rpb)

        def run(xs, sc1, sh1, n_batch, post):
            qkv = _proj(xs, sc1, sh1, w_qkv, n_out=3 * D, tn=D, out_dtype=BF16)
            return _out_proj(_na_attention(qkv, bias, n_batch), w_o, *post)
    elif kind == 1:
        w_in, b_in, dw, dw_b, cg, cb, w_out, b_out = mixer
        w_out = w_out.astype(BF16)

        def run(xs, sc1, sh1, n_batch, post):
            h = _proj(xs, sc1, sh1, w_in, n_out=D, tn=D // 2, out_dtype=F32, bias=b_in, glu=True)
            return _conv_mixer(h, dw, dw_b, cg, cb, w_out, b_out, *post)
    elif kind == 2:
        w_qkv, sink, w_o = mixer
        n_qk = (SWA_KV_HEADS * SWA_REP + SWA_KV_HEADS) * HEAD_DIM
        n_v = SWA_KV_HEADS * HEAD_DIM
        w_o, rope, mask = w_o.astype(BF16), _rope_tables(), _swa_mask_table()

        def run(xs, sc1, sh1, n_batch, post):
            qk = _proj(xs, sc1, sh1, w_qkv, n_out=n_qk, tn=n_qk, out_dtype=BF16, rope=rope)
            v = _proj(xs, sc1, sh1, w_qkv, n_out=n_v, tn=n_v, col_off=n_qk // n_v, out_dtype=BF16)
            return _out_proj(_swa_attention(qk, v, sink, mask, n_batch), w_o, *post)
    else:
        w_qkv, lam, subln_g, w_o = mixer
        lambda_init = 0.8 - 0.6 * math.exp(-0.3 * idx)
        w_o, rope = w_o.astype(BF16), _rope_tables()

        def run(xs, sc1, sh1, n_batch, post):
            qk = _proj(xs, sc1, sh1, w_qkv, n_out=2 * D, tn=D, out_dtype=BF16, rope=rope)
            v = _proj(xs, sc1, sh1, w_qkv, n_out=D, tn=D, col_off=2, out_dtype=BF16)
            return _out_proj(_diff_attention(qk, v, lam, subln_g, lambda_init, n_batch), w_o, *post)
    return run


def _hybrid_layer(idx, streams, cvec, mod_w, mod_b, mixer, ln1_g, ln1_b, moe, ln2_g, ln2_b, last=False):
    vec = lambda a: a.reshape(1, D)
    m_all = _adaln(cvec, mod_w, mod_b)
    ctx_row = sum(nb for _, _, nb in streams)
    rg_w, rg_b, re_w, re_b, w13, w2 = moe
    wr, br = _router_params(rg_w, rg_b, re_w, re_b)
    run_mixer = _mixer_fn(idx, mixer)
    out = []
    for xs, b0, nb in streams:
        m = jnp.concatenate([m_all[b0:b0 + nb], m_all[ctx_row:ctx_row + 1]], axis=0)
        sh1, sc1, g1, sh2, sc2, g2 = [m[:, None, k * D:(k + 1) * D] for k in range(6)]
        post = (xs, g1, vec(ln1_g), vec(ln1_b), sc2, sh2, wr, br)
        routed = run_mixer(xs, sc1, sh1, nb, post)
        out.append((_moe_layer(*routed, g2, vec(ln2_g), vec(ln2_b), w13, w2, latent_only=last), b0, nb))
    return out


def kernel(x, c, ctx, c_ctx, l0_mod_w, l0_mod_b, l0_na_w_qkv, l0_na_rpb, l0_na_w_o, l0_ln1_g, l0_ln1_b, l0_router_g_w, l0_router_g_b, l0_router_e_w, l0_router_e_b, l0_moe_w13, l0_moe_w2, l0_ln2_g, l0_ln2_b, l1_mod_w, l1_mod_b, l1_cv_w_in, l1_cv_b_in, l1_cv_dw, l1_cv_dw_b, l1_cv_ln_g, l1_cv_ln_b, l1_cv_w_out, l1_cv_b_out, l1_ln1_g, l1_ln1_b, l1_router_g_w, l1_router_g_b, l1_router_e_w, l1_router_e_b, l1_moe_w13, l1_moe_w2, l1_ln2_g, l1_ln2_b, l2_mod_w, l2_mod_b, l2_sw_w_qkv, l2_sw_sink, l2_sw_w_o, l2_ln1_g, l2_ln1_b, l2_router_g_w, l2_router_g_b, l2_router_e_w, l2_router_e_b, l2_moe_w13, l2_moe_w2, l2_ln2_g, l2_ln2_b, l3_mod_w, l3_mod_b, l3_df_w_qkv, l3_df_lambda, l3_df_subln_g, l3_df_w_o, l3_ln1_g, l3_ln1_b, l3_router_g_w, l3_router_g_b, l3_router_e_w, l3_router_e_b, l3_moe_w13, l3_moe_w2, l3_ln2_g, l3_ln2_b):
    layers = (
        (l0_mod_w, l0_mod_b, (l0_na_w_qkv, l0_na_rpb, l0_na_w_o), l0_ln1_g, l0_ln1_b,
         (l0_router_g_w, l0_router_g_b, l0_router_e_w, l0_router_e_b, l0_moe_w13, l0_moe_w2), l0_ln2_g, l0_ln2_b),
        (l1_mod_w, l1_mod_b, (l1_cv_w_in, l1_cv_b_in, l1_cv_dw, l1_cv_dw_b, l1_cv_ln_g, l1_cv_ln_b, l1_cv_w_out,
                              l1_cv_b_out), l1_ln1_g, l1_ln1_b,
         (l1_router_g_w, l1_router_g_b, l1_router_e_w, l1_router_e_b, l1_moe_w13, l1_moe_w2), l1_ln2_g, l1_ln2_b),
        (l2_mod_w, l2_mod_b, (l2_sw_w_qkv, l2_sw_sink, l2_sw_w_o), l2_ln1_g, l2_ln1_b,
         (l2_router_g_w, l2_router_g_b, l2_router_e_w, l2_router_e_b, l2_moe_w13, l2_moe_w2), l2_ln2_g, l2_ln2_b),
        (l3_mod_w, l3_mod_b, (l3_df_w_qkv, l3_df_lambda, l3_df_subln_g, l3_df_w_o), l3_ln1_g, l3_ln1_b,
         (l3_router_g_w, l3_router_g_b, l3_router_e_w, l3_router_e_b, l3_moe_w13, l3_moe_w2), l3_ln2_g, l3_ln2_b),
    )
    n_batch = x.shape[0]
    assert x.shape[1:] == (L, D) and ctx.shape[1:] == (C, D)
    n_streams = N_STREAMS if n_batch % N_STREAMS == 0 else 1
    nb = n_batch // n_streams
    streams = [(jnp.concatenate([x[b0:b0 + nb], ctx[b0:b0 + nb]], axis=1).reshape(nb * S, D), b0, nb)
               for b0 in range(0, n_batch, nb)]
    cvec = _modulation_inputs(c, c_ctx)
    for idx in range(DEPTH):
        streams = _hybrid_layer(idx, streams, cvec, *layers[idx], last=idx == DEPTH - 1)
    out = [xs.reshape(nb, L, D) for xs, _, _ in streams]
    return out[0] if len(out) == 1 else jnp.concatenate(out, axis=0)
```

```python
import functools
import math

import jax
import jax.numpy as jnp
from jax import lax
from jax.experimental import pallas as pl
from jax.experimental.pallas import tpu as pltpu

D = 2048
L = 2048
C = 256
S = L + C
DEPTH = 4
GRID_W = 64
HEAD_DIM = 128
ROPE_THETA = 10000.0
LN_EPS = 1e-5
NEG_INF = -1e30
DEEPNORM_ALPHA = (2.0 * DEPTH) ** 0.25
NA_HEADS = 16
NA_KH = 8
NA_KW = 16
NA_QROWS = 4
NA_KROWS = 12
CONV_WIDTH = 31
CONV_HALO = 16
SWA_KV_HEADS = 4
SWA_REP = 4
SWA_WINDOW = 128
SWA_BLOCK = 128
DIFF_HEADS = 8
MOE_GROUPS = 4
MOE_EPG = 8
MOE_EXPERTS = 32
MOE_TOP_K = 2
MOE_D_FF = 512
MOE_BLOCK = 512
ROUTER_PAD = 128

TM = 256
TILES_PER_BATCH = S // TM
LAT_TILES = L // TM
N_STREAMS = 1
VMEM_LIMIT = 52 * 1024 * 1024
SCALE = HEAD_DIM ** -0.5
LOG2E = math.log2(math.e)

F32 = jnp.float32
BF16 = jnp.bfloat16


def _opts(n_axes, *, flops, nbytes, transcendentals=0):
    return dict(
        compiler_params=pltpu.CompilerParams(dimension_semantics=("arbitrary",) * n_axes,
                                             vmem_limit_bytes=VMEM_LIMIT),
        cost_estimate=pl.CostEstimate(flops=int(flops), transcendentals=int(transcendentals),
                                      bytes_accessed=int(nbytes)))


def _dot(a, b):
    return jnp.dot(a, b, preferred_element_type=F32)


def _dot_t(a, b):
    return lax.dot_general(a, b, (((1,), (1,)), ((), ())), preferred_element_type=F32)


def _sigmoid(x):
    return 1.0 / (1.0 + jnp.exp(-x))


def _layer_norm(z, g, b):
    mu = jnp.mean(z, axis=-1, keepdims=True)
    zc = z - mu
    var = jnp.mean(zc * zc, axis=-1, keepdims=True)
    return zc * lax.rsqrt(var + LN_EPS) * g + b


def _mod_group(i, n_batch):
    return jnp.where(i % TILES_PER_BATCH == TILES_PER_BATCH - 1, n_batch, i // TILES_PER_BATCH)


def _adaln_kernel(c_ref, w_ref, b_ref, o_ref):
    c = c_ref[...]
    s = c * _sigmoid(c)
    o_ref[...] = _dot(s.astype(BF16), w_ref[...].astype(BF16)) + b_ref[...]


def _adaln(cvec, w, b):
    r = cvec.shape[0]
    n = w.shape[1]
    tn = 1024
    return pl.pallas_call(
        _adaln_kernel,
        grid=(n // tn,),
        in_specs=[pl.BlockSpec((r, D), lambda j: (0, 0)),
                  pl.BlockSpec((D, tn), lambda j: (0, j)),
                  pl.BlockSpec((1, tn), lambda j: (0, j))],
        out_specs=pl.BlockSpec((r, tn), lambda j: (0, j)),
        out_shape=jax.ShapeDtypeStruct((r, n), F32),
        **_opts(1, flops=2 * r * D * n, nbytes=4 * D * n),
        name="adaln",
    )(cvec, w, b.reshape(1, n))


def _rope_rotate(y, cos, sin):
    lane = lax.broadcasted_iota(jnp.int32, y.shape, 1)
    partner = jnp.where(lane % 64 < 32, pltpu.roll(y, 96, 1), pltpu.roll(y, 32, 1))
    return y * cos + partner * sin


def _proj_kernel(*refs, has_bias, glu, rope, tn):
    it = iter(refs)
    x_ref, sc_ref, sh_ref, w_ref = next(it), next(it), next(it), next(it)
    wg_ref = next(it) if glu else None
    b_ref = next(it) if has_bias else None
    bg_ref = next(it) if glu else None
    cos_ref = next(it) if rope else None
    sin_ref = next(it) if rope else None
    o_ref = next(it)
    wb_ref = next(it)
    wgb_ref = next(it) if glu else None

    @pl.when(pl.program_id(1) == 0)
    def _():
        wb_ref[...] = w_ref[...].astype(BF16)
        if glu:
            wgb_ref[...] = wg_ref[...].astype(BF16)

    u = (x_ref[...] * (1.0 + sc_ref[0]) + sh_ref[0]).astype(BF16)
    y = _dot(u, wb_ref[...])
    if has_bias:
        y = y + b_ref[...]
    if glu:
        y = y * _sigmoid(_dot(u, wgb_ref[...]) + bg_ref[...])
    if rope:
        cos = cos_ref[...]
        sin = sin_ref[...]
        for h in range(tn // HEAD_DIM):
            sl = slice(h * HEAD_DIM, (h + 1) * HEAD_DIM)
            o_ref[:, sl] = _rope_rotate(y[:, sl], cos, sin).astype(o_ref.dtype)
    else:
        o_ref[...] = y.astype(o_ref.dtype)


def _proj(x, sc, sh, w, *, n_out, tn, out_dtype, col_off=0, bias=None, glu=False, rope=None):
    n_rows = x.shape[0]
    n_batch = sc.shape[0] - 1
    n_tiles = n_rows // TM
    n_col = n_out // tn
    grp = lambda j, i: (_mod_group(i, n_batch), 0, 0)
    w_spec = lambda off: pl.BlockSpec((D, tn), lambda j, i: (0, j + off), pipeline_mode=pl.Buffered(1))
    in_specs = [pl.BlockSpec((TM, D), lambda j, i: (i, 0)),
                pl.BlockSpec((1, 1, D), grp),
                pl.BlockSpec((1, 1, D), grp),
                w_spec(col_off)]
    args = [x, sc, sh, w]
    scratch = [pltpu.VMEM((D, tn), BF16)]
    if glu:
        in_specs.append(w_spec(col_off + n_col))
        args.append(w)
        scratch.append(pltpu.VMEM((D, tn), BF16))
    if bias is not None:
        b2 = bias.reshape(1, -1)
        in_specs.append(pl.BlockSpec((1, tn), lambda j, i: (0, j + col_off)))
        args.append(b2)
        if glu:
            in_specs.append(pl.BlockSpec((1, tn), lambda j, i: (0, j + col_off + n_col)))
            args.append(b2)
    if rope is not None:
        rope_spec = pl.BlockSpec((TM, HEAD_DIM), lambda j, i: (i % TILES_PER_BATCH, 0))
        in_specs += [rope_spec, rope_spec]
        args += [rope[0], rope[1]]
    return pl.pallas_call(
        functools.partial(_proj_kernel, has_bias=bias is not None, glu=glu, rope=rope is not None, tn=tn),
        grid=(n_col, n_tiles),
        in_specs=in_specs,
        out_specs=pl.BlockSpec((TM, tn), lambda j, i: (i, j)),
        out_shape=jax.ShapeDtypeStruct((n_rows, n_out), out_dtype),
        scratch_shapes=scratch,
        **_opts(2, flops=2 * n_rows * D * n_out * (2 if glu else 1),
                nbytes=4 * n_rows * D * n_col + 4 * D * n_out * (2 if glu else 1) + 4 * n_rows * n_out),
        name="proj",
    )(*args)


ROUTE_EXPERT, ROUTE_GATE, ROUTE_RANK = 0, 2, 4


def _first_lane_where(cond, lane):
    return jnp.min(jnp.where(cond, lane, ROUTER_PAD), axis=-1, keepdims=True)


def _route_tile(lg, cnt_ref, counted):
    lane = lax.broadcasted_iota(jnp.int32, lg.shape, 1)
    gmask = lane < MOE_GROUPS
    gl = jnp.where(gmask, lg, NEG_INF)
    ge = jnp.exp(gl - gl.max(axis=-1, keepdims=True))
    gp = ge / ge.sum(axis=-1, keepdims=True)
    g_p = gp.max(axis=-1, keepdims=True)
    g_idx = _first_lane_where(gmask & (gp == g_p), lane)
    lo = MOE_GROUPS + MOE_EPG * g_idx
    emask = (lane >= lo) & (lane < lo + MOE_EPG)
    el = jnp.where(emask, lg, NEG_INF)
    ee = jnp.exp(el - el.max(axis=-1, keepdims=True))
    ep = jnp.where(emask, ee / ee.sum(axis=-1, keepdims=True), -1.0)
    p1 = ep.max(axis=-1, keepdims=True)
    i1 = _first_lane_where(ep == p1, lane)
    ep2 = jnp.where(lane == i1, -1.0, ep)
    p2 = ep2.max(axis=-1, keepdims=True)
    i2 = _first_lane_where(ep2 == p2, lane)
    den = p1 + p2
    gate1 = g_p * p1 / den
    gate2 = g_p * p2 / den
    oh1 = jnp.where(lane == i1, 1.0, 0.0)
    oh2 = jnp.where(lane == i2, 1.0, 0.0)
    n = lg.shape[0]
    tri = jnp.where(lax.broadcasted_iota(jnp.int32, (n, n), 0) > lax.broadcasted_iota(jnp.int32, (n, n), 1),
                    1.0, 0.0).astype(BF16)
    base = cnt_ref[...]
    tot1 = oh1.sum(axis=0, keepdims=True)
    pre1 = _dot(tri, oh1.astype(BF16)) + base
    pre2 = _dot(tri, oh2.astype(BF16)) + (base + tot1)
    rank1 = (oh1 * pre1).sum(axis=-1, keepdims=True)
    rank2 = (oh2 * pre2).sum(axis=-1, keepdims=True)
    cnt_ref[...] = base + jnp.where(counted, tot1 + oh2.sum(axis=0, keepdims=True), 0.0)
    cols = ((i1 - MOE_GROUPS).astype(F32), (i2 - MOE_GROUPS).astype(F32), gate1, gate2, rank1, rank2)
    route = jnp.zeros(lg.shape, F32)
    for k, col in enumerate(cols):
        route = jnp.where(lane == k, col, route)
    return route


def _pack_bf16_halves(u):
    bits = pltpu.bitcast(u.astype(F32), jnp.uint32)
    half = u.shape[1] // 2
    return bits[:, :half] | (bits[:, half:] >> 16)


def _unpack_halves_f32(p):
    return pltpu.bitcast(p & jnp.uint32(0xFFFF0000), F32), pltpu.bitcast(p << 16, F32)


def _unpack_bf16_halves(p):
    hi, lo = _unpack_halves_f32(p)
    return hi.astype(BF16), lo.astype(BF16)


def _post_mixer(y_rows, x_ref, g_ref, lng_ref, lnb_ref, sc2_ref, sh2_ref, wr_ref, br_ref,
                x1_ref, u2_ref, route_ref, cnt_out_ref, cnt_ref, counted=True, n_chunks=1, before_chunk=None):
    logits = []
    for c in range(n_chunks):
        if before_chunk is not None:
            before_chunk(c)
        rows = slice(c * TM // n_chunks, (c + 1) * TM // n_chunks)
        z = DEEPNORM_ALPHA * x_ref[rows, :] + g_ref[0] * y_rows(rows)
        x1 = _layer_norm(z, lng_ref[...], lnb_ref[...])
        x1_ref[rows, :] = x1
        u2 = (x1 * (1.0 + sc2_ref[0]) + sh2_ref[0]).astype(BF16)
        u2_ref[rows, :] = _pack_bf16_halves(u2)
        logits.append(_dot(u2, wr_ref[...]) + br_ref[...])
    logits = logits[0] if n_chunks == 1 else jnp.concatenate(logits, axis=0)
    route_ref[...] = _route_tile(logits, cnt_ref, counted)
    cnt_out_ref[...] = jnp.broadcast_to(cnt_ref[...], cnt_out_ref.shape)


def _post_mixer_specs(n_batch, lag=0):
    tile = lambda i: jnp.maximum(i - lag, 0)
    grp = lambda i: (_mod_group(tile(i), n_batch), 0, 0)
    row = lambda i: (tile(i), 0)
    const = lambda i: (0, 0)
    in_specs = [pl.BlockSpec((TM, D), row),
                pl.BlockSpec((1, 1, D), grp),
                pl.BlockSpec((1, D), const),
                pl.BlockSpec((1, D), const),
                pl.BlockSpec((1, 1, D), grp),
                pl.BlockSpec((1, 1, D), grp),
                pl.BlockSpec((D, ROUTER_PAD), const),
                pl.BlockSpec((1, ROUTER_PAD), const)]
    out_specs = [pl.BlockSpec((TM, D), row),
                 pl.BlockSpec((TM, D // 2), row),
                 pl.BlockSpec((TM, ROUTER_PAD), row),
                 pl.BlockSpec((8, ROUTER_PAD), const)]
    return in_specs, out_specs


def _post_mixer_out_shape(n_rows):
    return [jax.ShapeDtypeStruct((n_rows, D), F32),
            jax.ShapeDtypeStruct((n_rows, D // 2), jnp.uint32),
            jax.ShapeDtypeStruct((n_rows, ROUTER_PAD), F32),
            jax.ShapeDtypeStruct((8, ROUTER_PAD), F32)]


def _post_mixer_scratch():
    return [pltpu.VMEM((1, ROUTER_PAD), F32)]


OUT_PROJ_CHUNKS = 4


def _out_proj_kernel(o_ref, wo_ref, *rest):
    cnt_ref, y_even_ref, y_odd_ref = rest[-3:]
    i = pl.program_id(0)

    @pl.when(i == 0)
    def _():
        cnt_ref[...] = jnp.zeros(cnt_ref.shape, F32)
        y_odd_ref[...] = jnp.zeros(y_odd_ref.shape, F32)

    def step(cur_ref, prev_ref):
        def project(c):
            cols = slice(c * D // OUT_PROJ_CHUNKS, (c + 1) * D // OUT_PROJ_CHUNKS)
            cur_ref[:, cols] = _dot(o_ref[...], wo_ref[:, cols])

        _post_mixer(lambda rows: prev_ref[rows, :], *rest[:-2], counted=i > 0,
                    n_chunks=OUT_PROJ_CHUNKS, before_chunk=project)

    @pl.when(i % 2 == 0)
    def _():
        step(y_even_ref, y_odd_ref)

    @pl.when(i % 2 == 1)
    def _():
        step(y_odd_ref, y_even_ref)


def _out_proj(o, wo, x, g1, lng, lnb, sc2, sh2, wr, br):
    n_rows = x.shape[0]
    n_batch = g1.shape[0] - 1
    n_tiles = n_rows // TM
    pm_in, pm_out = _post_mixer_specs(n_batch, lag=1)
    return pl.pallas_call(
        _out_proj_kernel,
        grid=(n_tiles + 1,),
        in_specs=[pl.BlockSpec((TM, D), lambda i: (jnp.minimum(i, n_tiles - 1), 0)),
                  pl.BlockSpec((D, D), lambda i: (0, 0))] + pm_in,
        out_specs=pm_out,
        out_shape=_post_mixer_out_shape(n_rows),
        scratch_shapes=_post_mixer_scratch() + [pltpu.VMEM((TM, D), F32), pltpu.VMEM((TM, D), F32)],
        **_opts(1, flops=2 * n_rows * D * D, nbytes=12 * n_rows * D + 2 * D * D),
        name="out_proj",
    )(o, wo, x, g1, lng, lnb, sc2, sh2, wr, br)


def _na_key_row_start(j, rows):
    return jnp.clip(NA_QROWS * j - NA_KH // 2, 0, rows - NA_KROWS)


def _na_bias_table(rpb):
    rows = L // GRID_W
    n_blocks = rows // NA_QROWS
    j = jnp.array([0, 1, n_blocks - 1])
    qr = (NA_QROWS * j)[:, None] + jnp.arange(NA_QROWS)[None, :]
    kr = _na_key_row_start(j, rows)[:, None] + jnp.arange(NA_KROWS)[None, :]
    r0 = jnp.clip(qr - NA_KH // 2, 0, rows - NA_KH)
    row_ok = (kr[:, None, :] >= r0[:, :, None]) & (kr[:, None, :] < r0[:, :, None] + NA_KH)
    dr = jnp.clip(kr[:, None, :] - qr[:, :, None] + NA_KH - 1, 0, 2 * NA_KH - 2)
    cols = jnp.arange(GRID_W)
    col_start = jnp.clip(cols - NA_KW // 2, 0, GRID_W - NA_KW)
    col_ok = (cols[None, :] >= col_start[:, None]) & (cols[None, :] < col_start[:, None] + NA_KW)
    dc = jnp.clip(cols[None, :] - cols[:, None] + NA_KW - 1, 0, 2 * NA_KW - 2)
    by_row = jnp.where(col_ok, rpb[:, :, dc] * (1.0 / SCALE), NEG_INF)
    bias = jnp.where(row_ok[None, :, :, :, None, None], by_row[:, dr], NEG_INF)
    bias = jnp.transpose(bias, (0, 1, 2, 4, 3, 5))
    return bias.reshape(NA_HEADS, 3, NA_QROWS * GRID_W, NA_KROWS * GRID_W)


NA_GROUP = 8


def _na_kernel(q_ref, k_ref, v_ref, bias_ref, o_ref):
    rows = L // GRID_W
    qb = NA_QROWS * GRID_W
    kb = NA_KROWS * GRID_W
    n_blocks = rows // NA_QROWS
    kc = k_ref[L:S, :]
    vc = v_ref[L:S, :]

    def attend(blocks):
        scores = []
        for q_rows, k_rows, bias in blocks:
            q = q_ref[q_rows, :]
            s = _dot_t(q, kc)
            if k_rows is not None:
                s = jnp.concatenate([_dot_t(q, k_ref[k_rows, :]) + bias, s], axis=1)
            scores.append(s)
        probs = []
        for s in scores:
            p = jnp.exp2((s - s.max(axis=-1, keepdims=True)) * (SCALE * LOG2E))
            probs.append((p.astype(BF16), p.sum(axis=-1, keepdims=True)))
        for (q_rows, k_rows, _), (p, den) in zip(blocks, probs):
            if k_rows is None:
                o = _dot(p, vc)
            else:
                o = _dot(p[:, :kb], v_ref[k_rows, :]) + _dot(p[:, kb:], vc)
            o_ref[q_rows, :] = (o / den).astype(o_ref.dtype)

    def step(i, carry):
        blocks = []
        for u in range(NA_GROUP):
            j = i * NA_GROUP + u
            ks = pl.multiple_of(_na_key_row_start(j, rows) * GRID_W, GRID_W)
            kind = jnp.where(j == 0, 0, jnp.where(j == n_blocks - 1, 2, 1))
            blocks.append((pl.ds(pl.multiple_of(j * qb, qb), qb), pl.ds(ks, kb), bias_ref[0, kind]))
        attend(blocks)
        return carry

    lax.fori_loop(0, n_blocks // NA_GROUP, step, 0)
    attend([(slice(L, S), None, None)])


def _na_attention(qkv, bias, n_batch):
    h = NA_HEADS
    n_blocks = bias.shape[1]
    blk = lambda off: pl.BlockSpec((S, HEAD_DIM), lambda hh, b: (b, off + hh))
    return pl.pallas_call(
        _na_kernel,
        grid=(h, n_batch),
        in_specs=[blk(0), blk(h), blk(2 * h),
                  pl.BlockSpec((1, n_blocks) + bias.shape[2:], lambda hh, b: (hh, 0, 0, 0))],
        out_specs=pl.BlockSpec((S, HEAD_DIM), lambda hh, b: (b, hh)),
        out_shape=jax.ShapeDtypeStruct((n_batch * S, h * HEAD_DIM), BF16),
        **_opts(2, flops=4 * n_batch * h * S * (NA_KROWS * GRID_W + C) * HEAD_DIM,
                nbytes=8 * n_batch * S * h * HEAD_DIM,
                transcendentals=n_batch * h * S * (NA_KROWS * GRID_W + C)),
        name="na_attn",
    )(qkv, qkv, qkv, bias)


def _swa_mask_table():
    span = SWA_BLOCK + 2 * SWA_WINDOW
    qi = jnp.arange(SWA_BLOCK)[:, None]
    kj = jnp.arange(span)[None, :]
    band = jnp.abs(qi - (kj - SWA_WINDOW)) <= SWA_WINDOW
    first = band & (kj >= SWA_WINDOW)
    last = band & (kj < SWA_BLOCK + SWA_WINDOW)
    m = jnp.stack([first, band, last]).astype(F32)
    return jnp.where(m > 0, 0.0, NEG_INF).astype(F32)


SWA_GROUP = 4


def _swa_kernel(sink_ref, q_ref, k_ref, v_ref, mask_ref, o_ref, kpad_ref, vpad_ref):
    g = pl.program_id(1)
    nb = L // SWA_BLOCK
    span = SWA_BLOCK + 2 * SWA_WINDOW
    pad = SWA_WINDOW
    zeros = jnp.zeros((pad, HEAD_DIM), BF16)
    for src, dst in ((k_ref, kpad_ref), (v_ref, vpad_ref)):
        dst[0:pad, :] = zeros
        dst[pad + L:, :] = zeros
        dst[pad:pad + L, :] = src[0:L, :]
    kc = k_ref[L:S, :]
    vc = v_ref[L:S, :]

    def head(r):
        return slice(r * HEAD_DIM, (r + 1) * HEAD_DIM)

    def attend(blocks):
        scores = []
        for rows, keys, _, mask in blocks:
            for r in range(SWA_REP):
                q = q_ref[rows, head(r)]
                s = [_dot_t(q, k) for k in keys]
                if mask is not None:
                    s[0] = s[0] + mask
                scores.append(s[0] if len(s) == 1 else jnp.concatenate(s, axis=1))
        probs = []
        for i, s in enumerate(scores):
            sink_raw = sink_ref[g * SWA_REP + i % SWA_REP] * (1.0 / SCALE)
            m = jnp.maximum(s.max(axis=-1, keepdims=True), sink_raw)
            p = jnp.exp2((s - m) * (SCALE * LOG2E))
            den = p.sum(axis=-1, keepdims=True) + jnp.exp2((sink_raw - m) * (SCALE * LOG2E))
            probs.append((p.astype(BF16), den))
        for b, (rows, _, values, _) in enumerate(blocks):
            v = values[0] if len(values) == 1 else jnp.concatenate(values, axis=0)
            for r in range(SWA_REP):
                p, den = probs[b * SWA_REP + r]
                o_ref[rows, head(r)] = (_dot(p, v) / den).astype(o_ref.dtype)

    def step(i, carry):
        blocks = []
        for u in range(SWA_GROUP):
            n = i * SWA_GROUP + u
            start = pl.multiple_of(n * SWA_BLOCK, SWA_BLOCK)
            kind = jnp.where(n == 0, 0, jnp.where(n == nb - 1, 2, 1))
            blocks.append((pl.ds(start, SWA_BLOCK), [kpad_ref[pl.ds(start, span), :], kc],
                           [vpad_ref[pl.ds(start, span), :], vc], mask_ref[kind]))
        attend(blocks)
        return carry

    lax.fori_loop(0, nb // SWA_GROUP, step, 0)
    attend([(slice(L, S), [kc], [vc], None)])


def _swa_attention(qk, v, sink, mask, n_batch):
    qw = SWA_REP * HEAD_DIM
    n_q_blocks = SWA_KV_HEADS
    return pl.pallas_call(
        _swa_kernel,
        grid_spec=pltpu.PrefetchScalarGridSpec(
            num_scalar_prefetch=1,
            grid=(n_batch, SWA_KV_HEADS),
            in_specs=[pl.BlockSpec((S, qw), lambda b, g, sk: (b, g)),
                      pl.BlockSpec((S, HEAD_DIM), lambda b, g, sk: (b, n_q_blocks * SWA_REP + g)),
                      pl.BlockSpec((S, HEAD_DIM), lambda b, g, sk: (b, g)),
                      pl.BlockSpec(mask.shape, lambda b, g, sk: (0, 0, 0))],
            out_specs=pl.BlockSpec((S, qw), lambda b, g, sk: (b, g)),
            scratch_shapes=[pltpu.VMEM((L + 2 * SWA_WINDOW, HEAD_DIM), BF16),
                            pltpu.VMEM((L + 2 * SWA_WINDOW, HEAD_DIM), BF16)]),
        out_shape=jax.ShapeDtypeStruct((n_batch * S, SWA_KV_HEADS * qw), BF16),
        **_opts(2, flops=4 * n_batch * SWA_KV_HEADS * SWA_REP * S * (SWA_BLOCK + 2 * SWA_WINDOW + C) * HEAD_DIM,
                nbytes=2 * n_batch * S * (2 * SWA_KV_HEADS * qw + 2 * SWA_KV_HEADS * HEAD_DIM),
                transcendentals=n_batch * SWA_KV_HEADS * SWA_REP * S * (SWA_BLOCK + 2 * SWA_WINDOW + C)),
        name="swa_attn",
    )(sink, qk, qk, v, mask)


DIFF_GROUP = 4


def _diff_kernel(lam_ref, g_ref, q_ref, k_ref, v_ref, o_ref, *, lambda_init, tq):
    lam = lam_ref[...]
    lmbda = (jnp.exp(jnp.sum(lam[0:1] * lam[1:2], axis=-1, keepdims=True))
             - jnp.exp(jnp.sum(lam[2:3] * lam[3:4], axis=-1, keepdims=True)) + lambda_init)
    gain = g_ref[...] * (1.0 - lambda_init)

    def softmax_parts(s):
        m = s.max(axis=-1, keepdims=True)
        e = jnp.exp2((s - m) * (SCALE * LOG2E))
        return e, e.sum(axis=-1, keepdims=True)

    def attend(q_blocks, keys):
        scores = []
        for q_rows in q_blocks:
            q = q_ref[q_rows, :]
            scores.append((_dot_t(q[:, :HEAD_DIM], k_ref[keys, 0:HEAD_DIM]),
                           _dot_t(q[:, HEAD_DIM:], k_ref[keys, HEAD_DIM:2 * HEAD_DIM])))
        probs = []
        for s1, s2 in scores:
            e1, d1 = softmax_parts(s1)
            e2, d2 = softmax_parts(s2)
            probs.append((e1 * (1.0 / d1) - e2 * (lmbda / d2)).astype(BF16))
        for q_rows, pd in zip(q_blocks, probs):
            o = _dot(pd, v_ref[keys, :])
            o = o * lax.rsqrt(jnp.mean(o * o, axis=-1, keepdims=True) + LN_EPS) * gain
            o_ref[q_rows, :] = o.astype(o_ref.dtype)

    def step(i, carry):
        attend([pl.ds(pl.multiple_of((i * DIFF_GROUP + u) * tq, tq), tq) for u in range(DIFF_GROUP)], slice(0, S))
        return carry

    lax.fori_loop(0, L // (tq * DIFF_GROUP), step, 0)
    attend([slice(L, S)], slice(L, S))


def _diff_attention(qk, v, lam, subln_g, lambda_init, n_batch):
    hw = 2 * HEAD_DIM
    return pl.pallas_call(
        functools.partial(_diff_kernel, lambda_init=lambda_init, tq=256),
        grid=(n_batch, DIFF_HEADS),
        in_specs=[pl.BlockSpec((4, HEAD_DIM), lambda b, h: (0, 0)),
                  pl.BlockSpec((1, hw), lambda b, h: (0, 0)),
                  pl.BlockSpec((S, hw), lambda b, h: (b, h)),
                  pl.BlockSpec((S, hw), lambda b, h: (b, DIFF_HEADS + h)),
                  pl.BlockSpec((S, hw), lambda b, h: (b, h))],
        out_specs=pl.BlockSpec((S, hw), lambda b, h: (b, h)),
        out_shape=jax.ShapeDtypeStruct((n_batch * S, DIFF_HEADS * hw), BF16),
        **_opts(2, flops=8 * n_batch * DIFF_HEADS * S * S * HEAD_DIM, nbytes=8 * n_batch * S * DIFF_HEADS * hw,
                transcendentals=2 * n_batch * DIFF_HEADS * S * S),
        name="diff_attn",
    )(lam, subln_g.reshape(1, hw), qk, qk, v)


CONV_ROWS = 128


def _conv_kernel(hc_ref, hp_ref, hn_ref, dw_ref, dwb_ref, cg_ref, cb_ref, wout_ref, bout_ref, *rest):
    cnt_ref, win_ref, cv_ref = rest[-3:]

    @pl.when(pl.program_id(0) == 0)
    def _():
        cnt_ref[...] = jnp.zeros(cnt_ref.shape, F32)

    t = pl.program_id(0) % TILES_PER_BATCH
    first = (t == 0) | (t == TILES_PER_BATCH - 1)
    last = t >= TILES_PER_BATCH - 2
    win_ref[0:CONV_HALO, :] = jnp.where(first, 0.0, hp_ref[...])
    win_ref[CONV_HALO:CONV_HALO + TM, :] = hc_ref[...]
    win_ref[CONV_HALO + TM:, :] = jnp.where(last, 0.0, hn_ref[...])
    base = CONV_HALO - CONV_WIDTH // 2
    sub = 8

    def strip(c, carry):
        cs = pl.ds(pl.multiple_of(c * HEAD_DIM, HEAD_DIM), HEAD_DIM)
        for r in range(TM // CONV_ROWS):
            acc = jnp.zeros((CONV_ROWS, HEAD_DIM), F32)
            aligned = win_ref[pl.ds(r * CONV_ROWS, CONV_ROWS + 2 * CONV_HALO), cs]
            for res in range(sub):
                taps = [j for j in range(CONV_WIDTH) if (base + j) % sub == res]
                w = aligned if res == 0 else pltpu.roll(aligned, aligned.shape[0] - res, 0)
                for j in taps:
                    lo = (base + j) // sub * sub
                    acc = acc + dw_ref[j:j + 1, cs] * w[lo:lo + CONV_ROWS]
            cv_ref[r * CONV_ROWS:(r + 1) * CONV_ROWS, cs] = acc
        return carry

    lax.fori_loop(0, D // HEAD_DIM, strip, 0)

    hn = _layer_norm(cv_ref[...] + dwb_ref[...], cg_ref[...], cb_ref[...])
    hn = (hn * _sigmoid(hn)).astype(BF16)
    y = _dot(hn, wout_ref[...]) + bout_ref[...]
    _post_mixer(lambda rows: y[rows, :], *rest[:-2])


def _conv_mixer(h, dw, dw_b, cg, cb, wout, bout, x, g1, lng, lnb, sc2, sh2, wr, br):
    n_rows = x.shape[0]
    n_batch = g1.shape[0] - 1
    n_tiles = n_rows // TM
    halo_per_tile = TM // CONV_HALO
    n_halo = n_rows // CONV_HALO
    pm_in, pm_out = _post_mixer_specs(n_batch)
    const = lambda i: (0, 0)
    vec = lambda a: a.reshape(1, D)
    dw_pad = jnp.concatenate([dw, jnp.zeros((1, D), dw.dtype)], axis=0)
    return pl.pallas_call(
        _conv_kernel,
        grid=(n_tiles,),
        in_specs=[pl.BlockSpec((TM, D), lambda i: (i, 0)),
                  pl.BlockSpec((CONV_HALO, D), lambda i: (jnp.maximum(i * halo_per_tile - 1, 0), 0)),
                  pl.BlockSpec((CONV_HALO, D), lambda i: (jnp.minimum((i + 1) * halo_per_tile, n_halo - 1), 0)),
                  pl.BlockSpec((CONV_WIDTH + 1, D), const),
                  pl.BlockSpec((1, D), const),
                  pl.BlockSpec((1, D), const),
                  pl.BlockSpec((1, D), const),
                  pl.BlockSpec((D, D), const),
                  pl.BlockSpec((1, D), const)] + pm_in,
        out_specs=pm_out,
        out_shape=_post_mixer_out_shape(n_rows),
        scratch_shapes=_post_mixer_scratch() + [pltpu.VMEM((TM + 2 * CONV_HALO, D), F32),
                                                pltpu.VMEM((TM, D), F32)],
        **_opts(1, flops=2 * n_rows * D * (D + CONV_WIDTH), nbytes=14 * n_rows * D + 2 * D * D),
        name="conv_mixer",
    )(h, h, h, dw_pad, vec(dw_b), vec(cg), vec(cb), wout, vec(bout), x, g1, lng, lnb, sc2, sh2, wr, br)


def _moe_kernel(be_ref, nxt_ref, nu_ref, xs_ref, w13_hbm, w2_hbm, ys_ref,
                w13f_ref, w2f_ref, w13b_ref, w2b_ref, sem):
    i = pl.program_id(0)

    def weight_copies(e):
        return (pltpu.make_async_copy(w13_hbm.at[e], w13f_ref, sem.at[0]),
                pltpu.make_async_copy(w2_hbm.at[e], w2f_ref, sem.at[1]))

    @pl.when(i >= nu_ref[0])
    def _():
        ys_ref[...] = jnp.zeros(ys_ref.shape, ys_ref.dtype)

    @pl.when(i < nu_ref[0])
    def _():
        @pl.when(i == 0)
        def _():
            for cp in weight_copies(be_ref[0]):
                cp.start()

        changed = (i == 0) | (be_ref[i] != be_ref[jnp.maximum(i - 1, 0)])

        @pl.when(changed)
        def _():
            for cp in weight_copies(be_ref[i]):
                cp.wait()
            w13b_ref[...] = w13f_ref[...].astype(BF16)
            w2b_ref[...] = w2f_ref[...].astype(BF16)

            @pl.when(nxt_ref[i] >= 0)
            def _():
                for cp in weight_copies(nxt_ref[i]):
                    cp.start()

        x_hi, x_lo = _unpack_bf16_halves(xs_ref[...])
        a = _dot(x_hi, w13b_ref[0:D // 2, :]) + _dot(x_lo, w13b_ref[D // 2:, :])
        gate = a[:, :MOE_D_FF]
        hmid = (gate * _sigmoid(gate) * a[:, MOE_D_FF:]).astype(BF16)
        ys_ref[...] = _pack_bf16_halves(_dot(hmid, w2b_ref[...]).astype(BF16))


def _next_segment_expert(blk_expert, n_used):
    n_blocks = blk_expert.shape[0]
    j = jnp.arange(n_blocks, dtype=jnp.int32)
    later = (j[None, :] > j[:, None]) & (blk_expert[None, :] != blk_expert[:, None]) & (j[None, :] < n_used[0])
    first = jnp.argmax(later, axis=1)
    return jnp.where(jnp.any(later, axis=1), blk_expert[first], -1).astype(jnp.int32)


def _moe_experts(xs, blk_expert, n_used, w13, w2):
    n_slots = xs.shape[0]
    n_blocks = n_slots // MOE_BLOCK
    live = lambda i, nu: jnp.minimum(i, nu[0] - 1)
    return pl.pallas_call(
        _moe_kernel,
        grid_spec=pltpu.PrefetchScalarGridSpec(
            num_scalar_prefetch=3,
            grid=(n_blocks,),
            in_specs=[pl.BlockSpec((MOE_BLOCK, D // 2), lambda i, be, nxt, nu: (live(i, nu), 0)),
                      pl.BlockSpec(memory_space=pl.ANY),
                      pl.BlockSpec(memory_space=pl.ANY)],
            out_specs=pl.BlockSpec((MOE_BLOCK, D // 2), lambda i, be, nxt, nu: (i, 0)),
            scratch_shapes=[pltpu.VMEM((D, 2 * MOE_D_FF), F32),
                            pltpu.VMEM((MOE_D_FF, D), F32),
                            pltpu.VMEM((D, 2 * MOE_D_FF), BF16),
                            pltpu.VMEM((MOE_D_FF, D), BF16),
                            pltpu.SemaphoreType.DMA((2,))]),
        out_shape=jax.ShapeDtypeStruct((n_slots, D // 2), jnp.uint32),
        **_opts(1, flops=6 * n_slots * D * MOE_D_FF, nbytes=4 * n_slots * D + 12 * MOE_EXPERTS * D * MOE_D_FF),
        name="moe_experts",
    )(blk_expert, _next_segment_expert(blk_expert, n_used), n_used, xs, w13, w2)


def _combine_kernel(x_ref, ya_ref, yb_ref, route_ref, g_ref, lng_ref, lnb_ref, o_ref):
    route = route_ref[...]
    gate_a = route[:, ROUTE_GATE:ROUTE_GATE + 1]
    gate_b = route[:, ROUTE_GATE + 1:ROUTE_GATE + 2]
    a_hi, a_lo = _unpack_halves_f32(ya_ref[...])
    b_hi, b_lo = _unpack_halves_f32(yb_ref[...])
    f = jnp.concatenate([a_hi * gate_a + b_hi * gate_b, a_lo * gate_a + b_lo * gate_b], axis=1)
    z = DEEPNORM_ALPHA * x_ref[...] + g_ref[0] * f
    o_ref[...] = _layer_norm(z, lng_ref[...], lnb_ref[...])


def _combine_latent_kernel(*refs):
    @pl.when(pl.program_id(0) % TILES_PER_BATCH < LAT_TILES)
    def _():
        _combine_kernel(*refs)


def _combine(x1, ya, yb, route, g2, lng, lnb, latent_only):
    n_rows = x1.shape[0]
    n_batch = g2.shape[0] - 1
    row = pl.BlockSpec((TM, D), lambda i: (i, 0))
    packed = pl.BlockSpec((TM, D // 2), lambda i: (i, 0))
    const = pl.BlockSpec((1, D), lambda i: (0, 0))
    if latent_only:
        out_rows = n_batch * L
        out_spec = pl.BlockSpec((TM, D), lambda i: (
            i // TILES_PER_BATCH * LAT_TILES + jnp.minimum(i % TILES_PER_BATCH, LAT_TILES - 1), 0))
    else:
        out_rows, out_spec = n_rows, row
    return pl.pallas_call(
        _combine_latent_kernel if latent_only else _combine_kernel,
        grid=(n_rows // TM,),
        in_specs=[row, packed, packed,
                  pl.BlockSpec((TM, ROUTER_PAD), lambda i: (i, 0)),
                  pl.BlockSpec((1, 1, D), lambda i: (_mod_group(i, n_batch), 0, 0)),
                  const, const],
        out_specs=out_spec,
        out_shape=jax.ShapeDtypeStruct((out_rows, D), F32),
        **_opts(1, flops=10 * n_rows * D, nbytes=16 * n_rows * D),
        name="moe_combine",
    )(x1, ya, yb, route, g2, lng, lnb)


def _dispatch_plan(route, counts):
    n = route.shape[0]
    a = n * MOE_TOP_K
    experts = jnp.arange(MOE_EXPERTS, dtype=jnp.int32)
    counts = counts[0, MOE_GROUPS:MOE_GROUPS + MOE_EXPERTS].astype(jnp.int32)
    padded = (counts + MOE_BLOCK - 1) // MOE_BLOCK * MOE_BLOCK
    pad_end = jnp.cumsum(padded)
    pad_start = pad_end - padded
    expert = route[:, ROUTE_EXPERT:ROUTE_EXPERT + MOE_TOP_K].astype(jnp.int32)
    rank = route[:, ROUTE_RANK:ROUTE_RANK + MOE_TOP_K].astype(jnp.int32)
    slot = jnp.sum(jnp.where(expert[:, :, None] == experts, pad_start, 0), axis=-1) + rank
    n_blocks = -(-a // MOE_BLOCK) + MOE_EXPERTS
    tok_of = jnp.broadcast_to(jnp.arange(n, dtype=jnp.int32)[:, None], (n, MOE_TOP_K))
    tok = (jnp.arange(n_blocks * MOE_BLOCK, dtype=jnp.int32) % n).at[slot.reshape(a)].set(
        tok_of.reshape(a), unique_indices=True, mode='promise_in_bounds')
    blk_start = jnp.arange(n_blocks, dtype=jnp.int32) * MOE_BLOCK
    blk_expert = jnp.minimum(jnp.sum((pad_end[None, :] <= blk_start[:, None]).astype(jnp.int32), axis=-1),
                             MOE_EXPERTS - 1)
    n_used = (pad_end[-1] // MOE_BLOCK).reshape(1)
    return slot, tok, blk_expert, n_used


def _take_rows(a, idx):
    return a.at[idx].get(mode='promise_in_bounds')


def _moe_layer(x1, u2, route, counts, g2, lng, lnb, w13, w2, latent_only):
    slot, tok, blk_expert, n_used = _dispatch_plan(route, counts)
    ys = _moe_experts(_take_rows(u2, tok), blk_expert, n_used, w13, w2)
    return _combine(x1, _take_rows(ys, slot[:, 0]), _take_rows(ys, slot[:, 1]), route, g2, lng, lnb, latent_only)


def _rope_tables():
    t = jnp.arange(L, dtype=jnp.int32)
    pos = jnp.stack([t // GRID_W, t % GRID_W], -1).astype(F32)
    n_freq = HEAD_DIM // 4
    inv_freq = ROPE_THETA ** (-jnp.arange(n_freq, dtype=F32) / n_freq)
    ang = pos[:, :, None] * inv_freq
    cos, sin = jnp.cos(ang), jnp.sin(ang)
    cos_t = jnp.stack([cos, cos], axis=2).reshape(L, HEAD_DIM)
    sin_t = jnp.stack([-sin, sin], axis=2).reshape(L, HEAD_DIM)
    cos_t = jnp.concatenate([cos_t, jnp.ones((C, HEAD_DIM), F32)], axis=0)
    sin_t = jnp.concatenate([sin_t, jnp.zeros((C, HEAD_DIM), F32)], axis=0)
    return cos_t, sin_t


def _router_params(rg_w, rg_b, re_w, re_b):
    n = MOE_GROUPS + MOE_EXPERTS
    w = jnp.concatenate([rg_w, re_w, jnp.zeros((D, ROUTER_PAD - n), F32)], axis=1).astype(BF16)
    b = jnp.concatenate([rg_b, re_b, jnp.zeros((ROUTER_PAD - n,), F32)]).reshape(1, ROUTER_PAD)
    return w, b


def _modulation_inputs(c, c_ctx):
    n = c.shape[0] + 1
    pad = -n % 8
    return jnp.concatenate([c, c_ctx[None, :], jnp.zeros((pad, D), F32)], axis=0)


def _mixer_fn(idx, mixer):
    kind = idx % 4
    if kind == 0:
        w_qkv, rpb, w_o = mixer
        w_o, bias = w_o.astype(BF16), _na_bias_table(rpb)

        def run(xs, sc1, sh1, n_batch, post):
            qkv = _proj(xs, sc1, sh1, w_qkv, n_out=3 * D, tn=D, out_dtype=BF16)
            return _out_proj(_na_attention(qkv, bias, n_batch), w_o, *post)
    elif kind == 1:
        w_in, b_in, dw, dw_b, cg, cb, w_out, b_out = mixer
        w_out = w_out.astype(BF16)

        def run(xs, sc1, sh1, n_batch, post):
            h = _proj(xs, sc1, sh1, w_in, n_out=D, tn=D // 2, out_dtype=F32, bias=b_in, glu=True)
            return _conv_mixer(h, dw, dw_b, cg, cb, w_out, b_out, *post)
    elif kind == 2:
        w_qkv, sink, w_o = mixer
        n_qk = (SWA_KV_HEADS * SWA_REP + SWA_KV_HEADS) * HEAD_DIM
        n_v = SWA_KV_HEADS * HEAD_DIM
        w_o, rope, mask = w_o.astype(BF16), _rope_tables(), _swa_mask_table()

        def run(xs, sc1, sh1, n_batch, post):
            qk = _proj(xs, sc1, sh1, w_qkv, n_out=n_qk, tn=n_qk, out_dtype=BF16, rope=rope)
            v = _proj(xs, sc1, sh1, w_qkv, n_out=n_v, tn=n_v, col_off=n_qk // n_v, out_dtype=BF16)
            return _out_proj(_swa_attention(qk, v, sink, mask, n_batch), w_o, *post)
    else:
        w_qkv, lam, subln_g, w_o = mixer
        lambda_init = 0.8 - 0.6 * math.exp(-0.3 * idx)
        w_o, rope = w_o.astype(BF16), _rope_tables()

        def run(xs, sc1, sh1, n_batch, post):
            qk = _proj(xs, sc1, sh1, w_qkv, n_out=2 * D, tn=D, out_dtype=BF16, rope=rope)
            v = _proj(xs, sc1, sh1, w_qkv, n_out=D, tn=D, col_off=2, out_dtype=BF16)
            return _out_proj(_diff_attention(qk, v, lam, subln_g, lambda_init, n_batch), w_o, *post)
    return run


def _hybrid_layer(idx, streams, cvec, mod_w, mod_b, mixer, ln1_g, ln1_b, moe, ln2_g, ln2_b, last=False):
    vec = lambda a: a.reshape(1, D)
    m_all = _adaln(cvec, mod_w, mod_b)
    ctx_row = sum(nb for _, _, nb in streams)
    rg_w, rg_b, re_w, re_b, w13, w2 = moe
    wr, br = _router_params(rg_w, rg_b, re_w, re_b)
    run_mixer = _mixer_fn(idx, mixer)
    out = []
    for xs, b0, nb in streams:
        m = jnp.concatenate([m_all[b0:b0 + nb], m_all[ctx_row:ctx_row + 1]], axis=0)
        sh1, sc1, g1, sh2, sc2, g2 = [m[:, None, k * D:(k + 1) * D] for k in range(6)]
        post = (xs, g1, vec(ln1_g), vec(ln1_b), sc2, sh2, wr, br)
        routed = run_mixer(xs, sc1, sh1, nb, post)
        out.append((_moe_layer(*routed, g2, vec(ln2_g), vec(ln2_b), w13, w2, latent_only=last), b0, nb))
    return out


def kernel(x, c, ctx, c_ctx, l0_mod_w, l0_mod_b, l0_na_w_qkv, l0_na_rpb, l0_na_w_o, l0_ln1_g, l0_ln1_b, l0_router_g_w, l0_router_g_b, l0_router_e_w, l0_router_e_b, l0_moe_w13, l0_moe_w2, l0_ln2_g, l0_ln2_b, l1_mod_w, l1_mod_b, l1_cv_w_in, l1_cv_b_in, l1_cv_dw, l1_cv_dw_b, l1_cv_ln_g, l1_cv_ln_b, l1_cv_w_out, l1_cv_b_out, l1_ln1_g, l1_ln1_b, l1_router_g_w, l1_router_g_b, l1_router_e_w, l1_router_e_b, l1_moe_w13, l1_moe_w2, l1_ln2_g, l1_ln2_b, l2_mod_w, l2_mod_b, l2_sw_w_qkv, l2_sw_sink, l2_sw_w_o, l2_ln1_g, l2_ln1_b, l2_router_g_w, l2_router_g_b, l2_router_e_w, l2_router_e_b, l2_moe_w13, l2_moe_w2, l2_ln2_g, l2_ln2_b, l3_mod_w, l3_mod_b, l3_df_w_qkv, l3_df_lambda, l3_df_subln_g, l3_df_w_o, l3_ln1_g, l3_ln1_b, l3_router_g_w, l3_router_g_b, l3_router_e_w, l3_router_e_b, l3_moe_w13, l3_moe_w2, l3_ln2_g, l3_ln2_b):
    layers = (
        (l0_mod_w, l0_mod_b, (l0_na_w_qkv, l0_na_rpb, l0_na_w_o), l0_ln1_g, l0_ln1_b,
         (l0_router_g_w, l0_router_g_b, l0_router_e_w, l0_router_e_b, l0_moe_w13, l0_moe_w2), l0_ln2_g, l0_ln2_b),
        (l1_mod_w, l1_mod_b, (l1_cv_w_in, l1_cv_b_in, l1_cv_dw, l1_cv_dw_b, l1_cv_ln_g, l1_cv_ln_b, l1_cv_w_out,
                              l1_cv_b_out), l1_ln1_g, l1_ln1_b,
         (l1_router_g_w, l1_router_g_b, l1_router_e_w, l1_router_e_b, l1_moe_w13, l1_moe_w2), l1_ln2_g, l1_ln2_b),
        (l2_mod_w, l2_mod_b, (l2_sw_w_qkv, l2_sw_sink, l2_sw_w_o), l2_ln1_g, l2_ln1_b,
         (l2_router_g_w, l2_router_g_b, l2_router_e_w, l2_router_e_b, l2_moe_w13, l2_moe_w2), l2_ln2_g, l2_ln2_b),
        (l3_mod_w, l3_mod_b, (l3_df_w_qkv, l3_df_lambda, l3_df_subln_g, l3_df_w_o), l3_ln1_g, l3_ln1_b,
         (l3_router_g_w, l3_router_g_b, l3_router_e_w, l3_router_e_b, l3_moe_w13, l3_moe_w2), l3_ln2_g, l3_ln2_b),
    )
    n_batch = x.shape[0]
    assert x.shape[1:] == (L, D) and ctx.shape[1:] == (C, D)
    n_streams = N_STREAMS if n_batch % N_STREAMS == 0 else 1
    nb = n_batch // n_streams
    streams = [(jnp.concatenate([x[b0:b0 + nb], ctx[b0:b0 + nb]], axis=1).reshape(nb * S, D), b0, nb)
               for b0 in range(0, n_batch, nb)]
    cvec = _modulation_inputs(c, c_ctx)
    for idx in range(DEPTH):
        streams = _hybrid_layer(idx, streams, cvec, *layers[idx], last=idx == DEPTH - 1)
    out = [xs.reshape(nb, L, D) for xs, _, _ in streams]
    return out[0] if len(out) == 1 else jnp.concatenate(out, axis=0)
```
